```python
import jax
import jax.numpy as jnp
from jax import lax
import numpy as np

D_MODEL = 2048
BATCH = 4
SEQ = 2048
DEPTH = 1

HG_KEY_DIM = 128
HG_VAL_DIM = 128
HG_WIDTH = D_MODEL // 2
HG_HEADS = HG_WIDTH // HG_VAL_DIM
HG_KEY_WIDTH = HG_HEADS * HG_KEY_DIM
HG_CHUNK = 64

RG_WIDTH = D_MODEL - HG_WIDTH
RG_BLOCKS = 8
RG_BLOCK_DIM = RG_WIDTH // RG_BLOCKS
RG_CONV = 4
RG_C = 8.0

MIX_WIDTH = HG_WIDTH + RG_WIDTH
IN_WIDTHS = (HG_KEY_WIDTH, HG_KEY_WIDTH, HG_KEY_WIDTH, HG_WIDTH, HG_WIDTH, RG_WIDTH, RG_WIDTH)
IN_WIDTH = sum(IN_WIDTHS)

N_EXPERTS = 256
TOP_K = 8
N_GROUPS = 8
TOPK_GROUPS = 4
EXPERT_FF = 512
SHARED_FF = 512
ROUTE_SCALE = 2.5
MOE_BLOCK = 128

DN_ALPHA = (2 * DEPTH) ** 0.25
DN_BETA = (8 * DEPTH) ** -0.25
LN_EPS = 1e-5
RMS_EPS = 1e-6

kernel_name = 'hybrid_hgrn2_rglru_moe_encoder_layer'


def layer_norm(x, g, b):
    xf = x.astype(jnp.float32)
    mu = jnp.mean(xf, axis=-1, keepdims=True)
    var = jnp.mean(jnp.square(xf - mu), axis=-1, keepdims=True)
    y = (xf - mu) * lax.rsqrt(var + LN_EPS) * g.astype(jnp.float32) + b.astype(jnp.float32)
    return y.astype(x.dtype)


def hgrn2_chunk_scan(q, k, v, log_f):
    bsz, seq, heads, dk = q.shape
    dv = v.shape[-1]
    n_chunks = seq // HG_CHUNK

    def to_chunks(t):
        return t.reshape(bsz, n_chunks, HG_CHUNK, heads, t.shape[-1]).transpose(1, 0, 3, 2, 4)

    causal = jnp.tril(jnp.ones((HG_CHUNK, HG_CHUNK), dtype=bool))[:, :, None]

    def step(state, inp):
        qc, kc, vc, fc = inp
        cum = jnp.cumsum(fc, axis=2)
        rel = cum[:, :, :, None, :] - cum[:, :, None, :, :]
        decay = jnp.exp(jnp.where(causal, rel, -jnp.inf))
        scores = jnp.einsum('bhtd,bhsd,bhtsd->bhts', qc, kc, decay)
        o_intra = jnp.einsum('bhts,bhsv->bhtv', scores, vc)
        o_inter = jnp.einsum('bhtd,bhdv->bhtv', qc * jnp.exp(cum), state)
        last = cum[:, :, -1:, :]
        state = state * jnp.exp(last[:, :, 0, :])[..., None] + jnp.einsum(
            'bhsd,bhsv->bhdv', kc * jnp.exp(last - cum), vc)
        return state, o_intra + o_inter

    state0 = jnp.zeros((bsz, heads, dk, dv), jnp.float32)
    _, o = lax.scan(step, state0, (to_chunks(q), to_chunks(k), to_chunks(v), to_chunks(log_f)))
    return o.transpose(1, 0, 3, 2, 4).reshape(bsz, seq, heads, dv)


def rglru(xc, w_a, b_a, w_x, b_x, lam, reverse):
    bsz, seq, width = xc.shape
    xb = xc.reshape(bsz, seq, RG_BLOCKS, RG_BLOCK_DIM)
    r = jax.nn.sigmoid(jnp.einsum('bsni,nij->bsnj', xb, w_a.astype(jnp.float32)).reshape(bsz, seq, width)
                       + b_a.astype(jnp.float32))
    i = jax.nn.sigmoid(jnp.einsum('bsni,nij->bsnj', xb, w_x.astype(jnp.float32)).reshape(bsz, seq, width)
                       + b_x.astype(jnp.float32))
    log_a = -RG_C * r * jax.nn.softplus(-lam.astype(jnp.float32))
    a = jnp.exp(log_a)
    u = jnp.sqrt(-jnp.expm1(2.0 * log_a)) * (i * xc)

    def combine(left, right):
        a_l, h_l = left
        a_r, h_r = right
        return a_l * a_r, a_r * h_l + h_r

    _, h = lax.associative_scan(combine, (a, u), reverse=reverse, axis=1)
    return h


def token_mixer(h, w_in, lb_f, lb_b, hg_norm_w, conv_w, conv_b, rg_wa, rg_ba, rg_wx, rg_bx, rg_lam, w_out):
    bsz, seq, _ = h.shape
    f32 = jnp.float32
    splits = np.cumsum(IN_WIDTHS)[:-1].tolist()
    proj = h @ w_in
    q, zf, zb, v, g, xr, gr = jnp.split(proj, splits, axis=-1)

    def heads(t, d):
        return t.astype(f32).reshape(bsz, seq, HG_HEADS, d)

    qh = heads(jax.nn.silu(q), HG_KEY_DIM)
    vh = heads(v, HG_VAL_DIM)

    def forget(z, lb):
        z = heads(z, HG_KEY_DIM)
        lb = lb.reshape(HG_HEADS, HG_KEY_DIM)
        log_f = jnp.log(lb + (1.0 - lb) * jax.nn.sigmoid(z))
        k = (1.0 - lb) * jax.nn.sigmoid(-z)
        return log_f, k

    logf_f, k_f = forget(zf, lb_f)
    logf_b, k_b = forget(zb, lb_b)

    def rev(t):
        return jnp.flip(t, axis=1)

    o = hgrn2_chunk_scan(qh, k_f, vh, logf_f) + rev(hgrn2_chunk_scan(rev(qh), rev(k_b), rev(vh), rev(logf_b)))
    o = o * lax.rsqrt(jnp.mean(o * o, axis=-1, keepdims=True) + RMS_EPS) * hg_norm_w.astype(f32)
    y_hg = o.reshape(bsz, seq, HG_WIDTH).astype(h.dtype) * jax.nn.silu(g)

    pad_l = RG_CONV // 2
    xp = jnp.pad(xr, ((0, 0), (pad_l, RG_CONV - 1 - pad_l), (0, 0)))
    xc = conv_b
    for j in range(RG_CONV):
        xc = xc + xp[:, j:j + seq] * conv_w[j]
    xc = xc.astype(f32)
    h_rg = (rglru(xc, rg_wa[0], rg_ba[0], rg_wx[0], rg_bx[0], rg_lam[0], reverse=False)
            + rglru(xc, rg_wa[1], rg_ba[1], rg_wx[1], rg_bx[1], rg_lam[1], reverse=True))
    y_rg = h_rg.astype(h.dtype) * jax.nn.gelu(gr)

    return jnp.concatenate([y_hg, y_rg], axis=-1) @ w_out


def moe_ffn(h, layer, w_router, router_bias, w_gate, w_up, w_down, ws_gate, ws_up, ws_down):
    bsz, seq, d = h.shape
    xt = h.reshape(-1, d)
    n_tok = xt.shape[0]
    n_assign = n_tok * TOP_K

    scores = jax.nn.sigmoid((xt @ w_router).astype(jnp.float32))
    choice = scores + router_bias.astype(jnp.float32)
    group_score = lax.top_k(choice.reshape(n_tok, N_GROUPS, -1), 2)[0].sum(-1)
    _, top_groups = lax.top_k(group_score, TOPK_GROUPS)
    group_mask = jnp.any(top_groups[:, :, None] == jnp.arange(N_GROUPS)[None, None, :], axis=1)
    expert_mask = jnp.repeat(group_mask, N_EXPERTS // N_GROUPS, axis=1)
    _, top_idx = lax.top_k(jnp.where(expert_mask, choice, -jnp.inf), TOP_K)
    gate = jnp.take_along_axis(scores, top_idx, axis=1)
    gate = gate / jnp.sum(gate, axis=-1, keepdims=True) * ROUTE_SCALE

    flat_e = top_idx.reshape(-1).astype(jnp.int32)
    flat_tok = jnp.repeat(jnp.arange(n_tok, dtype=jnp.int32), TOP_K)
    order = jnp.argsort(flat_e)
    sorted_e = flat_e[order]
    counts = jnp.bincount(flat_e, length=N_EXPERTS)
    starts = jnp.cumsum(counts) - counts
    padded = (counts + MOE_BLOCK - 1) // MOE_BLOCK * MOE_BLOCK
    padded_end = jnp.cumsum(padded)
    padded_start = padded_end - padded
    dest = padded_start[sorted_e] + jnp.arange(n_assign, dtype=jnp.int32) - starts[sorted_e]
    n_blocks = -(-(n_assign + N_EXPERTS * (MOE_BLOCK - 1)) // MOE_BLOCK)
    n_rows = n_blocks * MOE_BLOCK
    row_tok = jnp.zeros((n_rows,), jnp.int32).at[dest].set(flat_tok[order])
    row_gate = jnp.zeros((n_rows,), jnp.float32).at[dest].set(gate.reshape(-1)[order])
    block_expert = jnp.minimum(
        jnp.searchsorted(padded_end, jnp.arange(n_blocks, dtype=jnp.int32) * MOE_BLOCK, side='right'),
        N_EXPERTS - 1)

    def expert_block(args):
        tok, g, e = args
        xb = xt[tok]
        hid = jax.nn.silu(xb @ w_gate[layer, e]) * (xb @ w_up[layer, e])
        return (hid @ w_down[layer, e]).astype(jnp.float32) * g[:, None]

    y_blocks = lax.map(expert_block, (row_tok.reshape(n_blocks, MOE_BLOCK),
                                      row_gate.reshape(n_blocks, MOE_BLOCK), block_expert))
    routed = jnp.zeros((n_tok, d), jnp.float32).at[row_tok].add(y_blocks.reshape(n_rows, d))
    shared = (jax.nn.silu(xt @ ws_gate) * (xt @ ws_up)) @ ws_down
    return (routed.astype(h.dtype) + shared).reshape(bsz, seq, d)


def setup_inputs(seed: int = 0) -> dict:
    key = jax.random.key(seed)
    ks = jax.random.split(key, 32)
    f32 = jnp.float32

    def nrm(i, shape, scale):
        return jax.random.normal(ks[i], shape, f32) * scale

    u = jax.random.uniform(ks[12], (DEPTH, 2, RG_WIDTH), f32, 0.9, 0.999)
    s = u ** (1.0 / RG_C)
    rg_lam = jnp.log(s) - jnp.log1p(-s)
    return {
        'x': nrm(0, (BATCH, SEQ, D_MODEL), 1.0),
        'ln_emb_g': 1.0 + nrm(1, (D_MODEL,), 0.02),
        'ln_emb_b': nrm(2, (D_MODEL,), 0.02),
        'w_in': nrm(3, (DEPTH, D_MODEL, IN_WIDTH), D_MODEL ** -0.5),
        'lb_logits': nrm(4, (2, DEPTH + 1, HG_KEY_WIDTH), 0.5),
        'hg_norm_w': 1.0 + nrm(5, (DEPTH, HG_VAL_DIM), 0.02),
        'conv_w': nrm(6, (DEPTH, RG_CONV, RG_WIDTH), RG_CONV ** -0.5),
        'conv_b': nrm(7, (DEPTH, RG_WIDTH), 0.02),
        'rg_wa': nrm(8, (DEPTH, 2, RG_BLOCKS, RG_BLOCK_DIM, RG_BLOCK_DIM), RG_BLOCK_DIM ** -0.5),
        'rg_ba': nrm(9, (DEPTH, 2, RG_WIDTH), 0.02),
        'rg_wx': nrm(10, (DEPTH, 2, RG_BLOCKS, RG_BLOCK_DIM, RG_BLOCK_DIM), RG_BLOCK_DIM ** -0.5),
        'rg_bx': nrm(11, (DEPTH, 2, RG_WIDTH), 0.02),
        'rg_lam': rg_lam,
        'w_out': nrm(13, (DEPTH, MIX_WIDTH, D_MODEL), MIX_WIDTH ** -0.5 * DN_BETA),
        'ln1_g': 1.0 + nrm(14, (DEPTH, D_MODEL), 0.02),
        'ln1_b': nrm(15, (DEPTH, D_MODEL), 0.02),
        'w_router': nrm(16, (DEPTH, D_MODEL, N_EXPERTS), D_MODEL ** -0.5),
        'router_bias': nrm(17, (DEPTH, N_EXPERTS), 0.01),
        'w_gate': nrm(18, (DEPTH, N_EXPERTS, D_MODEL, EXPERT_FF), D_MODEL ** -0.5),
        'w_up': nrm(19, (DEPTH, N_EXPERTS, D_MODEL, EXPERT_FF), D_MODEL ** -0.5),
        'w_down': nrm(20, (DEPTH, N_EXPERTS, EXPERT_FF, D_MODEL), EXPERT_FF ** -0.5 * DN_BETA),
        'ws_gate': nrm(21, (DEPTH, D_MODEL, SHARED_FF), D_MODEL ** -0.5),
        'ws_up': nrm(22, (DEPTH, D_MODEL, SHARED_FF), D_MODEL ** -0.5),
        'ws_down': nrm(23, (DEPTH, SHARED_FF, D_MODEL), SHARED_FF ** -0.5 * DN_BETA),
        'ln2_g': 1.0 + nrm(24, (DEPTH, D_MODEL), 0.02),
        'ln2_b': nrm(25, (DEPTH, D_MODEL), 0.02),
    }


def reference(x, ln_emb_g, ln_emb_b, w_in, lb_logits, hg_norm_w, conv_w, conv_b, rg_wa, rg_ba, rg_wx,
              rg_bx, rg_lam, w_out, ln1_g, ln1_b, w_router, router_bias, w_gate, w_up, w_down,
              ws_gate, ws_up, ws_down, ln2_g, ln2_b):
    lb = jnp.cumsum(jax.nn.softmax(lb_logits.astype(jnp.float32), axis=1), axis=1)
    h = layer_norm(x, ln_emb_g, ln_emb_b)
    for layer in range(DEPTH):
        mix = token_mixer(h, w_in[layer], lb[0, layer], lb[1, layer], hg_norm_w[layer], conv_w[layer],
                          conv_b[layer], rg_wa[layer], rg_ba[layer], rg_wx[layer], rg_bx[layer],
                          rg_lam[layer], w_out[layer])
        h = layer_norm(DN_ALPHA * h + mix, ln1_g[layer], ln1_b[layer])
        ffn = moe_ffn(h, layer, w_router[layer], router_bias[layer], w_gate, w_up, w_down,
                      ws_gate[layer], ws_up[layer], ws_down[layer])
        h = layer_norm(DN_ALPHA * h + ffn, ln2_g[layer], ln2_b[layer])
    return h
```

```python
import functools

import numpy as np
import jax
import jax.numpy as jnp
from jax import lax
from jax.experimental import pallas as pl
from jax.experimental.pallas import tpu as pltpu

F32 = jnp.float32
BF16 = jnp.bfloat16
U32 = jnp.uint32
I32 = jnp.int32

D_MODEL = 2048
DEPTH = 1
HG_KEY_DIM = 128
HG_VAL_DIM = 128
HG_WIDTH = D_MODEL // 2
HG_HEADS = HG_WIDTH // HG_VAL_DIM
HG_KEY_WIDTH = HG_HEADS * HG_KEY_DIM
RG_WIDTH = D_MODEL - HG_WIDTH
RG_BLOCKS = 8
RG_BLOCK_DIM = RG_WIDTH // RG_BLOCKS
RG_CONV = 4
RG_C = 8.0
MIX_WIDTH = HG_WIDTH + RG_WIDTH
IN_WIDTHS = (HG_KEY_WIDTH, HG_KEY_WIDTH, HG_KEY_WIDTH, HG_WIDTH, HG_WIDTH, RG_WIDTH, RG_WIDTH)
IN_WIDTH = sum(IN_WIDTHS)
N_EXPERTS = 256
TOP_K = 8
N_GROUPS = 8
TOPK_GROUPS = 4
GROUP_SIZE = N_EXPERTS // N_GROUPS
EXPERT_FF = 512
SHARED_FF = 512
ROUTE_SCALE = 2.5
MOE_BLOCK = 128
DN_ALPHA = (2 * DEPTH) ** 0.25
LN_EPS = 1e-5
RMS_EPS = 1e-6

LANES = 128
SUBLANES = 8
VMEM_LIMIT_BYTES = 56 * 1024 * 1024

HG_CHUNK = 128
HG_SUB = 8
HG_LEVELS = (16, 32, 64, 128)


def _cparams(sem, vmem=VMEM_LIMIT_BYTES):
    return pltpu.CompilerParams(dimension_semantics=sem, vmem_limit_bytes=vmem)


def _layer_norm(x, g, b):
    mu = jnp.mean(x, axis=-1, keepdims=True)
    xc = x - mu
    var = jnp.mean(xc * xc, axis=-1, keepdims=True)
    return xc * lax.rsqrt(var + LN_EPS) * g + b


def _pack_bf16_pair(x):
    n = x.shape[1] // 2
    bits = lax.bitcast_convert_type(x.astype(BF16).astype(F32), U32)
    return (bits[:, :n] >> 16) | (bits[:, n:] & jnp.uint32(0xFFFF0000))


def _unpack_bf16_pair(w):
    lo = lax.bitcast_convert_type(w << 16, F32)
    hi = lax.bitcast_convert_type(w & jnp.uint32(0xFFFF0000), F32)
    return jnp.concatenate([lo, hi], axis=1).astype(BF16)


def _ln_inproj_kernel(x_ref, g_ref, b_ref, w_ref, o_ref, hb_ref):
    @pl.when(pl.program_id(1) == 0)
    def _():
        hb_ref[...] = _layer_norm(x_ref[...], g_ref[...], b_ref[...]).astype(BF16)

    o_ref[...] = jnp.dot(hb_ref[...], w_ref[...], preferred_element_type=F32).astype(o_ref.dtype)


def ln_inproj(x2, g, b, w_bf16, tm=512, tn=1024):
    n, d = x2.shape
    width = w_bf16.shape[1]
    return pl.pallas_call(
        _ln_inproj_kernel,
        grid=(n // tm, width // tn),
        in_specs=[
            pl.BlockSpec((tm, d), lambda i, j: (i, 0)),
            pl.BlockSpec((1, d), lambda i, j: (0, 0)),
            pl.BlockSpec((1, d), lambda i, j: (0, 0)),
            pl.BlockSpec((d, tn), lambda i, j: (0, j)),
        ],
        out_specs=pl.BlockSpec((tm, tn), lambda i, j: (i, j)),
        out_shape=jax.ShapeDtypeStruct((n, width), BF16),
        scratch_shapes=[pltpu.VMEM((tm, d), BF16)],
        compiler_params=_cparams(("parallel", "arbitrary")),
        name="ln_inproj",
    )(x2, g.reshape(1, d), b.reshape(1, d), w_bf16)


def _hgrn2_constants():
    c = HG_CHUNK
    t = np.arange(c)[:, None]
    u = np.arange(c)[None, :]
    fwd = [(u <= t), (u > t)]
    bwd = [(u >= t), (u < t)]
    for m in HG_LEVELS:
        start = (t // m) * m
        mid = start + m // 2 - 1
        second = t > mid
        fwd.append(np.where(second, (u > mid) & (u <= t), (u > t) & (u <= mid)))
        bwd.append(np.where(second, (u > mid) & (u < t), (u >= t) & (u <= mid)))
    fwd = np.concatenate(fwd, axis=0).astype(np.float32)
    bwd = np.concatenate(bwd, axis=0).astype(np.float32)
    lvl = np.full((c, c), -1, np.int32)
    tt = np.arange(c)[:, None]
    ss = np.arange(c)[None, :]
    for li in reversed(range(len(HG_LEVELS))):
        m = HG_LEVELS[li]
        lvl = np.where((tt // m) == (ss // m), li, lvl)
    lvl = np.where((tt // HG_SUB) == (ss // HG_SUB), -1, lvl).astype(np.int32)
    lvl_f = np.where(tt > ss, lvl, -1).astype(np.int32)
    lvl_b = np.where(tt < ss, lvl, -1).astype(np.int32)
    return fwd, bwd, lvl_f, lvl_b


def _hgrn2_kernel(q_ref, zf_ref, zb_ref, v_ref, g_ref, lbf_ref, lbb_ref, nw_ref,
                  mf_ref, mb_ref, lvlf_ref, lvlb_ref, o_ref,
                  qs_s, v_s, lf_s, kf_s, lb_s, kb_s, cf_s, cb_s,
                  qif_s, kif_s, qib_s, kib_s, df_s, db_s, acc_s, stf_s, stb_s):
    seq = q_ref.shape[0]
    c = HG_CHUNK
    n_chunks = seq // c
    n_sub = c // HG_SUB
    n_lvl = len(HG_LEVELS)

    qs_s[...] = jax.nn.silu(q_ref[...].astype(F32))
    v_s[...] = v_ref[...].astype(F32)
    for z_ref, lb_ref, lf_out, k_out in ((zf_ref, lbf_ref, lf_s, kf_s), (zb_ref, lbb_ref, lb_s, kb_s)):
        z = z_ref[...].astype(F32)
        lb = lb_ref[...]
        lf_out[...] = jnp.log(lb + (1.0 - lb) * jax.nn.sigmoid(z))
        k_out[...] = (1.0 - lb) * jax.nn.sigmoid(-z)

    ones_mat = jnp.ones((LANES, LANES), BF16)
    lane8 = lax.broadcasted_iota(I32, (HG_SUB, LANES), 1)
    sub8 = lax.broadcasted_iota(I32, (HG_SUB, LANES), 0)

    def sums(mat_ref, lf):
        hi = lf.astype(BF16)
        lo = (lf - hi.astype(F32)).astype(BF16)
        both = jnp.concatenate([hi, lo], axis=1)
        r = jnp.dot(mat_ref[...], both, preferred_element_type=F32)
        return r[:, :LANES] + r[:, LANES:]

    def intra(n, carry):
        r0 = pl.multiple_of(n * c, c)
        rows = pl.ds(r0, c)
        qs = qs_s[rows, :]
        vb = v_s[rows, :].astype(BF16)
        a_mat = jnp.zeros((c, c), F32)
        for (mat_ref, lfs, ks, cs, qi_s, ki_s, d_s, lvl_ref, fwd) in (
                (mf_ref, lf_s, kf_s, cf_s, qif_s, kif_s, df_s, lvlf_ref, True),
                (mb_ref, lb_s, kb_s, cb_s, qib_s, kib_s, db_s, lvlb_ref, False)):
            lvl = lvl_ref[...]
            kk = ks[rows, :]
            e = sums(mat_ref, lfs[rows, :])
            cum = e[0:c]
            cs[rows, :] = cum
            qi_s[rows, :] = (qs * jnp.exp(cum)).astype(BF16)
            ki_s[rows, :] = (kk * jnp.exp(e[c:2 * c])).astype(BF16)
            tot = cum[c - 1:c, :] if fwd else cum[0:1, :]
            d_s[pl.ds(n, 1), :] = jnp.exp(tot)
            for li in range(n_lvl):
                w = jnp.exp(e[(2 + li) * c:(3 + li) * c])
                sc = lax.dot_general((qs * w).astype(BF16), (kk * w).astype(BF16),
                                     (((1,), (1,)), ((), ())), preferred_element_type=F32)
                a_mat = jnp.where(lvl == li, sc, a_mat)

        p_list = []
        for j in range(n_sub):
            rj = r0 + j * HG_SUB
            sl = pl.ds(rj, HG_SUB)
            cf = cf_s[sl, :]
            cb = cb_s[sl, :]
            qj = qs_s[sl, :]
            for s in range(HG_SUB):
                src = pl.ds(rj + s, 1)
                df = jnp.where(sub8 >= s, cf - cf_s[src, :], -jnp.inf)
                db = jnp.where(sub8 <= s, cb - cb_s[src, :], -jnp.inf)
                p = qj * (kf_s[src, :] * jnp.exp(df) + kb_s[src, :] * jnp.exp(db))
                p_list.append(p)
        p_all = jnp.concatenate(p_list, axis=0).astype(BF16)
        red = jnp.dot(p_all, ones_mat, preferred_element_type=F32)
        diag_rows = []
        for j in range(n_sub):
            blk = jnp.zeros((HG_SUB, LANES), F32)
            for s in range(HG_SUB):
                idx = (j * HG_SUB + s) * HG_SUB
                blk = jnp.where(lane8 == j * HG_SUB + s, red[idx:idx + HG_SUB], blk)
            diag_rows.append(blk)
        a_mat = a_mat + jnp.concatenate(diag_rows, axis=0)
        acc_s[rows, :] = jnp.dot(a_mat.astype(BF16), vb, preferred_element_type=F32)
        return carry

    lax.fori_loop(0, n_chunks, intra, 0)

    stf_s[...] = jnp.zeros_like(stf_s)
    stb_s[...] = jnp.zeros_like(stb_s)

    def inter(n, carry):
        for (st_s, qi_s, ki_s, d_s, idx) in ((stf_s, qif_s, kif_s, df_s, n),
                                             (stb_s, qib_s, kib_s, db_s, n_chunks - 1 - n)):
            rows = pl.ds(pl.multiple_of(idx * c, c), c)
            st = st_s[...]
            acc_s[rows, :] += lax.dot_general(qi_s[rows, :], st.astype(BF16),
                                              (((1,), (1,)), ((), ())), preferred_element_type=F32)
            upd = lax.dot_general(v_s[rows, :].astype(BF16), ki_s[rows, :],
                                  (((0,), (0,)), ((), ())), preferred_element_type=F32)
            st_s[...] = st * d_s[pl.ds(idx, 1), :] + upd
        return carry

    lax.fori_loop(0, n_chunks, inter, 0)

    o = acc_s[...]
    o = o * lax.rsqrt(jnp.mean(o * o, axis=-1, keepdims=True) + RMS_EPS) * nw_ref[...]
    o_ref[...] = (o * jax.nn.silu(g_ref[...].astype(F32))).astype(o_ref.dtype)


def hgrn2_mixer(proj, lb_f, lb_b, norm_w, batch, seq):
    n = proj.shape[0]
    heads = HG_HEADS
    c = HG_CHUNK
    n_chunks = seq // c
    mf, mb, lvl_f, lvl_b = _hgrn2_constants()
    n_stack = mf.shape[0]

    def col(off):
        return pl.BlockSpec((seq, LANES), lambda b, h, off=off: (b, off + h))

    def per_head(arr):
        return pl.BlockSpec((1, LANES), lambda b, h: (0, h))

    const2 = lambda b, h: (0, 0)
    seq_f32 = pltpu.VMEM((seq, LANES), F32)
    seq_bf16 = pltpu.VMEM((seq, LANES), BF16)
    dec = pltpu.VMEM((n_chunks, LANES), F32)
    state = pltpu.VMEM((LANES, LANES), F32)
    return pl.pallas_call(
        _hgrn2_kernel,
        grid=(batch, heads),
        in_specs=[col(0), col(heads), col(2 * heads), col(3 * heads), col(4 * heads),
                  per_head(lb_f), per_head(lb_b),
                  pl.BlockSpec((1, LANES), const2),
                  pl.BlockSpec((n_stack, c), const2),
                  pl.BlockSpec((n_stack, c), const2),
                  pl.BlockSpec((c, c), const2),
                  pl.BlockSpec((c, c), const2)],
        out_specs=pl.BlockSpec((seq, LANES), lambda b, h: (b, h)),
        out_shape=jax.ShapeDtypeStruct((n, HG_WIDTH), BF16),
        scratch_shapes=[seq_f32] * 8 + [seq_bf16] * 4 + [dec, dec, seq_f32, state, state],
        compiler_params=_cparams(("parallel", "parallel")),
        name="hgrn2_mixer",
    )(proj, proj, proj, proj, proj, lb_f.reshape(1, -1), lb_b.reshape(1, -1), norm_w.reshape(1, -1),
      jnp.asarray(mf, BF16), jnp.asarray(mb, BF16), jnp.asarray(lvl_f), jnp.asarray(lvl_b))


def _shift_rows(x, d, fill, reverse):
    pad = jnp.full((d, x.shape[1]), fill, x.dtype)
    if reverse:
        return jnp.concatenate([x[d:], pad], axis=0)
    return jnp.concatenate([pad, x[:-d]], axis=0)


def _linear_scan(a, u, reverse):
    seq = a.shape[0]
    d = 1
    while d < seq:
        u = u + a * _shift_rows(u, d, 0.0, reverse)
        if d * 2 < seq:
            a = a * _shift_rows(a, d, 1.0, reverse)
        d *= 2
    return u


def _rglru_kernel(xr_ref, gr_ref, cw_ref, cb_ref, wa_ref, ba_ref, wx_ref, bx_ref, lam_ref, o_ref, xp_s):
    seq = xr_ref.shape[0]
    pad = SUBLANES
    pad_l = RG_CONV // 2
    xp_s[0:pad, :] = jnp.zeros((pad, LANES), F32)
    xp_s[pad + seq:, :] = jnp.zeros((pad, LANES), F32)
    xp_s[pad:pad + seq, :] = xr_ref[...].astype(F32)
    xc = cb_ref[...]
    for j in range(RG_CONV):
        off = pad + j - pad_l
        xc = xc + xp_s[off:off + seq, :] * cw_ref[j:j + 1, :]
    xcb = xc.astype(BF16)
    h = None
    for d in range(2):
        r = jax.nn.sigmoid(jnp.dot(xcb, wa_ref[d, 0], preferred_element_type=F32) + ba_ref[d:d + 1, :])
        i = jax.nn.sigmoid(jnp.dot(xcb, wx_ref[d, 0], preferred_element_type=F32) + bx_ref[d:d + 1, :])
        log_a = (-RG_C) * r * jax.nn.softplus(-lam_ref[d:d + 1, :])
        a = jnp.exp(log_a)
        u = jnp.sqrt(-jnp.tanh(log_a) * (a * a + 1.0)) * (i * xc)
        hd = _linear_scan(a, u, reverse=(d == 1))
        h = hd if h is None else h + hd
    o_ref[...] = (h * jax.nn.gelu(gr_ref[...].astype(F32))).astype(o_ref.dtype)


def rglru_mixer(proj, conv_w, conv_b, wa, ba, wx, bx, lam, batch, seq):
    n = proj.shape[0]
    x_off = (3 * HG_KEY_WIDTH + 2 * HG_WIDTH) // LANES
    g_off = x_off + RG_WIDTH // LANES
    blk = lambda b, j: (0, j)
    return pl.pallas_call(
        _rglru_kernel,
        grid=(batch, RG_BLOCKS),
        in_specs=[
            pl.BlockSpec((seq, LANES), lambda b, j: (b, x_off + j)),
            pl.BlockSpec((seq, LANES), lambda b, j: (b, g_off + j)),
            pl.BlockSpec((RG_CONV, LANES), blk),
            pl.BlockSpec((1, LANES), blk),
            pl.BlockSpec((2, 1, RG_BLOCK_DIM, RG_BLOCK_DIM), lambda b, j: (0, j, 0, 0)),
            pl.BlockSpec((2, LANES), blk),
            pl.BlockSpec((2, 1, RG_BLOCK_DIM, RG_BLOCK_DIM), lambda b, j: (0, j, 0, 0)),
            pl.BlockSpec((2, LANES), blk),
            pl.BlockSpec((2, LANES), blk),
        ],
        out_specs=pl.BlockSpec((seq, LANES), lambda b, j: (b, j)),
        out_shape=jax.ShapeDtypeStruct((n, RG_WIDTH), BF16),
        scratch_shapes=[pltpu.VMEM((seq + 2 * SUBLANES, LANES), F32)],
        compiler_params=_cparams(("parallel", "parallel")),
        name="rglru_mixer",
    )(proj, proj, conv_w, conv_b.reshape(1, -1), wa.astype(BF16), ba, wx.astype(BF16), bx, lam)


def _outproj_kernel(yh_ref, yr_ref, x_ref, g0_ref, b0_ref, wo_ref, g1_ref, b1_ref, wr_ref,
                    h1_ref, h1p_ref, lg_ref):
    half = yh_ref.shape[1]
    h0 = _layer_norm(x_ref[...], g0_ref[...], b0_ref[...])
    mix = jnp.dot(yh_ref[...], wo_ref[0:half, :], preferred_element_type=F32)
    mix = mix + jnp.dot(yr_ref[...], wo_ref[half:, :], preferred_element_type=F32)
    h1 = _layer_norm(DN_ALPHA * h0 + mix, g1_ref[...], b1_ref[...])
    h1_ref[...] = h1
    h1p_ref[...] = _pack_bf16_pair(h1)
    lg_ref[...] = lax.dot_general(wr_ref[...], h1.astype(BF16), (((1,), (1,)), ((), ())),
                                  preferred_element_type=F32)


def outproj_ln_router(y_hg, y_rg, x2, g0, b0, w_out_bf16, g1, b1, w_router_t_bf16, tm=256):
    n, d = x2.shape
    half = y_hg.shape[1]
    n_e = w_router_t_bf16.shape[0]
    row = lambda i: (i, 0)
    const = lambda i: (0, 0)
    vec = pl.BlockSpec((1, d), const)
    return pl.pallas_call(
        _outproj_kernel,
        grid=(n // tm,),
        in_specs=[pl.BlockSpec((tm, half), row), pl.BlockSpec((tm, half), row), pl.BlockSpec((tm, d), row),
                  vec, vec, pl.BlockSpec((2 * half, d), const), vec, vec,
                  pl.BlockSpec((n_e, d), const)],
        out_specs=[pl.BlockSpec((tm, d), row), pl.BlockSpec((tm, d // 2), row),
                   pl.BlockSpec((n_e, tm), lambda i: (0, i))],
        out_shape=[jax.ShapeDtypeStruct((n, d), F32), jax.ShapeDtypeStruct((n, d // 2), U32),
                   jax.ShapeDtypeStruct((n_e, n), F32)],
        compiler_params=_cparams(("parallel",)),
        name="outproj_ln_router",
    )(y_hg, y_rg, x2, g0.reshape(1, d), b0.reshape(1, d), w_out_bf16, g1.reshape(1, d), b1.reshape(1, d),
      w_router_t_bf16)


def _first_argmax(vals, iota, n):
    m = jnp.max(vals, axis=0, keepdims=True)
    first = jnp.min(jnp.where(vals == m, iota, n), axis=0, keepdims=True)
    return m, first


def _route_kernel(lg_ref, bias_ref, idx_ref, gate_ref):
    n_e, tn = lg_ref.shape
    scores = jax.nn.sigmoid(lg_ref[...])
    choice = scores + bias_ref[...]
    neg = -jnp.inf
    gi = lax.broadcasted_iota(I32, (GROUP_SIZE, tn), 0)
    gs_rows = []
    for g in range(N_GROUPS):
        blk = choice[g * GROUP_SIZE:(g + 1) * GROUP_SIZE]
        m1, f1 = _first_argmax(blk, gi, GROUP_SIZE)
        m2 = jnp.max(jnp.where(gi == f1, neg, blk), axis=0, keepdims=True)
        gs_rows.append(m1 + m2)
    gs = jnp.concatenate(gs_rows, axis=0)
    ri = lax.broadcasted_iota(I32, (N_GROUPS, tn), 0)
    gmask = jnp.zeros((N_GROUPS, tn), jnp.bool_)
    for _ in range(TOPK_GROUPS):
        _, f = _first_argmax(gs, ri, N_GROUPS)
        hit = ri == f
        gmask = gmask | hit
        gs = jnp.where(hit, neg, gs)
    gm = jnp.where(gmask, 1.0, 0.0)
    emask = jnp.concatenate(
        [jnp.broadcast_to(gm[g:g + 1], (GROUP_SIZE, tn)) for g in range(N_GROUPS)], axis=0)
    masked = jnp.where(emask > 0.5, choice, neg)
    ei = lax.broadcasted_iota(I32, (n_e, tn), 0)
    idx_rows, gate_rows = [], []
    for _ in range(TOP_K):
        _, f = _first_argmax(masked, ei, n_e)
        hit = ei == f
        idx_rows.append(f)
        gate_rows.append(jnp.sum(jnp.where(hit, scores, 0.0), axis=0, keepdims=True))
        masked = jnp.where(hit, neg, masked)
    gate = jnp.concatenate(gate_rows, axis=0)
    gate = gate / jnp.sum(gate, axis=0, keepdims=True) * ROUTE_SCALE
    idx_ref[...] = jnp.concatenate(idx_rows, axis=0)
    gate_ref[...] = gate


def route_topk(logits_t, bias, tn=512):
    n_e, n = logits_t.shape
    return pl.pallas_call(
        _route_kernel,
        grid=(n // tn,),
        in_specs=[pl.BlockSpec((n_e, tn), lambda i: (0, i)), pl.BlockSpec((n_e, 1), lambda i: (0, 0))],
        out_specs=[pl.BlockSpec((TOP_K, tn), lambda i: (0, i)), pl.BlockSpec((TOP_K, tn), lambda i: (0, i))],
        out_shape=[jax.ShapeDtypeStruct((TOP_K, n), I32), jax.ShapeDtypeStruct((TOP_K, n), F32)],
        compiler_params=_cparams(("parallel",)),
        name="route_topk",
    )(logits_t, bias.reshape(n_e, 1))


def _rank_kernel(idx_ref, su_ref, rank_ref, cnt_ref, carry_s):
    n_e = cnt_ref.shape[0]
    tn = idx_ref.shape[1]

    @pl.when(pl.program_id(0) == 0)
    def _():
        carry_s[...] = jnp.zeros_like(carry_s)

    idx = idx_ref[...]
    ei = lax.broadcasted_iota(I32, (n_e, tn), 0)
    hits = [ei == idx[k:k + 1] for k in range(TOP_K)]
    member = jnp.zeros((n_e, tn), F32)
    for h in hits:
        member = member + jnp.where(h, 1.0, 0.0)
    prefix = jnp.dot(member.astype(BF16), su_ref[...], preferred_element_type=F32) + carry_s[:, 0:1]
    rank_ref[...] = jnp.concatenate(
        [jnp.sum(jnp.where(h, prefix, 0.0), axis=0, keepdims=True) for h in hits], axis=0).astype(I32)
    carry_s[...] = carry_s[...] + jnp.sum(member, axis=1, keepdims=True)
    cnt_ref[...] = carry_s[...].astype(I32)


def expert_ranks(idx_t, tn=512):
    k, n = idx_t.shape
    su = jnp.asarray(np.triu(np.ones((tn, tn), np.float32), 1), BF16)
    return pl.pallas_call(
        _rank_kernel,
        grid=(n // tn,),
        in_specs=[pl.BlockSpec((k, tn), lambda i: (0, i)), pl.BlockSpec((tn, tn), lambda i: (0, 0))],
        out_specs=[pl.BlockSpec((k, tn), lambda i: (0, i)), pl.BlockSpec((N_EXPERTS, LANES), lambda i: (0, 0))],
        out_shape=[jax.ShapeDtypeStruct((k, n), I32), jax.ShapeDtypeStruct((N_EXPERTS, LANES), I32)],
        scratch_shapes=[pltpu.VMEM((N_EXPERTS, LANES), F32)],
        compiler_params=_cparams(("arbitrary",)),
        name="expert_ranks",
    )(idx_t, su)


def _gather_rows_kernel(tok_ref, nrows_ref, src_ref, o_ref, sem):
    rb = o_ref.shape[0]
    i = pl.program_id(0)

    @pl.when(i * rb < nrows_ref[0])
    def _():
        def issue(r, c):
            t = tok_ref[i * rb + r]
            pltpu.make_async_copy(src_ref.at[pl.ds(t, 1)], o_ref.at[pl.ds(r, 1)], sem).start()
            return c

        lax.fori_loop(0, rb, issue, 0, unroll=8)
        pltpu.make_async_copy(src_ref.at[pl.ds(0, rb)], o_ref, sem).wait()


def gather_rows(src, row_tok, n_used_rows, rb=256):
    n_rows = row_tok.shape[0]
    width = src.shape[1]
    last = lambda nr: jnp.maximum((nr[0] + rb - 1) // rb - 1, 0)
    return pl.pallas_call(
        _gather_rows_kernel,
        grid_spec=pltpu.PrefetchScalarGridSpec(
            num_scalar_prefetch=2,
            grid=(n_rows // rb,),
            in_specs=[pl.BlockSpec(memory_space=pl.ANY)],
            out_specs=pl.BlockSpec((rb, width), lambda i, tok, nr: (jnp.minimum(i, last(nr)), 0)),
            scratch_shapes=[pltpu.SemaphoreType.DMA(())],
        ),
        out_shape=jax.ShapeDtypeStruct((n_rows, width), src.dtype),
        compiler_params=_cparams(("arbitrary",)),
        name="gather_rows",
    )(row_tok, n_used_rows, src)


def _experts_kernel(be_ref, nb_ref, xs_ref, wg_ref, wu_ref, wd_ref, ys_ref, wg_s, wu_s, wd_s):
    i = pl.program_id(0)

    @pl.when(i < nb_ref[0])
    def _():
        prev = be_ref[jnp.maximum(i - 1, 0)]

        @pl.when((i == 0) | (be_ref[i] != prev))
        def _():
            wg_s[...] = wg_ref[0].astype(BF16)
            wu_s[...] = wu_ref[0].astype(BF16)
            wd_s[...] = wd_ref[0].astype(BF16)

        x = _unpack_bf16_pair(xs_ref[...])
        a = jnp.dot(x, wg_s[...], preferred_element_type=F32)
        u = jnp.dot(x, wu_s[...], preferred_element_type=F32)
        hid = (jax.nn.silu(a) * u).astype(BF16)
        ys_ref[...] = _pack_bf16_pair(jnp.dot(hid, wd_s[...], preferred_element_type=F32))


def grouped_experts(xs, block_expert, n_blocks_used, w_gate, w_up, w_down, tm=MOE_BLOCK):
    n_rows, half = xs.shape
    d = 2 * half
    ff = w_gate.shape[-1]
    n_blocks = n_rows // tm
    rowmap = lambda i, be, nb: (jnp.minimum(i, jnp.maximum(nb[0] - 1, 0)), 0)
    wmap = lambda i, be, nb: (be[i], 0, 0)
    return pl.pallas_call(
        _experts_kernel,
        grid_spec=pltpu.PrefetchScalarGridSpec(
            num_scalar_prefetch=2,
            grid=(n_blocks,),
            in_specs=[pl.BlockSpec((tm, half), rowmap),
                      pl.BlockSpec((1, d, ff), wmap), pl.BlockSpec((1, d, ff), wmap),
                      pl.BlockSpec((1, ff, d), wmap)],
            out_specs=pl.BlockSpec((tm, half), rowmap),
            scratch_shapes=[pltpu.VMEM((d, ff), BF16), pltpu.VMEM((d, ff), BF16), pltpu.VMEM((ff, d), BF16)],
        ),
        out_shape=jax.ShapeDtypeStruct((n_rows, half), U32),
        compiler_params=_cparams(("arbitrary",)),
        name="grouped_experts",
    )(block_expert, n_blocks_used, xs, w_gate, w_up, w_down)


def _combine_kernel(dest_ref, h1_ref, h1p_ref, gate_ref, wsg_ref, wsu_ref, wsd_ref, g2_ref, b2_ref, ys_ref,
                    o_ref, buf_s, sem):
    tm = h1_ref.shape[0]
    i = pl.program_id(0)
    n_tok = pl.num_programs(0) * tm

    def issue(r, c):
        for k in range(TOP_K):
            row = dest_ref[k * n_tok + i * tm + r]
            pltpu.make_async_copy(ys_ref.at[pl.ds(row, 1)], buf_s.at[k, pl.ds(r, 1)], sem).start()
        return c

    lax.fori_loop(0, tm, issue, 0, unroll=2)

    xb = _unpack_bf16_pair(h1p_ref[...])
    a = jnp.dot(xb, wsg_ref[...], preferred_element_type=F32)
    u = jnp.dot(xb, wsu_ref[...], preferred_element_type=F32)
    shared = jnp.dot((jax.nn.silu(a) * u).astype(BF16), wsd_ref[...], preferred_element_type=F32)

    for k in range(TOP_K):
        pltpu.make_async_copy(ys_ref.at[pl.ds(0, tm)], buf_s.at[k], sem).wait()

    gate = gate_ref[...]
    routed = jnp.zeros_like(shared)
    for k in range(TOP_K):
        routed = routed + _unpack_bf16_pair(buf_s[k]).astype(F32) * gate[:, k:k + 1]
    o_ref[...] = _layer_norm(DN_ALPHA * h1_ref[...] + routed + shared, g2_ref[...], b2_ref[...])


def combine_shared_ln(dest_flat, h1, h1p, gate_nk, ys, wsg, wsu, wsd, g2, b2, tm=128):
    n, d = h1.shape
    half = d // 2
    ff = wsg.shape[1]
    row = lambda i, dest: (i, 0)
    const = lambda i, dest: (0, 0)
    return pl.pallas_call(
        _combine_kernel,
        grid_spec=pltpu.PrefetchScalarGridSpec(
            num_scalar_prefetch=1,
            grid=(n // tm,),
            in_specs=[pl.BlockSpec((tm, d), row), pl.BlockSpec((tm, half), row), pl.BlockSpec((tm, TOP_K), row),
                      pl.BlockSpec((d, ff), const), pl.BlockSpec((d, ff), const), pl.BlockSpec((ff, d), const),
                      pl.BlockSpec((1, d), const), pl.BlockSpec((1, d), const),
                      pl.BlockSpec(memory_space=pl.ANY)],
            out_specs=pl.BlockSpec((tm, d), row),
            scratch_shapes=[pltpu.VMEM((TOP_K, tm, half), U32), pltpu.SemaphoreType.DMA(())],
        ),
        out_shape=jax.ShapeDtypeStruct((n, d), F32),
        compiler_params=_cparams(("arbitrary",)),
        name="combine_shared_ln",
    )(dest_flat, h1, h1p, gate_nk, wsg, wsu, wsd, g2.reshape(1, d), b2.reshape(1, d), ys)


def kernel(x, ln_emb_g, ln_emb_b, w_in, lb_logits, hg_norm_w, conv_w, conv_b, rg_wa, rg_ba, rg_wx, rg_bx,
           rg_lam, w_out, ln1_g, ln1_b, w_router, router_bias, w_gate, w_up, w_down, ws_gate, ws_up, ws_down,
           ln2_g, ln2_b):
    batch, seq, d = x.shape
    n_tok = batch * seq
    layer = 0
    x2 = x.reshape(n_tok, d)
    lb = jnp.cumsum(jax.nn.softmax(lb_logits.astype(F32), axis=1), axis=1)

    proj = ln_inproj(x2, ln_emb_g, ln_emb_b, w_in[layer].astype(BF16))
    y_hg = hgrn2_mixer(proj, lb[0, layer], lb[1, layer], hg_norm_w[layer], batch, seq)
    y_rg = rglru_mixer(proj, conv_w[layer], conv_b[layer], rg_wa[layer], rg_ba[layer], rg_wx[layer],
                       rg_bx[layer], rg_lam[layer], batch, seq)
    h1, h1p, logits_t = outproj_ln_router(
        y_hg, y_rg, x2, ln_emb_g, ln_emb_b, w_out[layer].astype(BF16), ln1_g[layer], ln1_b[layer],
        w_router[layer].T.astype(BF16))

    idx_t, gate_t = route_topk(logits_t, router_bias[layer])
    rank_t, counts = expert_ranks(idx_t)
    counts = counts[:, 0]

    n_assign = n_tok * TOP_K
    padded = (counts + MOE_BLOCK - 1) // MOE_BLOCK * MOE_BLOCK
    padded_end = jnp.cumsum(padded)
    padded_start = padded_end - padded
    n_blocks = -(-(n_assign + N_EXPERTS * (MOE_BLOCK - 1)) // MOE_BLOCK)
    n_blocks = -(-n_blocks // 2) * 2
    n_rows = n_blocks * MOE_BLOCK
    n_used_rows = padded_end[-1:].astype(I32)
    n_used_blocks = n_used_rows // MOE_BLOCK
    dest_t = padded_start[idx_t].astype(I32) + rank_t
    tok_ids = jnp.broadcast_to(jnp.arange(n_tok, dtype=I32)[None, :], (TOP_K, n_tok))
    row_tok = jnp.zeros((n_rows,), I32).at[dest_t.reshape(-1)].set(tok_ids.reshape(-1))
    blk_ids = jnp.minimum(jnp.arange(n_blocks, dtype=I32), jnp.maximum(n_used_blocks[0] - 1, 0))
    block_expert = jnp.minimum(
        jnp.searchsorted(padded_end, blk_ids * MOE_BLOCK, side='right'), N_EXPERTS - 1).astype(I32)

    xs = gather_rows(h1p, row_tok, n_used_rows)
    ys = grouped_experts(xs, block_expert, n_used_blocks, w_gate[layer], w_up[layer], w_down[layer])
    out = combine_shared_ln(dest_t.reshape(-1), h1, h1p, gate_t.T, ys,
                            ws_gate[layer].astype(BF16), ws_up[layer].astype(BF16),
                            ws_down[layer].astype(BF16), ln2_g[layer], ln2_b[layer])
    return out.reshape(batch, seq, d)
```

```python
import functools

import numpy as np
import jax
import jax.numpy as jnp
from jax import lax
from jax.experimental import pallas as pl
from jax.experimental.pallas import tpu as pltpu

F32 = jnp.float32
BF16 = jnp.bfloat16
U32 = jnp.uint32
I32 = jnp.int32

D_MODEL = 2048
DEPTH = 1
HG_KEY_DIM = 128
HG_VAL_DIM = 128
HG_WIDTH = D_MODEL // 2
HG_HEADS = HG_WIDTH // HG_VAL_DIM
HG_KEY_WIDTH = HG_HEADS * HG_KEY_DIM
RG_WIDTH = D_MODEL - HG_WIDTH
RG_BLOCKS = 8
RG_BLOCK_DIM = RG_WIDTH // RG_BLOCKS
RG_CONV = 4
RG_C = 8.0
MIX_WIDTH = HG_WIDTH + RG_WIDTH
IN_WIDTHS = (HG_KEY_WIDTH, HG_KEY_WIDTH, HG_KEY_WIDTH, HG_WIDTH, HG_WIDTH, RG_WIDTH, RG_WIDTH)
IN_WIDTH = sum(IN_WIDTHS)
N_EXPERTS = 256
TOP_K = 8
N_GROUPS = 8
TOPK_GROUPS = 4
GROUP_SIZE = N_EXPERTS // N_GROUPS
EXPERT_FF = 512
SHARED_FF = 512
ROUTE_SCALE = 2.5
MOE_BLOCK = 128
DN_ALPHA = (2 * DEPTH) ** 0.25
LN_EPS = 1e-5
RMS_EPS = 1e-6

LANES = 128
SUBLANES = 8
VMEM_LIMIT_BYTES = 56 * 1024 * 1024

HG_CHUNK = 128
HG_SUB = 8
HG_LEVELS = (16, 32, 64, 128)


def _cparams(sem, vmem=VMEM_LIMIT_BYTES):
    return pltpu.CompilerParams(dimension_semantics=sem, vmem_limit_bytes=vmem)


def _layer_norm(x, g, b):
    mu = jnp.mean(x, axis=-1, keepdims=True)
    xc = x - mu
    var = jnp.mean(xc * xc, axis=-1, keepdims=True)
    return xc * lax.rsqrt(var + LN_EPS) * g + b


def _pack_bf16_pair(x):
    n = x.shape[1] // 2
    bits = lax.bitcast_convert_type(x.astype(BF16).astype(F32), U32)
    return (bits[:, :n] >> 16) | (bits[:, n:] & jnp.uint32(0xFFFF0000))


def _unpack_bf16_pair(w):
    lo = lax.bitcast_convert_type(w << 16, F32)
    hi = lax.bitcast_convert_type(w & jnp.uint32(0xFFFF0000), F32)
    return jnp.concatenate([lo, hi], axis=1).astype(BF16)


def _store_row_tiles(ref, packed):
    m = packed.shape[0]
    for s in range(SUBLANES):
        ref[pl.ds(s, m, stride=SUBLANES), :] = packed[:, s * LANES:(s + 1) * LANES]


def _load_row_tiles(ref, m):
    return jnp.concatenate([ref[pl.ds(s, m, stride=SUBLANES), :] for s in range(SUBLANES)], axis=1)


def _ln_inproj_kernel(x_ref, g_ref, b_ref, w_ref, o_ref, hb_ref):
    @pl.when(pl.program_id(1) == 0)
    def _():
        hb_ref[...] = _layer_norm(x_ref[...], g_ref[...], b_ref[...]).astype(BF16)

    o_ref[...] = jnp.dot(hb_ref[...], w_ref[...], preferred_element_type=F32).astype(o_ref.dtype)


def ln_inproj(x2, g, b, w_bf16, tm=512, tn=1024):
    n, d = x2.shape
    width = w_bf16.shape[1]
    return pl.pallas_call(
        _ln_inproj_kernel,
        grid=(n // tm, width // tn),
        in_specs=[
            pl.BlockSpec((tm, d), lambda i, j: (i, 0)),
            pl.BlockSpec((1, d), lambda i, j: (0, 0)),
            pl.BlockSpec((1, d), lambda i, j: (0, 0)),
            pl.BlockSpec((d, tn), lambda i, j: (0, j)),
        ],
        out_specs=pl.BlockSpec((tm, tn), lambda i, j: (i, j)),
        out_shape=jax.ShapeDtypeStruct((n, width), BF16),
        scratch_shapes=[pltpu.VMEM((tm, d), BF16)],
        compiler_params=_cparams(("parallel", "arbitrary")),
        name="ln_inproj",
    )(x2, g.reshape(1, d), b.reshape(1, d), w_bf16)


def _hgrn2_constants():
    c = HG_CHUNK
    t = np.arange(c)[:, None]
    u = np.arange(c)[None, :]
    fwd = [(u <= t), (u > t)]
    bwd = [(u >= t), (u < t)]
    for m in HG_LEVELS:
        start = (t // m) * m
        mid = start + m // 2 - 1
        second = t > mid
        fwd.append(np.where(second, (u > mid) & (u <= t), (u > t) & (u <= mid)))
        bwd.append(np.where(second, (u > mid) & (u < t), (u >= t) & (u <= mid)))
    fwd = np.concatenate(fwd, axis=0).astype(np.float32)
    bwd = np.concatenate(bwd, axis=0).astype(np.float32)
    lvl = np.full((c, c), -1, np.int32)
    tt = np.arange(c)[:, None]
    ss = np.arange(c)[None, :]
    for li in reversed(range(len(HG_LEVELS))):
        m = HG_LEVELS[li]
        lvl = np.where((tt // m) == (ss // m), li, lvl)
    lvl = np.where((tt // HG_SUB) == (ss // HG_SUB), -1, lvl).astype(np.int32)
    lvl_f = np.where(tt > ss, lvl, -1).astype(np.int32)
    lvl_b = np.where(tt < ss, lvl, -1).astype(np.int32)
    return fwd, bwd, lvl_f, lvl_b


def _hgrn2_kernel(q_ref, zf_ref, zb_ref, v_ref, g_ref, lbf_ref, lbb_ref, nw_ref,
                  mf_ref, mb_ref, lvlf_ref, lvlb_ref, o_ref,
                  qs_s, v_s, lf_s, kf_s, lb_s, kb_s, cf_s, cb_s,
                  qif_s, kif_s, qib_s, kib_s, df_s, db_s, acc_s, stf_s, stb_s):
    seq = q_ref.shape[0]
    c = HG_CHUNK
    n_chunks = seq // c
    n_sub = c // HG_SUB
    n_lvl = len(HG_LEVELS)

    qs_s[...] = jax.nn.silu(q_ref[...].astype(F32))
    v_s[...] = v_ref[...].astype(F32)
    for z_ref, lb_ref, lf_out, k_out in ((zf_ref, lbf_ref, lf_s, kf_s), (zb_ref, lbb_ref, lb_s, kb_s)):
        z = z_ref[...].astype(F32)
        lb = lb_ref[...]
        sig = jax.nn.sigmoid(z)
        lf_out[...] = jnp.log(lb + (1.0 - lb) * sig)
        k_out[...] = (1.0 - lb) * (1.0 - sig)

    ones_mat = jnp.ones((LANES, LANES), BF16)
    lane8 = lax.broadcasted_iota(I32, (HG_SUB, LANES), 1)
    sub8 = lax.broadcasted_iota(I32, (HG_SUB, LANES), 0)

    def sums(mat_ref, lf):
        hi = lf.astype(BF16)
        lo = (lf - hi.astype(F32)).astype(BF16)
        both = jnp.concatenate([hi, lo], axis=1)
        r = jnp.dot(mat_ref[...], both, preferred_element_type=F32)
        return r[:, :LANES] + r[:, LANES:]

    def intra(n, carry):
        r0 = pl.multiple_of(n * c, c)
        rows = pl.ds(r0, c)
        qs = qs_s[rows, :]
        vb = v_s[rows, :].astype(BF16)
        a_mat = jnp.zeros((c, c), F32)
        for (mat_ref, lfs, ks, cs, qi_s, ki_s, d_s, lvl_ref, fwd) in (
                (mf_ref, lf_s, kf_s, cf_s, qif_s, kif_s, df_s, lvlf_ref, True),
                (mb_ref, lb_s, kb_s, cb_s, qib_s, kib_s, db_s, lvlb_ref, False)):
            lvl = lvl_ref[...]
            kk = ks[rows, :]
            e = sums(mat_ref, lfs[rows, :])
            cum = e[0:c]
            cs[rows, :] = cum
            qi_s[rows, :] = (qs * jnp.exp(cum)).astype(BF16)
            ki_s[rows, :] = (kk * jnp.exp(e[c:2 * c])).astype(BF16)
            tot = cum[c - 1:c, :] if fwd else cum[0:1, :]
            d_s[pl.ds(n, 1), :] = jnp.exp(tot)
            for li in range(n_lvl):
                w = jnp.exp(e[(2 + li) * c:(3 + li) * c])
                sc = lax.dot_general((qs * w).astype(BF16), (kk * w).astype(BF16),
                                     (((1,), (1,)), ((), ())), preferred_element_type=F32)
                a_mat = jnp.where(lvl == li, sc, a_mat)

        p_list = []
        for j in range(n_sub):
            rj = r0 + j * HG_SUB
            sl = pl.ds(rj, HG_SUB)
            cf = cf_s[sl, :]
            cb = cb_s[sl, :]
            qj = qs_s[sl, :]
            for s in range(HG_SUB):
                src = pl.ds(rj + s, 1)
                df = jnp.where(sub8 >= s, cf - cf_s[src, :], -jnp.inf)
                db = jnp.where(sub8 <= s, cb - cb_s[src, :], -jnp.inf)
                p = qj * (kf_s[src, :] * jnp.exp(df) + kb_s[src, :] * jnp.exp(db))
                p_list.append(p)
        p_all = jnp.concatenate(p_list, axis=0).astype(BF16)
        red = jnp.dot(p_all, ones_mat, preferred_element_type=F32)
        diag_rows = []
        for j in range(n_sub):
            blk = jnp.zeros((HG_SUB, LANES), F32)
            for s in range(HG_SUB):
                idx = (j * HG_SUB + s) * HG_SUB
                blk = jnp.where(lane8 == j * HG_SUB + s, red[idx:idx + HG_SUB], blk)
            diag_rows.append(blk)
        a_mat = a_mat + jnp.concatenate(diag_rows, axis=0)
        acc_s[rows, :] = jnp.dot(a_mat.astype(BF16), vb, preferred_element_type=F32)
        return carry

    lax.fori_loop(0, n_chunks, intra, 0, unroll=2)

    stf_s[...] = jnp.zeros_like(stf_s)
    stb_s[...] = jnp.zeros_like(stb_s)

    def inter(n, carry):
        for (st_s, qi_s, ki_s, d_s, idx) in ((stf_s, qif_s, kif_s, df_s, n),
                                             (stb_s, qib_s, kib_s, db_s, n_chunks - 1 - n)):
            rows = pl.ds(pl.multiple_of(idx * c, c), c)
            st = st_s[...]
            acc_s[rows, :] += lax.dot_general(qi_s[rows, :], st.astype(BF16),
                                              (((1,), (1,)), ((), ())), preferred_element_type=F32)
            upd = lax.dot_general(v_s[rows, :].astype(BF16), ki_s[rows, :],
                                  (((0,), (0,)), ((), ())), preferred_element_type=F32)
            st_s[...] = st * d_s[pl.ds(idx, 1), :] + upd
        return carry

    lax.fori_loop(0, n_chunks, inter, 0)

    o = acc_s[...]
    o = o * lax.rsqrt(jnp.mean(o * o, axis=-1, keepdims=True) + RMS_EPS) * nw_ref[...]
    o_ref[...] = (o * jax.nn.silu(g_ref[...].astype(F32))).astype(o_ref.dtype)


def hgrn2_mixer(proj, lb_f, lb_b, norm_w, batch, seq):
    n = proj.shape[0]
    heads = HG_HEADS
    c = HG_CHUNK
    n_chunks = seq // c
    mf, mb, lvl_f, lvl_b = _hgrn2_constants()
    n_stack = mf.shape[0]

    def col(off):
        return pl.BlockSpec((seq, LANES), lambda b, h, off=off: (b, off + h))

    def per_head(arr):
        return pl.BlockSpec((1, LANES), lambda b, h: (0, h))

    const2 = lambda b, h: (0, 0)
    seq_f32 = pltpu.VMEM((seq, LANES), F32)
    seq_bf16 = pltpu.VMEM((seq, LANES), BF16)
    dec = pltpu.VMEM((n_chunks, LANES), F32)
    state = pltpu.VMEM((LANES, LANES), F32)
    return pl.pallas_call(
        _hgrn2_kernel,
        grid=(batch, heads),
        in_specs=[col(0), col(heads), col(2 * heads), col(3 * heads), col(4 * heads),
                  per_head(lb_f), per_head(lb_b),
                  pl.BlockSpec((1, LANES), const2),
                  pl.BlockSpec((n_stack, c), const2),
                  pl.BlockSpec((n_stack, c), const2),
                  pl.BlockSpec((c, c), const2),
                  pl.BlockSpec((c, c), const2)],
        out_specs=pl.BlockSpec((seq, LANES), lambda b, h: (b, h)),
        out_shape=jax.ShapeDtypeStruct((n, HG_WIDTH), BF16),
        scratch_shapes=[seq_f32] * 8 + [seq_bf16] * 4 + [dec, dec, seq_f32, state, state],
        compiler_params=_cparams(("parallel", "parallel")),
        name="hgrn2_mixer",
    )(proj, proj, proj, proj, proj, lb_f.reshape(1, -1), lb_b.reshape(1, -1), norm_w.reshape(1, -1),
      jnp.asarray(mf, BF16), jnp.asarray(mb, BF16), jnp.asarray(lvl_f), jnp.asarray(lvl_b))


def _shift_rows(x, d, fill, reverse):
    pad = jnp.full((d, x.shape[1]), fill, x.dtype)
    if reverse:
        return jnp.concatenate([x[d:], pad], axis=0)
    return jnp.concatenate([pad, x[:-d]], axis=0)


def _linear_scan(a, u, reverse):
    seq = a.shape[0]
    d = 1
    while d < seq:
        u = u + a * _shift_rows(u, d, 0.0, reverse)
        if d * 2 < seq:
            a = a * _shift_rows(a, d, 1.0, reverse)
        d *= 2
    return u


def _rglru_kernel(xr_ref, gr_ref, cw_ref, cb_ref, wa_ref, ba_ref, wx_ref, bx_ref, lam_ref, o_ref, xp_s):
    seq = xr_ref.shape[0]
    pad = SUBLANES
    pad_l = RG_CONV // 2
    xp_s[0:pad, :] = jnp.zeros((pad, LANES), F32)
    xp_s[pad + seq:, :] = jnp.zeros((pad, LANES), F32)
    xp_s[pad:pad + seq, :] = xr_ref[...].astype(F32)
    xc = cb_ref[...]
    for j in range(RG_CONV):
        off = pad + j - pad_l
        xc = xc + xp_s[off:off + seq, :] * cw_ref[j:j + 1, :]
    xcb = xc.astype(BF16)
    h = None
    for d in range(2):
        r = jax.nn.sigmoid(jnp.dot(xcb, wa_ref[d, 0], preferred_element_type=F32) + ba_ref[d:d + 1, :])
        i = jax.nn.sigmoid(jnp.dot(xcb, wx_ref[d, 0], preferred_element_type=F32) + bx_ref[d:d + 1, :])
        log_a = (-RG_C) * r * jax.nn.softplus(-lam_ref[d:d + 1, :])
        a = jnp.exp(log_a)
        u = jnp.sqrt(-jnp.tanh(log_a) * (a * a + 1.0)) * (i * xc)
        hd = _linear_scan(a, u, reverse=(d == 1))
        h = hd if h is None else h + hd
    o_ref[...] = (h * jax.nn.gelu(gr_ref[...].astype(F32))).astype(o_ref.dtype)


def rglru_mixer(proj, conv_w, conv_b, wa, ba, wx, bx, lam, batch, seq):
    n = proj.shape[0]
    x_off = (3 * HG_KEY_WIDTH + 2 * HG_WIDTH) // LANES
    g_off = x_off + RG_WIDTH // LANES
    blk = lambda b, j: (0, j)
    return pl.pallas_call(
        _rglru_kernel,
        grid=(batch, RG_BLOCKS),
        in_specs=[
            pl.BlockSpec((seq, LANES), lambda b, j: (b, x_off + j)),
            pl.BlockSpec((seq, LANES), lambda b, j: (b, g_off + j)),
            pl.BlockSpec((RG_CONV, LANES), blk),
            pl.BlockSpec((1, LANES), blk),
            pl.BlockSpec((2, 1, RG_BLOCK_DIM, RG_BLOCK_DIM), lambda b, j: (0, j, 0, 0)),
            pl.BlockSpec((2, LANES), blk),
            pl.BlockSpec((2, 1, RG_BLOCK_DIM, RG_BLOCK_DIM), lambda b, j: (0, j, 0, 0)),
            pl.BlockSpec((2, LANES), blk),
            pl.BlockSpec((2, LANES), blk),
        ],
        out_specs=pl.BlockSpec((seq, LANES), lambda b, j: (b, j)),
        out_shape=jax.ShapeDtypeStruct((n, RG_WIDTH), BF16),
        scratch_shapes=[pltpu.VMEM((seq + 2 * SUBLANES, LANES), F32)],
        compiler_params=_cparams(("parallel", "parallel")),
        name="rglru_mixer",
    )(proj, proj, conv_w, conv_b.reshape(1, -1), wa.astype(BF16), ba, wx.astype(BF16), bx, lam)


def _outproj_kernel(yh_ref, yr_ref, x_ref, g0_ref, b0_ref, wo_ref, g1_ref, b1_ref, wr_ref,
                    h1_ref, h1p_ref, lg_ref):
    half = yh_ref.shape[1]
    h0 = _layer_norm(x_ref[...], g0_ref[...], b0_ref[...])
    mix = jnp.dot(yh_ref[...], wo_ref[0:half, :], preferred_element_type=F32)
    mix = mix + jnp.dot(yr_ref[...], wo_ref[half:, :], preferred_element_type=F32)
    h1 = _layer_norm(DN_ALPHA * h0 + mix, g1_ref[...], b1_ref[...])
    h1_ref[...] = h1
    _store_row_tiles(h1p_ref, _pack_bf16_pair(h1))
    lg_ref[...] = lax.dot_general(wr_ref[...], h1.astype(BF16), (((1,), (1,)), ((), ())),
                                  preferred_element_type=F32)


def outproj_ln_router(y_hg, y_rg, x2, g0, b0, w_out_bf16, g1, b1, w_router_t_bf16, tm=256):
    n, d = x2.shape
    half = y_hg.shape[1]
    n_e = w_router_t_bf16.shape[0]
    row = lambda i: (i, 0)
    const = lambda i: (0, 0)
    vec = pl.BlockSpec((1, d), const)
    return pl.pallas_call(
        _outproj_kernel,
        grid=(n // tm,),
        in_specs=[pl.BlockSpec((tm, half), row), pl.BlockSpec((tm, half), row), pl.BlockSpec((tm, d), row),
                  vec, vec, pl.BlockSpec((2 * half, d), const), vec, vec,
                  pl.BlockSpec((n_e, d), const)],
        out_specs=[pl.BlockSpec((tm, d), row), pl.BlockSpec((tm * SUBLANES, LANES), row),
                   pl.BlockSpec((n_e, tm), lambda i: (0, i))],
        out_shape=[jax.ShapeDtypeStruct((n, d), F32), jax.ShapeDtypeStruct((n * SUBLANES, LANES), U32),
                   jax.ShapeDtypeStruct((n_e, n), F32)],
        compiler_params=_cparams(("parallel",)),
        name="outproj_ln_router",
    )(y_hg, y_rg, x2, g0.reshape(1, d), b0.reshape(1, d), w_out_bf16, g1.reshape(1, d), b1.reshape(1, d),
      w_router_t_bf16)


def _first_argmax(vals, iota, n):
    m = jnp.max(vals, axis=0, keepdims=True)
    first = jnp.min(jnp.where(vals == m, iota, n), axis=0, keepdims=True)
    return m, first


def _route_kernel(lg_ref, bias_ref, idx_ref, gate_ref):
    n_e, tn = lg_ref.shape
    scores = jax.nn.sigmoid(lg_ref[...])
    choice = scores + bias_ref[...]
    neg = -jnp.inf
    gi = lax.broadcasted_iota(I32, (GROUP_SIZE, tn), 0)
    gs_rows = []
    for g in range(N_GROUPS):
        blk = choice[g * GROUP_SIZE:(g + 1) * GROUP_SIZE]
        m1, f1 = _first_argmax(blk, gi, GROUP_SIZE)
        m2 = jnp.max(jnp.where(gi == f1, neg, blk), axis=0, keepdims=True)
        gs_rows.append(m1 + m2)
    gs = jnp.concatenate(gs_rows, axis=0)
    ri = lax.broadcasted_iota(I32, (N_GROUPS, tn), 0)
    gmask = jnp.zeros((N_GROUPS, tn), jnp.bool_)
    for _ in range(TOPK_GROUPS):
        _, f = _first_argmax(gs, ri, N_GROUPS)
        hit = ri == f
        gmask = gmask | hit
        gs = jnp.where(hit, neg, gs)
    gm = jnp.where(gmask, 1.0, 0.0)
    emask = jnp.concatenate(
        [jnp.broadcast_to(gm[g:g + 1], (GROUP_SIZE, tn)) for g in range(N_GROUPS)], axis=0)
    masked = jnp.where(emask > 0.5, choice, neg)
    ei = lax.broadcasted_iota(I32, (n_e, tn), 0)
    idx_rows, gate_rows = [], []
    for _ in range(TOP_K):
        _, f = _first_argmax(masked, ei, n_e)
        hit = ei == f
        idx_rows.append(f)
        gate_rows.append(jnp.sum(jnp.where(hit, scores, 0.0), axis=0, keepdims=True))
        masked = jnp.where(hit, neg, masked)
    gate = jnp.concatenate(gate_rows, axis=0)
    gate = gate / jnp.sum(gate, axis=0, keepdims=True) * ROUTE_SCALE
    idx_ref[...] = jnp.concatenate(idx_rows, axis=0)
    gate_ref[...] = gate


def route_topk(logits_t, bias, tn=512):
    n_e, n = logits_t.shape
    return pl.pallas_call(
        _route_kernel,
        grid=(n // tn,),
        in_specs=[pl.BlockSpec((n_e, tn), lambda i: (0, i)), pl.BlockSpec((n_e, 1), lambda i: (0, 0))],
        out_specs=[pl.BlockSpec((TOP_K, tn), lambda i: (0, i)), pl.BlockSpec((TOP_K, tn), lambda i: (0, i))],
        out_shape=[jax.ShapeDtypeStruct((TOP_K, n), I32), jax.ShapeDtypeStruct((TOP_K, n), F32)],
        compiler_params=_cparams(("parallel",)),
        name="route_topk",
    )(logits_t, bias.reshape(n_e, 1))


def _rank_kernel(idx_ref, su_ref, rank_ref, cnt_ref, carry_s):
    n_e = cnt_ref.shape[0]
    tn = idx_ref.shape[1]

    @pl.when(pl.program_id(0) == 0)
    def _():
        carry_s[...] = jnp.zeros_like(carry_s)

    idx = idx_ref[...]
    ei = lax.broadcasted_iota(I32, (n_e, tn), 0)
    hits = [ei == idx[k:k + 1] for k in range(TOP_K)]
    member = jnp.zeros((n_e, tn), F32)
    for h in hits:
        member = member + jnp.where(h, 1.0, 0.0)
    prefix = jnp.dot(member.astype(BF16), su_ref[...], preferred_element_type=F32) + carry_s[:, 0:1]
    rank_ref[...] = jnp.concatenate(
        [jnp.sum(jnp.where(h, prefix, 0.0), axis=0, keepdims=True) for h in hits], axis=0).astype(I32)
    carry_s[...] = carry_s[...] + jnp.sum(member, axis=1, keepdims=True)
    cnt_ref[...] = carry_s[...].astype(I32)


def expert_ranks(idx_t, tn=512):
    k, n = idx_t.shape
    su = jnp.asarray(np.triu(np.ones((tn, tn), np.float32), 1), BF16)
    return pl.pallas_call(
        _rank_kernel,
        grid=(n // tn,),
        in_specs=[pl.BlockSpec((k, tn), lambda i: (0, i)), pl.BlockSpec((tn, tn), lambda i: (0, 0))],
        out_specs=[pl.BlockSpec((k, tn), lambda i: (0, i)), pl.BlockSpec((N_EXPERTS, LANES), lambda i: (0, 0))],
        out_shape=[jax.ShapeDtypeStruct((k, n), I32), jax.ShapeDtypeStruct((N_EXPERTS, LANES), I32)],
        scratch_shapes=[pltpu.VMEM((N_EXPERTS, LANES), F32)],
        compiler_params=_cparams(("arbitrary",)),
        name="expert_ranks",
    )(idx_t, su)


def _dest_kernel(idx_ref, rank_ref, start_ref, dest_ref):
    n_e = start_ref.shape[0]
    tn = idx_ref.shape[1]
    idx = idx_ref[...]
    ei = lax.broadcasted_iota(I32, (n_e, tn), 0)
    start = start_ref[...]
    base = jnp.concatenate(
        [jnp.sum(jnp.where(ei == idx[k:k + 1], start, 0.0), axis=0, keepdims=True) for k in range(TOP_K)],
        axis=0)
    dest_ref[...] = base.astype(I32) + rank_ref[...]


def assignment_rows(idx_t, rank_t, padded_start, tn=512):
    k, n = idx_t.shape
    blk = pl.BlockSpec((k, tn), lambda i: (0, i))
    return pl.pallas_call(
        _dest_kernel,
        grid=(n // tn,),
        in_specs=[blk, blk, pl.BlockSpec((N_EXPERTS, 1), lambda i: (0, 0))],
        out_specs=blk,
        out_shape=jax.ShapeDtypeStruct((k, n), I32),
        compiler_params=_cparams(("parallel",)),
        name="assignment_rows",
    )(idx_t, rank_t, padded_start.astype(F32).reshape(N_EXPERTS, 1))


WEIGHT_DMA_PRIORITY = 1


def _experts_kernel(be_ref, nxt_ref, slot_ref, tok_ref, nb_ref,
                    h1p_ref, wg_ref, wu_ref, wd_ref, ys_ref,
                    xbuf, wgb, wub, wdb, wg_s, wu_s, wd_s, gsem, wsem):
    tm = ys_ref.shape[0] // SUBLANES
    i = pl.program_id(0)
    nb = nb_ref[0]

    def weight_copies(e, slot):
        return (pltpu.make_async_copy(wg_ref.at[e], wgb.at[slot], wsem.at[slot]),
                pltpu.make_async_copy(wu_ref.at[e], wub.at[slot], wsem.at[slot]),
                pltpu.make_async_copy(wd_ref.at[e], wdb.at[slot], wsem.at[slot]))

    def start_rows(blk, slot):
        def issue(r, c):
            t = tok_ref[blk * tm + r]
            pltpu.make_async_copy(h1p_ref.at[pl.ds(t * SUBLANES, SUBLANES)],
                                  xbuf.at[slot, pl.ds(r * SUBLANES, SUBLANES)], gsem.at[slot]).start()
            return c

        lax.fori_loop(0, tm, issue, 0, unroll=8)

    @pl.when(i < nb)
    def _():
        e = be_ref[i]
        wslot = slot_ref[i]
        xslot = lax.rem(i, 2)

        @pl.when(i == 0)
        def _():
            for cp in weight_copies(e, wslot):
                cp.start(priority=WEIGHT_DMA_PRIORITY)
            start_rows(0, 0)

        @pl.when((i == 0) | (e != be_ref[jnp.maximum(i - 1, 0)]))
        def _():
            for cp in weight_copies(e, wslot):
                cp.wait()
            nxt = nxt_ref[i]

            @pl.when(nxt != e)
            def _():
                for cp in weight_copies(nxt, 1 - wslot):
                    cp.start(priority=WEIGHT_DMA_PRIORITY)

            wg_s[...] = wgb[wslot].astype(BF16)
            wu_s[...] = wub[wslot].astype(BF16)
            wd_s[...] = wdb[wslot].astype(BF16)

        @pl.when(i + 1 < nb)
        def _():
            start_rows(i + 1, 1 - xslot)

        pltpu.make_async_copy(h1p_ref.at[pl.ds(0, tm * SUBLANES)], xbuf.at[xslot], gsem.at[xslot]).wait()
        x = _unpack_bf16_pair(_load_row_tiles(xbuf.at[xslot], tm))
        a = jnp.dot(x, wg_s[...], preferred_element_type=F32)
        u = jnp.dot(x, wu_s[...], preferred_element_type=F32)
        hid = (jax.nn.silu(a) * u).astype(BF16)
        y = jnp.dot(hid, wd_s[...], preferred_element_type=F32)
        _store_row_tiles(ys_ref, _pack_bf16_pair(y))


def grouped_experts(h1p_tiles, row_tok, block_expert, block_next, block_slot, n_blocks_used,
                    w_gate, w_up, w_down, tm=MOE_BLOCK):
    n_rows = row_tok.shape[0]
    n_e, d, ff = w_gate.shape
    n_blocks = n_rows // tm
    rowmap = lambda i, be, nx, sl, tok, nb: (jnp.minimum(i, jnp.maximum(nb[0] - 1, 0)), 0)
    hbm = pl.BlockSpec(memory_space=pl.ANY)
    return pl.pallas_call(
        _experts_kernel,
        grid_spec=pltpu.PrefetchScalarGridSpec(
            num_scalar_prefetch=5,
            grid=(n_blocks,),
            in_specs=[hbm, hbm, hbm, hbm],
            out_specs=pl.BlockSpec((tm * SUBLANES, LANES), rowmap),
            scratch_shapes=[pltpu.VMEM((2, tm * SUBLANES, LANES), U32),
                            pltpu.VMEM((2, d, ff), F32), pltpu.VMEM((2, d, ff), F32), pltpu.VMEM((2, ff, d), F32),
                            pltpu.VMEM((d, ff), BF16), pltpu.VMEM((d, ff), BF16), pltpu.VMEM((ff, d), BF16),
                            pltpu.SemaphoreType.DMA((2,)), pltpu.SemaphoreType.DMA((2,))],
        ),
        out_shape=jax.ShapeDtypeStruct((n_rows * SUBLANES, LANES), U32),
        compiler_params=_cparams(("arbitrary",)),
        name="grouped_experts",
    )(block_expert, block_next, block_slot, row_tok, n_blocks_used, h1p_tiles, w_gate, w_up, w_down)


def _combine_kernel(dest_ref, h1_ref, h1p_ref, gate_ref, wsg_ref, wsu_ref, wsd_ref, g2_ref, b2_ref, ys_ref,
                    o_ref, buf_s, sem):
    tm = h1_ref.shape[0]
    i = pl.program_id(0)
    n_steps = pl.num_programs(0)
    n_tok = n_steps * tm
    slot = lax.rem(i, 2)

    def start_rows(step, sl):
        def issue(r, c):
            for k in range(TOP_K):
                row = dest_ref[k * n_tok + step * tm + r]
                pltpu.make_async_copy(ys_ref.at[pl.ds(row * SUBLANES, SUBLANES)],
                                      buf_s.at[sl, k, pl.ds(r * SUBLANES, SUBLANES)], sem.at[sl]).start()
            return c

        lax.fori_loop(0, tm, issue, 0, unroll=2)

    @pl.when(i == 0)
    def _():
        start_rows(0, 0)

    @pl.when(i + 1 < n_steps)
    def _():
        start_rows(i + 1, 1 - slot)

    xb = _unpack_bf16_pair(_load_row_tiles(h1p_ref, tm))
    a = jnp.dot(xb, wsg_ref[...], preferred_element_type=F32)
    u = jnp.dot(xb, wsu_ref[...], preferred_element_type=F32)
    shared = jnp.dot((jax.nn.silu(a) * u).astype(BF16), wsd_ref[...], preferred_element_type=F32)

    for k in range(TOP_K):
        pltpu.make_async_copy(ys_ref.at[pl.ds(0, tm * SUBLANES)], buf_s.at[slot, k], sem.at[slot]).wait()

    gate = gate_ref[...]
    routed = jnp.zeros_like(shared)
    for k in range(TOP_K):
        yk = _unpack_bf16_pair(_load_row_tiles(buf_s.at[slot, k], tm))
        routed = routed + yk.astype(F32) * gate[:, k:k + 1]
    o_ref[...] = _layer_norm(DN_ALPHA * h1_ref[...] + routed + shared, g2_ref[...], b2_ref[...])


def combine_shared_ln(dest_flat, h1, h1p_tiles, gate_nk, ys_tiles, wsg, wsu, wsd, g2, b2, tm=128):
    n, d = h1.shape
    ff = wsg.shape[1]
    row = lambda i, dest: (i, 0)
    const = lambda i, dest: (0, 0)
    return pl.pallas_call(
        _combine_kernel,
        grid_spec=pltpu.PrefetchScalarGridSpec(
            num_scalar_prefetch=1,
            grid=(n // tm,),
            in_specs=[pl.BlockSpec((tm, d), row), pl.BlockSpec((tm * SUBLANES, LANES), row),
                      pl.BlockSpec((tm, TOP_K), row),
                      pl.BlockSpec((d, ff), const), pl.BlockSpec((d, ff), const), pl.BlockSpec((ff, d), const),
                      pl.BlockSpec((1, d), const), pl.BlockSpec((1, d), const),
                      pl.BlockSpec(memory_space=pl.ANY)],
            out_specs=pl.BlockSpec((tm, d), row),
            scratch_shapes=[pltpu.VMEM((2, TOP_K, tm * SUBLANES, LANES), U32), pltpu.SemaphoreType.DMA((2,))],
        ),
        out_shape=jax.ShapeDtypeStruct((n, d), F32),
        compiler_params=_cparams(("arbitrary",)),
        name="combine_shared_ln",
    )(dest_flat, h1, h1p_tiles, gate_nk, wsg, wsu, wsd, g2.reshape(1, d), b2.reshape(1, d), ys_tiles)


def kernel(x, ln_emb_g, ln_emb_b, w_in, lb_logits, hg_norm_w, conv_w, conv_b, rg_wa, rg_ba, rg_wx, rg_bx,
           rg_lam, w_out, ln1_g, ln1_b, w_router, router_bias, w_gate, w_up, w_down, ws_gate, ws_up, ws_down,
           ln2_g, ln2_b):
    batch, seq, d = x.shape
    assert d == 2 * SUBLANES * LANES, "packed rows must fill one 8x128 tile"
    n_tok = batch * seq
    layer = 0
    x2 = x.reshape(n_tok, d)
    lb = jnp.cumsum(jax.nn.softmax(lb_logits.astype(F32), axis=1), axis=1)

    proj = ln_inproj(x2, ln_emb_g, ln_emb_b, w_in[layer].astype(BF16))
    y_hg = hgrn2_mixer(proj, lb[0, layer], lb[1, layer], hg_norm_w[layer], batch, seq)
    y_rg = rglru_mixer(proj, conv_w[layer], conv_b[layer], rg_wa[layer], rg_ba[layer], rg_wx[layer],
                       rg_bx[layer], rg_lam[layer], batch, seq)
    h1, h1p_tiles, logits_t = outproj_ln_router(
        y_hg, y_rg, x2, ln_emb_g, ln_emb_b, w_out[layer].astype(BF16), ln1_g[layer], ln1_b[layer],
        w_router[layer].T.astype(BF16))

    idx_t, gate_t = route_topk(logits_t, router_bias[layer])
    rank_t, counts = expert_ranks(idx_t)
    counts = counts[:, 0]

    n_assign = n_tok * TOP_K
    n_blocks = -(-(n_assign + N_EXPERTS * (MOE_BLOCK - 1)) // MOE_BLOCK)
    n_rows = n_blocks * MOE_BLOCK
    padded = (counts + MOE_BLOCK - 1) // MOE_BLOCK * MOE_BLOCK
    padded_end = jnp.cumsum(padded)
    padded_start = padded_end - padded
    n_used_blocks = (padded_end[-1:] // MOE_BLOCK).astype(I32)
    blk_row = jnp.minimum(jnp.arange(n_blocks, dtype=I32), n_used_blocks[0] - 1) * MOE_BLOCK
    block_expert = jnp.minimum(
        jnp.sum((padded_end[None, :] <= blk_row[:, None]).astype(I32), axis=1), N_EXPERTS - 1)
    e_ids = jnp.arange(N_EXPERTS, dtype=I32)
    active = counts > 0
    later = lax.cummin(jnp.where(active, e_ids, N_EXPERTS)[::-1])[::-1]
    nxt = jnp.concatenate([later[1:], jnp.full((1,), N_EXPERTS, I32)])
    next_active = jnp.where(nxt < N_EXPERTS, nxt, e_ids).astype(I32)
    ordinal = jnp.cumsum(active.astype(I32)) - 1
    onehot = (block_expert[:, None] == e_ids[None, :]).astype(I32)
    block_next = jnp.sum(onehot * next_active[None, :], axis=1)
    block_slot = jnp.sum(onehot * (ordinal % 2)[None, :], axis=1)

    dest_t = assignment_rows(idx_t, rank_t, padded_start)
    tok_ids = jnp.broadcast_to(jnp.arange(n_tok, dtype=I32)[None, :], (TOP_K, n_tok))
    row_tok = jnp.zeros((n_rows,), I32).at[dest_t.reshape(-1)].set(tok_ids.reshape(-1))

    ys_tiles = grouped_experts(h1p_tiles, row_tok, block_expert, block_next, block_slot, n_used_blocks,
                               w_gate[layer], w_up[layer], w_down[layer])
    out = combine_shared_ln(dest_t.reshape(-1), h1, h1p_tiles, gate_t.T, ys_tiles,
                            ws_gate[layer].astype(BF16), ws_up[layer].astype(BF16),
                            ws_down[layer].astype(BF16), ln2_g[layer], ln2_b[layer])
    return out.reshape(batch, seq, d)
```

```python
import functools

import numpy as np
import jax
import jax.numpy as jnp
from jax import lax
from jax.experimental import pallas as pl
from jax.experimental.pallas import tpu as pltpu

F32 = jnp.float32
BF16 = jnp.bfloat16
U32 = jnp.uint32
I32 = jnp.int32

D_MODEL = 2048
DEPTH = 1
HG_KEY_DIM = 128
HG_VAL_DIM = 128
HG_WIDTH = D_MODEL // 2
HG_HEADS = HG_WIDTH // HG_VAL_DIM
HG_KEY_WIDTH = HG_HEADS * HG_KEY_DIM
RG_WIDTH = D_MODEL - HG_WIDTH
RG_BLOCKS = 8
RG_BLOCK_DIM = RG_WIDTH // RG_BLOCKS
RG_CONV = 4
RG_C = 8.0
MIX_WIDTH = HG_WIDTH + RG_WIDTH
IN_WIDTHS = (HG_KEY_WIDTH, HG_KEY_WIDTH, HG_KEY_WIDTH, HG_WIDTH, HG_WIDTH, RG_WIDTH, RG_WIDTH)
IN_WIDTH = sum(IN_WIDTHS)
N_EXPERTS = 256
TOP_K = 8
N_GROUPS = 8
TOPK_GROUPS = 4
GROUP_SIZE = N_EXPERTS // N_GROUPS
EXPERT_FF = 512
SHARED_FF = 512
ROUTE_SCALE = 2.5
MOE_BLOCK = 128
DN_ALPHA = (2 * DEPTH) ** 0.25
LN_EPS = 1e-5
RMS_EPS = 1e-6

LANES = 128
SUBLANES = 8
VMEM_LIMIT_BYTES = 56 * 1024 * 1024

HG_CHUNK = 128
HG_SUB = 8
HG_LEVELS = (16, 32, 64, 128)


def _cparams(sem, vmem=VMEM_LIMIT_BYTES):
    return pltpu.CompilerParams(dimension_semantics=sem, vmem_limit_bytes=vmem)


def _layer_norm(x, g, b):
    mu = jnp.mean(x, axis=-1, keepdims=True)
    xc = x - mu
    var = jnp.mean(xc * xc, axis=-1, keepdims=True)
    return xc * lax.rsqrt(var + LN_EPS) * g + b


def _pack_bf16_pair(x):
    n = x.shape[1] // 2
    bits = lax.bitcast_convert_type(x.astype(BF16).astype(F32), U32)
    return (bits[:, :n] >> 16) | (bits[:, n:] & jnp.uint32(0xFFFF0000))


def _unpack_bf16_pair(w):
    lo = lax.bitcast_convert_type(w << 16, F32)
    hi = lax.bitcast_convert_type(w & jnp.uint32(0xFFFF0000), F32)
    return jnp.concatenate([lo, hi], axis=1).astype(BF16)


def _store_row_tiles(ref, packed):
    m = packed.shape[0]
    for s in range(SUBLANES):
        ref[pl.ds(s, m, stride=SUBLANES), :] = packed[:, s * LANES:(s + 1) * LANES]


def _load_row_tiles(ref, m):
    return jnp.concatenate([ref[pl.ds(s, m, stride=SUBLANES), :] for s in range(SUBLANES)], axis=1)


def _ln_inproj_kernel(x_ref, g_ref, b_ref, w_ref, o_ref, hb_ref):
    @pl.when(pl.program_id(1) == 0)
    def _():
        hb_ref[...] = _layer_norm(x_ref[...], g_ref[...], b_ref[...]).astype(BF16)

    o_ref[...] = jnp.dot(hb_ref[...], w_ref[...], preferred_element_type=F32).astype(o_ref.dtype)


def ln_inproj(x2, g, b, w_bf16, tm=512, tn=1024):
    n, d = x2.shape
    width = w_bf16.shape[1]
    return pl.pallas_call(
        _ln_inproj_kernel,
        grid=(n // tm, width // tn),
        in_specs=[
            pl.BlockSpec((tm, d), lambda i, j: (i, 0)),
            pl.BlockSpec((1, d), lambda i, j: (0, 0)),
            pl.BlockSpec((1, d), lambda i, j: (0, 0)),
            pl.BlockSpec((d, tn), lambda i, j: (0, j)),
        ],
        out_specs=pl.BlockSpec((tm, tn), lambda i, j: (i, j)),
        out_shape=jax.ShapeDtypeStruct((n, width), BF16),
        scratch_shapes=[pltpu.VMEM((tm, d), BF16)],
        compiler_params=_cparams(("parallel", "arbitrary")),
        name="ln_inproj",
    )(x2, g.reshape(1, d), b.reshape(1, d), w_bf16)


def _hgrn2_constants():
    c = HG_CHUNK
    t = np.arange(c)[:, None]
    u = np.arange(c)[None, :]
    fwd = [(u <= t), (u > t)]
    bwd = [(u >= t), (u < t)]
    for m in HG_LEVELS:
        start = (t // m) * m
        mid = start + m // 2 - 1
        second = t > mid
        fwd.append(np.where(second, (u > mid) & (u <= t), (u > t) & (u <= mid)))
        bwd.append(np.where(second, (u > mid) & (u < t), (u >= t) & (u <= mid)))
    fwd = np.concatenate(fwd, axis=0).astype(np.float32)
    bwd = np.concatenate(bwd, axis=0).astype(np.float32)
    fwd = np.concatenate([fwd, fwd], axis=1)
    bwd = np.concatenate([bwd, bwd], axis=1)
    lvl = np.full((c, c), -1, np.int32)
    tt = np.arange(c)[:, None]
    ss = np.arange(c)[None, :]
    for li in reversed(range(len(HG_LEVELS))):
        m = HG_LEVELS[li]
        lvl = np.where((tt // m) == (ss // m), li, lvl)
    lvl = np.where((tt // HG_SUB) == (ss // HG_SUB), -1, lvl).astype(np.int32)
    return fwd, bwd, lvl


def _hgrn2_kernel(q_ref, zf_ref, zb_ref, v_ref, g_ref, lbf_ref, lbb_ref, nw_ref,
                  mf_ref, mb_ref, lvl_ref, o_ref,
                  qs_s, v_s, lf_s, kf_s, lb_s, kb_s, cf_s, cb_s,
                  qif_s, kif_s, qib_s, kib_s, df_s, db_s, acc_s, stf_s, stb_s):
    seq = q_ref.shape[0]
    c = HG_CHUNK
    n_chunks = seq // c
    n_lvl = len(HG_LEVELS)

    qs_s[...] = jax.nn.silu(q_ref[...].astype(F32))
    v_s[...] = v_ref[...].astype(F32)
    for z_ref, lb_ref, lf_out, k_out in ((zf_ref, lbf_ref, lf_s, kf_s), (zb_ref, lbb_ref, lb_s, kb_s)):
        z = z_ref[...].astype(F32)
        lb = lb_ref[...]
        sig = jax.nn.sigmoid(z)
        lf_out[...] = jnp.log(lb + (1.0 - lb) * sig)
        k_out[...] = (1.0 - lb) * (1.0 - sig)

    ones_mat = jnp.ones((LANES, LANES), BF16)
    lvl = lvl_ref[...]
    row_c = lax.broadcasted_iota(I32, (c, LANES), 0)
    lane8 = lax.broadcasted_iota(I32, (HG_SUB, LANES), 1)
    sub8 = lax.broadcasted_iota(I32, (HG_SUB, LANES), 0)
    n_stack = (2 + n_lvl) * c

    def intra(n, carry):
        r0 = pl.multiple_of(n * c, c)
        rows = pl.ds(r0, c)
        qs = qs_s[rows, :]
        kf = kf_s[rows, :]
        kb = kb_s[rows, :]
        hl = []
        for lfs in (lf_s, lb_s):
            lf = lfs[rows, :]
            hi = lf.astype(BF16)
            hl.append(jnp.concatenate([hi, (lf - hi.astype(F32)).astype(BF16)], axis=0))
        ef = jnp.dot(mf_ref[...], hl[0], preferred_element_type=F32)
        eb = jnp.dot(mb_ref[...], hl[1], preferred_element_type=F32)
        for (e, kk, cs, qi_s, ki_s, d_s, tot_row) in ((ef, kf, cf_s, qif_s, kif_s, df_s, c - 1),
                                                    (eb, kb, cb_s, qib_s, kib_s, db_s, 0)):
            cum = e[0:c]
            cs[rows, :] = cum
            qi_s[rows, :] = (qs * jnp.exp(cum)).astype(BF16)
            ki_s[rows, :] = (kk * jnp.exp(e[c:2 * c])).astype(BF16)
            d_s[pl.ds(n, 1), :] = jnp.exp(cum[tot_row:tot_row + 1, :])

        a_mat = jnp.zeros((c, c), F32)
        for li, m in enumerate(HG_LEVELS):
            second = (row_c & (m // 2)) != 0
            wf = jnp.exp(ef[(2 + li) * c:(3 + li) * c])
            wb = jnp.exp(eb[(2 + li) * c:(3 + li) * c])
            qcat = jnp.concatenate([qs * jnp.where(second, wf, 0.0), qs * jnp.where(second, 0.0, wb)], axis=1)
            kcat = jnp.concatenate([kf * jnp.where(second, 0.0, wf), kb * jnp.where(second, wb, 0.0)], axis=1)
            sc = lax.dot_general(qcat.astype(BF16), kcat.astype(BF16),
                                 (((1,), (1,)), ((), ())), preferred_element_type=F32)
            a_mat = jnp.where(lvl == li, sc, a_mat)

        p_list = []
        for j in range(c // HG_SUB):
            rj = r0 + j * HG_SUB
            sl = pl.ds(rj, HG_SUB)
            cf = cf_s[sl, :]
            cb = cb_s[sl, :]
            qj = qs_s[sl, :]
            for s in range(HG_SUB):
                src = pl.ds(rj + s, 1)
                df = jnp.where(sub8 >= s, cf - cf_s[src, :], -jnp.inf)
                db = jnp.where(sub8 <= s, cb - cb_s[src, :], -jnp.inf)
                p_list.append(qj * (kf_s[src, :] * jnp.exp(df) + kb_s[src, :] * jnp.exp(db)))
        p_all = jnp.concatenate(p_list, axis=0).astype(BF16)
        red = jnp.dot(p_all, ones_mat, preferred_element_type=F32)
        diag_rows = []
        for j in range(c // HG_SUB):
            blk = jnp.zeros((HG_SUB, LANES), F32)
            for s in range(HG_SUB):
                idx = (j * HG_SUB + s) * HG_SUB
                blk = jnp.where(lane8 == j * HG_SUB + s, red[idx:idx + HG_SUB], blk)
            diag_rows.append(blk)
        a_mat = a_mat + jnp.concatenate(diag_rows, axis=0)
        acc_s[rows, :] = jnp.dot(a_mat.astype(BF16), v_s[rows, :].astype(BF16), preferred_element_type=F32)
        return carry

    lax.fori_loop(0, n_chunks, intra, 0, unroll=4)

    stf_s[...] = jnp.zeros_like(stf_s)
    stb_s[...] = jnp.zeros_like(stb_s)

    def inter(n, carry):
        for (st_s, qi_s, ki_s, d_s, idx) in ((stf_s, qif_s, kif_s, df_s, n),
                                             (stb_s, qib_s, kib_s, db_s, n_chunks - 1 - n)):
            rows = pl.ds(pl.multiple_of(idx * c, c), c)
            st = st_s[...]
            acc_s[rows, :] += lax.dot_general(qi_s[rows, :], st.astype(BF16),
                                              (((1,), (1,)), ((), ())), preferred_element_type=F32)
            upd = lax.dot_general(v_s[rows, :].astype(BF16), ki_s[rows, :],
                                  (((0,), (0,)), ((), ())), preferred_element_type=F32)
            st_s[...] = st * d_s[pl.ds(idx, 1), :] + upd
        return carry

    lax.fori_loop(0, n_chunks, inter, 0)

    o = acc_s[...]
    o = o * lax.rsqrt(jnp.mean(o * o, axis=-1, keepdims=True) + RMS_EPS) * nw_ref[...]
    o_ref[...] = (o * jax.nn.silu(g_ref[...].astype(F32))).astype(o_ref.dtype)


def hgrn2_mixer(proj, lb_f, lb_b, norm_w, batch, seq):
    n = proj.shape[0]
    heads = HG_HEADS
    c = HG_CHUNK
    n_chunks = seq // c
    mf, mb, lvl = _hgrn2_constants()
    n_stack = mf.shape[0]

    def col(off):
        return pl.BlockSpec((seq, LANES), lambda b, h, off=off: (b, off + h))

    def per_head(arr):
        return pl.BlockSpec((1, LANES), lambda b, h: (0, h))

    const2 = lambda b, h: (0, 0)
    seq_f32 = pltpu.VMEM((seq, LANES), F32)
    seq_bf16 = pltpu.VMEM((seq, LANES), BF16)
    dec = pltpu.VMEM((n_chunks, LANES), F32)
    state = pltpu.VMEM((LANES, LANES), F32)
    return pl.pallas_call(
        _hgrn2_kernel,
        grid=(batch, heads),
        in_specs=[col(0), col(heads), col(2 * heads), col(3 * heads), col(4 * heads),
                  per_head(lb_f), per_head(lb_b),
                  pl.BlockSpec((1, LANES), const2),
                  pl.BlockSpec((n_stack, 2 * c), const2),
                  pl.BlockSpec((n_stack, 2 * c), const2),
                  pl.BlockSpec((c, c), const2)],
        out_specs=pl.BlockSpec((seq, LANES), lambda b, h: (b, h)),
        out_shape=jax.ShapeDtypeStruct((n, HG_WIDTH), BF16),
        scratch_shapes=[seq_f32] * 8 + [seq_bf16] * 4 + [dec, dec, seq_f32, state, state],
        compiler_params=_cparams(("parallel", "parallel")),
        name="hgrn2_mixer",
    )(proj, proj, proj, proj, proj, lb_f.reshape(1, -1), lb_b.reshape(1, -1), norm_w.reshape(1, -1),
      jnp.asarray(mf, BF16), jnp.asarray(mb, BF16), jnp.asarray(lvl))


def _shift_rows(x, d, fill, reverse):
    pad = jnp.full((d, x.shape[1]), fill, x.dtype)
    if reverse:
        return jnp.concatenate([x[d:], pad], axis=0)
    return jnp.concatenate([pad, x[:-d]], axis=0)


def _linear_scan(a, u, reverse):
    seq = a.shape[0]
    d = 1
    while d < seq:
        u = u + a * _shift_rows(u, d, 0.0, reverse)
        if d * 2 < seq:
            a = a * _shift_rows(a, d, 1.0, reverse)
        d *= 2
    return u


def _rglru_kernel(xr_ref, gr_ref, cw_ref, cb_ref, wa_ref, ba_ref, wx_ref, bx_ref, lam_ref, o_ref, xp_s):
    seq = xr_ref.shape[0]
    pad = SUBLANES
    pad_l = RG_CONV // 2
    xp_s[0:pad, :] = jnp.zeros((pad, LANES), F32)
    xp_s[pad + seq:, :] = jnp.zeros((pad, LANES), F32)
    xp_s[pad:pad + seq, :] = xr_ref[...].astype(F32)
    xc = cb_ref[...]
    for j in range(RG_CONV):
        off = pad + j - pad_l
        xc = xc + xp_s[off:off + seq, :] * cw_ref[j:j + 1, :]
    xcb = xc.astype(BF16)
    h = None
    for d in range(2):
        r = jax.nn.sigmoid(jnp.dot(xcb, wa_ref[d, 0], preferred_element_type=F32) + ba_ref[d:d + 1, :])
        i = jax.nn.sigmoid(jnp.dot(xcb, wx_ref[d, 0], preferred_element_type=F32) + bx_ref[d:d + 1, :])
        log_a = (-RG_C) * r * jax.nn.softplus(-lam_ref[d:d + 1, :])
        a = jnp.exp(log_a)
        u = jnp.sqrt(-jnp.tanh(log_a) * (a * a + 1.0)) * (i * xc)
        hd = _linear_scan(a, u, reverse=(d == 1))
        h = hd if h is None else h + hd
    o_ref[...] = (h * jax.nn.gelu(gr_ref[...].astype(F32))).astype(o_ref.dtype)


def rglru_mixer(proj, conv_w, conv_b, wa, ba, wx, bx, lam, batch, seq):
    n = proj.shape[0]
    x_off = (3 * HG_KEY_WIDTH + 2 * HG_WIDTH) // LANES
    g_off = x_off + RG_WIDTH // LANES
    blk = lambda b, j: (0, j)
    return pl.pallas_call(
        _rglru_kernel,
        grid=(batch, RG_BLOCKS),
        in_specs=[
            pl.BlockSpec((seq, LANES), lambda b, j: (b, x_off + j)),
            pl.BlockSpec((seq, LANES), lambda b, j: (b, g_off + j)),
            pl.BlockSpec((RG_CONV, LANES), blk),
            pl.BlockSpec((1, LANES), blk),
            pl.BlockSpec((2, 1, RG_BLOCK_DIM, RG_BLOCK_DIM), lambda b, j: (0, j, 0, 0)),
            pl.BlockSpec((2, LANES), blk),
            pl.BlockSpec((2, 1, RG_BLOCK_DIM, RG_BLOCK_DIM), lambda b, j: (0, j, 0, 0)),
            pl.BlockSpec((2, LANES), blk),
            pl.BlockSpec((2, LANES), blk),
        ],
        out_specs=pl.BlockSpec((seq, LANES), lambda b, j: (b, j)),
        out_shape=jax.ShapeDtypeStruct((n, RG_WIDTH), BF16),
        scratch_shapes=[pltpu.VMEM((seq + 2 * SUBLANES, LANES), F32)],
        compiler_params=_cparams(("parallel", "parallel")),
        name="rglru_mixer",
    )(proj, proj, conv_w, conv_b.reshape(1, -1), wa.astype(BF16), ba, wx.astype(BF16), bx, lam)


def _outproj_kernel(yh_ref, yr_ref, x_ref, g0_ref, b0_ref, wo_ref, g1_ref, b1_ref, wr_ref,
                    h1_ref, h1p_ref, lg_ref):
    half = yh_ref.shape[1]
    h0 = _layer_norm(x_ref[...], g0_ref[...], b0_ref[...])
    mix = jnp.dot(yh_ref[...], wo_ref[0:half, :], preferred_element_type=F32)
    mix = mix + jnp.dot(yr_ref[...], wo_ref[half:, :], preferred_element_type=F32)
    h1 = _layer_norm(DN_ALPHA * h0 + mix, g1_ref[...], b1_ref[...])
    h1_ref[...] = h1
    _store_row_tiles(h1p_ref, _pack_bf16_pair(h1))
    lg_ref[...] = lax.dot_general(wr_ref[...], h1.astype(BF16), (((1,), (1,)), ((), ())),
                                  preferred_element_type=F32)


def outproj_ln_router(y_hg, y_rg, x2, g0, b0, w_out_bf16, g1, b1, w_router_t_bf16, tm=256):
    n, d = x2.shape
    half = y_hg.shape[1]
    n_e = w_router_t_bf16.shape[0]
    row = lambda i: (i, 0)
    const = lambda i: (0, 0)
    vec = pl.BlockSpec((1, d), const)
    return pl.pallas_call(
        _outproj_kernel,
        grid=(n // tm,),
        in_specs=[pl.BlockSpec((tm, half), row), pl.BlockSpec((tm, half), row), pl.BlockSpec((tm, d), row),
                  vec, vec, pl.BlockSpec((2 * half, d), const), vec, vec,
                  pl.BlockSpec((n_e, d), const)],
        out_specs=[pl.BlockSpec((tm, d), row), pl.BlockSpec((tm * SUBLANES, LANES), row),
                   pl.BlockSpec((n_e, tm), lambda i: (0, i))],
        out_shape=[jax.ShapeDtypeStruct((n, d), F32), jax.ShapeDtypeStruct((n * SUBLANES, LANES), U32),
                   jax.ShapeDtypeStruct((n_e, n), F32)],
        compiler_params=_cparams(("parallel",)),
        name="outproj_ln_router",
    )(y_hg, y_rg, x2, g0.reshape(1, d), b0.reshape(1, d), w_out_bf16, g1.reshape(1, d), b1.reshape(1, d),
      w_router_t_bf16)


def _first_argmax(vals, iota, n):
    m = jnp.max(vals, axis=0, keepdims=True)
    first = jnp.min(jnp.where(vals == m, iota, n), axis=0, keepdims=True)
    return m, first


def _route_kernel(lg_ref, bias_ref, idx_ref, gate_ref):
    n_e, tn = lg_ref.shape
    scores = jax.nn.sigmoid(lg_ref[...])
    choice = scores + bias_ref[...]
    neg = -jnp.inf
    gi = lax.broadcasted_iota(I32, (GROUP_SIZE, tn), 0)
    gs_rows = []
    for g in range(N_GROUPS):
        blk = choice[g * GROUP_SIZE:(g + 1) * GROUP_SIZE]
        m1, f1 = _first_argmax(blk, gi, GROUP_SIZE)
        m2 = jnp.max(jnp.where(gi == f1, neg, blk), axis=0, keepdims=True)
        gs_rows.append(m1 + m2)
    gs = jnp.concatenate(gs_rows, axis=0)
    ri = lax.broadcasted_iota(I32, (N_GROUPS, tn), 0)
    gmask = jnp.zeros((N_GROUPS, tn), jnp.bool_)
    for _ in range(TOPK_GROUPS):
        _, f = _first_argmax(gs, ri, N_GROUPS)
        hit = ri == f
        gmask = gmask | hit
        gs = jnp.where(hit, neg, gs)
    gm = jnp.where(gmask, 1.0, 0.0)
    emask = jnp.concatenate(
        [jnp.broadcast_to(gm[g:g + 1], (GROUP_SIZE, tn)) for g in range(N_GROUPS)], axis=0)
    masked = jnp.where(emask > 0.5, choice, neg)
    ei = lax.broadcasted_iota(I32, (n_e, tn), 0)
    idx_rows, gate_rows = [], []
    for _ in range(TOP_K):
        _, f = _first_argmax(masked, ei, n_e)
        hit = ei == f
        idx_rows.append(f)
        gate_rows.append(jnp.sum(jnp.where(hit, scores, 0.0), axis=0, keepdims=True))
        masked = jnp.where(hit, neg, masked)
    gate = jnp.concatenate(gate_rows, axis=0)
    gate = gate / jnp.sum(gate, axis=0, keepdims=True) * ROUTE_SCALE
    idx_ref[...] = jnp.concatenate(idx_rows, axis=0)
    gate_ref[...] = gate


def route_topk(logits_t, bias, tn=512):
    n_e, n = logits_t.shape
    return pl.pallas_call(
        _route_kernel,
        grid=(n // tn,),
        in_specs=[pl.BlockSpec((n_e, tn), lambda i: (0, i)), pl.BlockSpec((n_e, 1), lambda i: (0, 0))],
        out_specs=[pl.BlockSpec((TOP_K, tn), lambda i: (0, i)), pl.BlockSpec((TOP_K, tn), lambda i: (0, i))],
        out_shape=[jax.ShapeDtypeStruct((TOP_K, n), I32), jax.ShapeDtypeStruct((TOP_K, n), F32)],
        compiler_params=_cparams(("parallel",)),
        name="route_topk",
    )(logits_t, bias.reshape(n_e, 1))


def _rank_kernel(idx_ref, su_ref, rank_ref, cnt_ref, carry_s):
    n_e = cnt_ref.shape[0]
    tn = idx_ref.shape[1]

    @pl.when(pl.program_id(0) == 0)
    def _():
        carry_s[...] = jnp.zeros_like(carry_s)

    idx = idx_ref[...]
    ei = lax.broadcasted_iota(I32, (n_e, tn), 0)
    hits = [ei == idx[k:k + 1] for k in range(TOP_K)]
    member = jnp.zeros((n_e, tn), F32)
    for h in hits:
        member = member + jnp.where(h, 1.0, 0.0)
    prefix = jnp.dot(member.astype(BF16), su_ref[...], preferred_element_type=F32) + carry_s[:, 0:1]
    rank_ref[...] = jnp.concatenate(
        [jnp.sum(jnp.where(h, prefix, 0.0), axis=0, keepdims=True) for h in hits], axis=0).astype(I32)
    carry_s[...] = carry_s[...] + jnp.sum(member, axis=1, keepdims=True)
    cnt_ref[...] = carry_s[...].astype(I32)


def expert_ranks(idx_t, tn=512):
    k, n = idx_t.shape
    su = jnp.asarray(np.triu(np.ones((tn, tn), np.float32), 1), BF16)
    return pl.pallas_call(
        _rank_kernel,
        grid=(n // tn,),
        in_specs=[pl.BlockSpec((k, tn), lambda i: (0, i)), pl.BlockSpec((tn, tn), lambda i: (0, 0))],
        out_specs=[pl.BlockSpec((k, tn), lambda i: (0, i)), pl.BlockSpec((N_EXPERTS, LANES), lambda i: (0, 0))],
        out_shape=[jax.ShapeDtypeStruct((k, n), I32), jax.ShapeDtypeStruct((N_EXPERTS, LANES), I32)],
        scratch_shapes=[pltpu.VMEM((N_EXPERTS, LANES), F32)],
        compiler_params=_cparams(("arbitrary",)),
        name="expert_ranks",
    )(idx_t, su)


def _dest_kernel(idx_ref, rank_ref, start_ref, dest_ref):
    n_e = start_ref.shape[0]
    tn = idx_ref.shape[1]
    idx = idx_ref[...]
    ei = lax.broadcasted_iota(I32, (n_e, tn), 0)
    start = start_ref[...]
    base = jnp.concatenate(
        [jnp.sum(jnp.where(ei == idx[k:k + 1], start, 0.0), axis=0, keepdims=True) for k in range(TOP_K)],
        axis=0)
    dest_ref[...] = base.astype(I32) + rank_ref[...]


def assignment_rows(idx_t, rank_t, padded_start, tn=512):
    k, n = idx_t.shape
    blk = pl.BlockSpec((k, tn), lambda i: (0, i))
    return pl.pallas_call(
        _dest_kernel,
        grid=(n // tn,),
        in_specs=[blk, blk, pl.BlockSpec((N_EXPERTS, 1), lambda i: (0, 0))],
        out_specs=blk,
        out_shape=jax.ShapeDtypeStruct((k, n), I32),
        compiler_params=_cparams(("parallel",)),
        name="assignment_rows",
    )(idx_t, rank_t, padded_start.astype(F32).reshape(N_EXPERTS, 1))


WEIGHT_DMA_PRIORITY = 1
ROW_PREFETCH = 2
ROW_SLOTS = ROW_PREFETCH + 1


def _rowmap_kernel(dest_ref, tok_ref):
    n_tok = dest_ref.shape[0] // TOP_K

    def body(j, c):
        for k in range(TOP_K):
            tok_ref[dest_ref[k * n_tok + j]] = j
        return c

    lax.fori_loop(0, n_tok, body, 0, unroll=4)


def row_tokens(dest_flat, n_rows):
    return pl.pallas_call(
        _rowmap_kernel,
        grid_spec=pltpu.PrefetchScalarGridSpec(
            num_scalar_prefetch=1,
            grid=(1,),
            in_specs=[],
            out_specs=pl.BlockSpec(memory_space=pltpu.SMEM),
        ),
        out_shape=jax.ShapeDtypeStruct((n_rows,), I32),
        compiler_params=_cparams(("arbitrary",)),
        name="row_tokens",
    )(dest_flat)


def _experts_kernel(be_ref, nxt_ref, slot_ref, cnt_ref, tok_ref, nb_ref,
                    h1p_ref, wg_ref, wu_ref, wd_ref, ys_ref,
                    xbuf, wgb, wub, wdb, wg_s, wu_s, wd_s, gsem, wsem):
    tm = ys_ref.shape[0] // SUBLANES
    i = pl.program_id(0)
    nb = nb_ref[0]

    def weight_copies(e, slot):
        return (pltpu.make_async_copy(wg_ref.at[e], wgb.at[slot], wsem.at[slot]),
                pltpu.make_async_copy(wu_ref.at[e], wub.at[slot], wsem.at[slot]),
                pltpu.make_async_copy(wd_ref.at[e], wdb.at[slot], wsem.at[slot]))

    def start_rows(blk, slot):
        last = cnt_ref[blk] - 1

        def issue(r, c):
            t = tok_ref[blk * tm + jnp.minimum(r, last)]
            pltpu.make_async_copy(h1p_ref.at[pl.ds(t * SUBLANES, SUBLANES)],
                                  xbuf.at[slot, pl.ds(r * SUBLANES, SUBLANES)], gsem.at[slot]).start()
            return c

        lax.fori_loop(0, tm, issue, 0, unroll=8)

    @pl.when(i < nb)
    def _():
        e = be_ref[i]
        wslot = slot_ref[i]
        xslot = lax.rem(i, ROW_SLOTS)

        @pl.when(i == 0)
        def _():
            for cp in weight_copies(e, wslot):
                cp.start(priority=WEIGHT_DMA_PRIORITY)
            for b in range(ROW_PREFETCH):
                @pl.when(b < nb)
                def _(b=b):
                    start_rows(b, b)

        @pl.when((i == 0) | (e != be_ref[jnp.maximum(i - 1, 0)]))
        def _():
            for cp in weight_copies(e, wslot):
                cp.wait()
            nxt = nxt_ref[i]

            @pl.when(nxt != e)
            def _():
                for cp in weight_copies(nxt, 1 - wslot):
                    cp.start(priority=WEIGHT_DMA_PRIORITY)

            wg_s[...] = wgb[wslot].astype(BF16)
            wu_s[...] = wub[wslot].astype(BF16)
            wd_s[...] = wdb[wslot].astype(BF16)

        @pl.when(i + ROW_PREFETCH < nb)
        def _():
            start_rows(i + ROW_PREFETCH, lax.rem(i + ROW_PREFETCH, ROW_SLOTS))

        pltpu.make_async_copy(h1p_ref.at[pl.ds(0, tm * SUBLANES)], xbuf.at[xslot], gsem.at[xslot]).wait()
        x = _unpack_bf16_pair(_load_row_tiles(xbuf.at[xslot], tm))
        a = jnp.dot(x, wg_s[...], preferred_element_type=F32)
        u = jnp.dot(x, wu_s[...], preferred_element_type=F32)
        hid = (jax.nn.silu(a) * u).astype(BF16)
        y = jnp.dot(hid, wd_s[...], preferred_element_type=F32)
        _store_row_tiles(ys_ref, _pack_bf16_pair(y))


def grouped_experts(h1p_tiles, row_tok, block_expert, block_next, block_slot, block_count, n_blocks_used,
                    w_gate, w_up, w_down, tm=MOE_BLOCK):
    n_rows = row_tok.shape[0]
    n_e, d, ff = w_gate.shape
    n_blocks = n_rows // tm
    rowmap = lambda i, be, nx, sl, cn, tok, nb: (jnp.minimum(i, jnp.maximum(nb[0] - 1, 0)), 0)
    hbm = pl.BlockSpec(memory_space=pl.ANY)
    return pl.pallas_call(
        _experts_kernel,
        grid_spec=pltpu.PrefetchScalarGridSpec(
            num_scalar_prefetch=6,
            grid=(n_blocks,),
            in_specs=[hbm, hbm, hbm, hbm],
            out_specs=pl.BlockSpec((tm * SUBLANES, LANES), rowmap),
            scratch_shapes=[pltpu.VMEM((ROW_SLOTS, tm * SUBLANES, LANES), U32),
                            pltpu.VMEM((2, d, ff), F32), pltpu.VMEM((2, d, ff), F32), pltpu.VMEM((2, ff, d), F32),
                            pltpu.VMEM((d, ff), BF16), pltpu.VMEM((d, ff), BF16), pltpu.VMEM((ff, d), BF16),
                            pltpu.SemaphoreType.DMA((ROW_SLOTS,)), pltpu.SemaphoreType.DMA((2,))],
        ),
        out_shape=jax.ShapeDtypeStruct((n_rows * SUBLANES, LANES), U32),
        compiler_params=_cparams(("arbitrary",)),
        name="grouped_experts",
    )(block_expert, block_next, block_slot, block_count, row_tok, n_blocks_used, h1p_tiles, w_gate, w_up, w_down)


def _combine_kernel(dest_ref, h1_ref, h1p_ref, gate_ref, wsg_ref, wsu_ref, wsd_ref, g2_ref, b2_ref, ys_ref,
                    o_ref, buf_s, sem):
    tm = h1_ref.shape[0]
    i = pl.program_id(0)
    n_steps = pl.num_programs(0)
    n_tok = n_steps * tm
    slot = lax.rem(i, 2)

    def start_rows(step, sl):
        def issue(r, c):
            for k in range(TOP_K):
                row = dest_ref[k * n_tok + step * tm + r]
                pltpu.make_async_copy(ys_ref.at[pl.ds(row * SUBLANES, SUBLANES)],
                                      buf_s.at[sl, k, pl.ds(r * SUBLANES, SUBLANES)], sem.at[sl]).start()
            return c

        lax.fori_loop(0, tm, issue, 0, unroll=2)

    @pl.when(i == 0)
    def _():
        start_rows(0, 0)

    @pl.when(i + 1 < n_steps)
    def _():
        start_rows(i + 1, 1 - slot)

    xb = _unpack_bf16_pair(_load_row_tiles(h1p_ref, tm))
    a = jnp.dot(xb, wsg_ref[...], preferred_element_type=F32)
    u = jnp.dot(xb, wsu_ref[...], preferred_element_type=F32)
    shared = jnp.dot((jax.nn.silu(a) * u).astype(BF16), wsd_ref[...], preferred_element_type=F32)

    for k in range(TOP_K):
        pltpu.make_async_copy(ys_ref.at[pl.ds(0, tm * SUBLANES)], buf_s.at[slot, k], sem.at[slot]).wait()

    gate = gate_ref[...]
    routed = jnp.zeros_like(shared)
    for k in range(TOP_K):
        yk = _unpack_bf16_pair(_load_row_tiles(buf_s.at[slot, k], tm))
        routed = routed + yk.astype(F32) * gate[:, k:k + 1]
    o_ref[...] = _layer_norm(DN_ALPHA * h1_ref[...] + routed + shared, g2_ref[...], b2_ref[...])


def combine_shared_ln(dest_flat, h1, h1p_tiles, gate_nk, ys_tiles, wsg, wsu, wsd, g2, b2, tm=128):
    n, d = h1.shape
    ff = wsg.shape[1]
    row = lambda i, dest: (i, 0)
    const = lambda i, dest: (0, 0)
    return pl.pallas_call(
        _combine_kernel,
        grid_spec=pltpu.PrefetchScalarGridSpec(
            num_scalar_prefetch=1,
            grid=(n // tm,),
            in_specs=[pl.BlockSpec((tm, d), row), pl.BlockSpec((tm * SUBLANES, LANES), row),
                      pl.BlockSpec((tm, TOP_K), row),
                      pl.BlockSpec((d, ff), const), pl.BlockSpec((d, ff), const), pl.BlockSpec((ff, d), const),
                      pl.BlockSpec((1, d), const), pl.BlockSpec((1, d), const),
                      pl.BlockSpec(memory_space=pl.ANY)],
            out_specs=pl.BlockSpec((tm, d), row),
            scratch_shapes=[pltpu.VMEM((2, TOP_K, tm * SUBLANES, LANES), U32), pltpu.SemaphoreType.DMA((2,))],
        ),
        out_shape=jax.ShapeDtypeStruct((n, d), F32),
        compiler_params=_cparams(("arbitrary",)),
        name="combine_shared_ln",
    )(dest_flat, h1, h1p_tiles, gate_nk, wsg, wsu, wsd, g2.reshape(1, d), b2.reshape(1, d), ys_tiles)


def kernel(x, ln_emb_g, ln_emb_b, w_in, lb_logits, hg_norm_w, conv_w, conv_b, rg_wa, rg_ba, rg_wx, rg_bx,
           rg_lam, w_out, ln1_g, ln1_b, w_router, router_bias, w_gate, w_up, w_down, ws_gate, ws_up, ws_down,
           ln2_g, ln2_b):
    batch, seq, d = x.shape
    assert d == 2 * SUBLANES * LANES, "packed rows must fill one 8x128 tile"
    n_tok = batch * seq
    layer = 0
    x2 = x.reshape(n_tok, d)
    lb = jnp.cumsum(jax.nn.softmax(lb_logits.astype(F32), axis=1), axis=1)

    proj = ln_inproj(x2, ln_emb_g, ln_emb_b, w_in[layer].astype(BF16))
    y_hg = hgrn2_mixer(proj, lb[0, layer], lb[1, layer], hg_norm_w[layer], batch, seq)
    y_rg = rglru_mixer(proj, conv_w[layer], conv_b[layer], rg_wa[layer], rg_ba[layer], rg_wx[layer],
                       rg_bx[layer], rg_lam[layer], batch, seq)
    h1, h1p_tiles, logits_t = outproj_ln_router(
        y_hg, y_rg, x2, ln_emb_g, ln_emb_b, w_out[layer].astype(BF16), ln1_g[layer], ln1_b[layer],
        w_router[layer].T.astype(BF16))

    idx_t, gate_t = route_topk(logits_t, router_bias[layer])
    rank_t, counts = expert_ranks(idx_t)
    counts = counts[:, 0]

    n_assign = n_tok * TOP_K
    n_blocks = -(-(n_assign + N_EXPERTS * (MOE_BLOCK - 1)) // MOE_BLOCK)
    n_rows = n_blocks * MOE_BLOCK
    padded = (counts + MOE_BLOCK - 1) // MOE_BLOCK * MOE_BLOCK
    padded_end = jnp.cumsum(padded)
    padded_start = padded_end - padded
    n_used_blocks = (padded_end[-1:] // MOE_BLOCK).astype(I32)
    blk_row = jnp.minimum(jnp.arange(n_blocks, dtype=I32), n_used_blocks[0] - 1) * MOE_BLOCK
    block_expert = jnp.minimum(
        jnp.sum((padded_end[None, :] <= blk_row[:, None]).astype(I32), axis=1), N_EXPERTS - 1)
    e_ids = jnp.arange(N_EXPERTS, dtype=I32)
    active = counts > 0
    later = lax.cummin(jnp.where(active, e_ids, N_EXPERTS)[::-1])[::-1]
    nxt = jnp.concatenate([later[1:], jnp.full((1,), N_EXPERTS, I32)])
    next_active = jnp.where(nxt < N_EXPERTS, nxt, e_ids).astype(I32)
    ordinal = jnp.cumsum(active.astype(I32)) - 1
    onehot = (block_expert[:, None] == e_ids[None, :]).astype(I32)
    block_next = jnp.sum(onehot * next_active[None, :], axis=1)
    block_slot = jnp.sum(onehot * (ordinal % 2)[None, :], axis=1)
    block_count = jnp.clip(jnp.sum(onehot * (counts + padded_start)[None, :], axis=1) - blk_row, 1, MOE_BLOCK)

    dest_flat = assignment_rows(idx_t, rank_t, padded_start).reshape(-1)
    row_tok = row_tokens(dest_flat, n_rows)

    ys_tiles = grouped_experts(h1p_tiles, row_tok, block_expert, block_next, block_slot, block_count,
                               n_used_blocks, w_gate[layer], w_up[layer], w_down[layer])
    out = combine_shared_ln(dest_flat, h1, h1p_tiles, gate_t.T, ys_tiles,
                            ws_gate[layer].astype(BF16), ws_up[layer].astype(BF16),
                            ws_down[layer].astype(BF16), ln2_g[layer], ln2_b[layer])
    return out.reshape(batch, seq, d)
```

```python
import functools

import numpy as np
import jax
import jax.numpy as jnp
from jax import lax
from jax.experimental import pallas as pl
from jax.experimental.pallas import tpu as pltpu

F32 = jnp.float32
BF16 = jnp.bfloat16
U32 = jnp.uint32
I32 = jnp.int32

D_MODEL = 2048
DEPTH = 1
HG_KEY_DIM = 128
HG_VAL_DIM = 128
HG_WIDTH = D_MODEL // 2
HG_HEADS = HG_WIDTH // HG_VAL_DIM
HG_KEY_WIDTH = HG_HEADS * HG_KEY_DIM
RG_WIDTH = D_MODEL - HG_WIDTH
RG_BLOCKS = 8
RG_BLOCK_DIM = RG_WIDTH // RG_BLOCKS
RG_CONV = 4
RG_C = 8.0
MIX_WIDTH = HG_WIDTH + RG_WIDTH
IN_WIDTHS = (HG_KEY_WIDTH, HG_KEY_WIDTH, HG_KEY_WIDTH, HG_WIDTH, HG_WIDTH, RG_WIDTH, RG_WIDTH)
IN_WIDTH = sum(IN_WIDTHS)
N_EXPERTS = 256
TOP_K = 8
N_GROUPS = 8
TOPK_GROUPS = 4
GROUP_SIZE = N_EXPERTS // N_GROUPS
EXPERT_FF = 512
SHARED_FF = 512
ROUTE_SCALE = 2.5
MOE_BLOCK = 128
DN_ALPHA = (2 * DEPTH) ** 0.25
LN_EPS = 1e-5
RMS_EPS = 1e-6

LANES = 128
SUBLANES = 8
VMEM_LIMIT_BYTES = 56 * 1024 * 1024

HG_CHUNK = 128
HG_SUB = 8
HG_LEVELS = (16, 32, 64, 128)


def _cparams(sem, vmem=VMEM_LIMIT_BYTES):
    return pltpu.CompilerParams(dimension_semantics=sem, vmem_limit_bytes=vmem)


def _layer_norm(x, g, b):
    mu = jnp.mean(x, axis=-1, keepdims=True)
    xc = x - mu
    var = jnp.mean(xc * xc, axis=-1, keepdims=True)
    return xc * lax.rsqrt(var + LN_EPS) * g + b


def _pack_bf16_pair(x):
    n = x.shape[1] // 2
    bits = lax.bitcast_convert_type(x.astype(BF16).astype(F32), U32)
    return (bits[:, :n] >> 16) | (bits[:, n:] & jnp.uint32(0xFFFF0000))


def _unpack_pair_f32(w):
    lo = lax.bitcast_convert_type(w << 16, F32)
    hi = lax.bitcast_convert_type(w & jnp.uint32(0xFFFF0000), F32)
    return jnp.concatenate([lo, hi], axis=1)


def _unpack_bf16_pair(w):
    return _unpack_pair_f32(w).astype(BF16)


def _store_row_tiles(ref, packed):
    m = packed.shape[0]
    for s in range(SUBLANES):
        ref[pl.ds(s, m, stride=SUBLANES), :] = packed[:, s * LANES:(s + 1) * LANES]


def _load_row_tiles(ref, m):
    return jnp.concatenate([ref[pl.ds(s, m, stride=SUBLANES), :] for s in range(SUBLANES)], axis=1)


def _ln_inproj_kernel(x_ref, g_ref, b_ref, w_ref, o_ref, hb_ref):
    @pl.when(pl.program_id(1) == 0)
    def _():
        hb_ref[...] = _layer_norm(x_ref[...], g_ref[...], b_ref[...]).astype(BF16)

    o_ref[...] = jnp.dot(hb_ref[...], w_ref[...], preferred_element_type=F32).astype(o_ref.dtype)


def ln_inproj(x2, g, b, w_bf16, tm=1024, tn=1024):
    n, d = x2.shape
    width = w_bf16.shape[1]
    return pl.pallas_call(
        _ln_inproj_kernel,
        grid=(n // tm, width // tn),
        in_specs=[
            pl.BlockSpec((tm, d), lambda i, j: (i, 0)),
            pl.BlockSpec((1, d), lambda i, j: (0, 0)),
            pl.BlockSpec((1, d), lambda i, j: (0, 0)),
            pl.BlockSpec((d, tn), lambda i, j: (0, j)),
        ],
        out_specs=pl.BlockSpec((tm, tn), lambda i, j: (i, j)),
        out_shape=jax.ShapeDtypeStruct((n, width), BF16),
        scratch_shapes=[pltpu.VMEM((tm, d), BF16)],
        compiler_params=_cparams(("parallel", "arbitrary")),
        name="ln_inproj",
    )(x2, g.reshape(1, d), b.reshape(1, d), w_bf16)


def _hgrn2_constants():
    c = HG_CHUNK
    t = np.arange(c)[:, None]
    u = np.arange(c)[None, :]
    fwd = [(u <= t), (u > t)]
    bwd = [(u >= t), (u < t)]
    for m in HG_LEVELS:
        start = (t // m) * m
        mid = start + m // 2 - 1
        second = t > mid
        fwd.append(np.where(second, (u > mid) & (u <= t), (u > t) & (u <= mid)))
        bwd.append(np.where(second, (u > mid) & (u < t), (u >= t) & (u <= mid)))
    fwd = np.concatenate(fwd, axis=0).astype(np.float32)
    bwd = np.concatenate(bwd, axis=0).astype(np.float32)
    fwd = np.concatenate([fwd, fwd], axis=1)
    bwd = np.concatenate([bwd, bwd], axis=1)
    lvl = np.full((c, c), -1, np.int32)
    tt = np.arange(c)[:, None]
    ss = np.arange(c)[None, :]
    for li in reversed(range(len(HG_LEVELS))):
        m = HG_LEVELS[li]
        lvl = np.where((tt // m) == (ss // m), li, lvl)
    lvl = np.where((tt // HG_SUB) == (ss // HG_SUB), -1, lvl).astype(np.int32)
    return fwd, bwd, lvl


def _hgrn2_kernel(q_ref, zf_ref, zb_ref, v_ref, g_ref, lbf_ref, lbb_ref, nw_ref,
                  mf_ref, mb_ref, lvl_ref, o_ref,
                  qs_s, v_s, lf_s, kf_s, lb_s, kb_s, cf_s, cb_s,
                  qif_s, kif_s, qib_s, kib_s, df_s, db_s, acc_s, stf_s, stb_s):
    seq = q_ref.shape[0]
    c = HG_CHUNK
    n_chunks = seq // c
    n_lvl = len(HG_LEVELS)

    qs_s[...] = jax.nn.silu(q_ref[...].astype(F32))
    v_s[...] = v_ref[...].astype(F32)
    for z_ref, lb_ref, lf_out, k_out in ((zf_ref, lbf_ref, lf_s, kf_s), (zb_ref, lbb_ref, lb_s, kb_s)):
        z = z_ref[...].astype(F32)
        lb = lb_ref[...]
        sig = jax.nn.sigmoid(z)
        lf_out[...] = jnp.log(lb + (1.0 - lb) * sig)
        k_out[...] = (1.0 - lb) * (1.0 - sig)

    ones_mat = jnp.ones((LANES, LANES), BF16)
    lvl = lvl_ref[...]
    row_c = lax.broadcasted_iota(I32, (c, LANES), 0)
    lane8 = lax.broadcasted_iota(I32, (HG_SUB, LANES), 1)
    sub8 = lax.broadcasted_iota(I32, (HG_SUB, LANES), 0)
    n_stack = (2 + n_lvl) * c

    def intra(n, carry):
        r0 = pl.multiple_of(n * c, c)
        rows = pl.ds(r0, c)
        qs = qs_s[rows, :]
        kf = kf_s[rows, :]
        kb = kb_s[rows, :]
        hl = []
        for lfs in (lf_s, lb_s):
            lf = lfs[rows, :]
            hi = lf.astype(BF16)
            hl.append(jnp.concatenate([hi, (lf - hi.astype(F32)).astype(BF16)], axis=0))
        ef = jnp.dot(mf_ref[...], hl[0], preferred_element_type=F32)
        eb = jnp.dot(mb_ref[...], hl[1], preferred_element_type=F32)
        for (e, kk, cs, qi_s, ki_s, d_s, tot_row) in ((ef, kf, cf_s, qif_s, kif_s, df_s, c - 1),
                                                    (eb, kb, cb_s, qib_s, kib_s, db_s, 0)):
            cum = e[0:c]
            cs[rows, :] = cum
            qi_s[rows, :] = (qs * jnp.exp(cum)).astype(BF16)
            ki_s[rows, :] = (kk * jnp.exp(e[c:2 * c])).astype(BF16)
            d_s[pl.ds(n, 1), :] = jnp.exp(cum[tot_row:tot_row + 1, :])

        a_mat = jnp.zeros((c, c), F32)
        for li, m in enumerate(HG_LEVELS):
            second = (row_c & (m // 2)) != 0
            wf = jnp.exp(ef[(2 + li) * c:(3 + li) * c])
            wb = jnp.exp(eb[(2 + li) * c:(3 + li) * c])
            qcat = jnp.concatenate([qs * jnp.where(second, wf, 0.0), qs * jnp.where(second, 0.0, wb)], axis=1)
            kcat = jnp.concatenate([kf * jnp.where(second, 0.0, wf), kb * jnp.where(second, wb, 0.0)], axis=1)
            sc = lax.dot_general(qcat.astype(BF16), kcat.astype(BF16),
                                 (((1,), (1,)), ((), ())), preferred_element_type=F32)
            a_mat = jnp.where(lvl == li, sc, a_mat)

        p_list = []
        for j in range(c // HG_SUB):
            rj = r0 + j * HG_SUB
            sl = pl.ds(rj, HG_SUB)
            cf = cf_s[sl, :]
            cb = cb_s[sl, :]
            qj = qs_s[sl, :]
            for s in range(HG_SUB):
                src = pl.ds(rj + s, 1)
                df = jnp.where(sub8 >= s, cf - cf_s[src, :], -jnp.inf)
                db = jnp.where(sub8 <= s, cb - cb_s[src, :], -jnp.inf)
                p_list.append(qj * (kf_s[src, :] * jnp.exp(df) + kb_s[src, :] * jnp.exp(db)))
        p_all = jnp.concatenate(p_list, axis=0).astype(BF16)
        red = jnp.dot(p_all, ones_mat, preferred_element_type=F32)
        diag_rows = []
        for j in range(c // HG_SUB):
            blk = jnp.zeros((HG_SUB, LANES), F32)
            for s in range(HG_SUB):
                idx = (j * HG_SUB + s) * HG_SUB
                blk = jnp.where(lane8 == j * HG_SUB + s, red[idx:idx + HG_SUB], blk)
            diag_rows.append(blk)
        a_mat = a_mat + jnp.concatenate(diag_rows, axis=0)
        acc_s[rows, :] = jnp.dot(a_mat.astype(BF16), v_s[rows, :].astype(BF16), preferred_element_type=F32)
        return carry

    lax.fori_loop(0, n_chunks, intra, 0, unroll=4)

    stf_s[...] = jnp.zeros_like(stf_s)
    stb_s[...] = jnp.zeros_like(stb_s)

    def inter(n, carry):
        for (st_s, qi_s, ki_s, d_s, idx) in ((stf_s, qif_s, kif_s, df_s, n),
                                             (stb_s, qib_s, kib_s, db_s, n_chunks - 1 - n)):
            rows = pl.ds(pl.multiple_of(idx * c, c), c)
            st = st_s[...]
            acc_s[rows, :] += lax.dot_general(qi_s[rows, :], st.astype(BF16),
                                              (((1,), (1,)), ((), ())), preferred_element_type=F32)
            upd = lax.dot_general(v_s[rows, :].astype(BF16), ki_s[rows, :],
                                  (((0,), (0,)), ((), ())), preferred_element_type=F32)
            st_s[...] = st * d_s[pl.ds(idx, 1), :] + upd
        return carry

    lax.fori_loop(0, n_chunks, inter, 0, unroll=4)

    o = acc_s[...]
    o = o * lax.rsqrt(jnp.mean(o * o, axis=-1, keepdims=True) + RMS_EPS) * nw_ref[...]
    o_ref[...] = (o * jax.nn.silu(g_ref[...].astype(F32))).astype(o_ref.dtype)


def hgrn2_mixer(proj, lb_f, lb_b, norm_w, batch, seq):
    n = proj.shape[0]
    heads = HG_HEADS
    c = HG_CHUNK
    n_chunks = seq // c
    mf, mb, lvl = _hgrn2_constants()
    n_stack = mf.shape[0]

    def col(off):
        return pl.BlockSpec((seq, LANES), lambda b, h, off=off: (b, off + h))

    def per_head(arr):
        return pl.BlockSpec((1, LANES), lambda b, h: (0, h))

    const2 = lambda b, h: (0, 0)
    seq_f32 = pltpu.VMEM((seq, LANES), F32)
    seq_bf16 = pltpu.VMEM((seq, LANES), BF16)
    dec = pltpu.VMEM((n_chunks, LANES), F32)
    state = pltpu.VMEM((LANES, LANES), F32)
    return pl.pallas_call(
        _hgrn2_kernel,
        grid=(batch, heads),
        in_specs=[col(0), col(heads), col(2 * heads), col(3 * heads), col(4 * heads),
                  per_head(lb_f), per_head(lb_b),
                  pl.BlockSpec((1, LANES), const2),
                  pl.BlockSpec((n_stack, 2 * c), const2),
                  pl.BlockSpec((n_stack, 2 * c), const2),
                  pl.BlockSpec((c, c), const2)],
        out_specs=pl.BlockSpec((seq, LANES), lambda b, h: (b, h)),
        out_shape=jax.ShapeDtypeStruct((n, HG_WIDTH), BF16),
        scratch_shapes=[seq_f32] * 8 + [seq_bf16] * 4 + [dec, dec, seq_f32, state, state],
        compiler_params=_cparams(("parallel", "parallel")),
        name="hgrn2_mixer",
    )(proj, proj, proj, proj, proj, lb_f.reshape(1, -1), lb_b.reshape(1, -1), norm_w.reshape(1, -1),
      jnp.asarray(mf, BF16), jnp.asarray(mb, BF16), jnp.asarray(lvl))


def _shift_rows(x, d, fill, reverse):
    pad = jnp.full((d, x.shape[1]), fill, x.dtype)
    if reverse:
        return jnp.concatenate([x[d:], pad], axis=0)
    return jnp.concatenate([pad, x[:-d]], axis=0)


def _linear_scan(a, u, reverse):
    seq = a.shape[0]
    d = 1
    while d < seq:
        u = u + a * _shift_rows(u, d, 0.0, reverse)
        if d * 2 < seq:
            a = a * _shift_rows(a, d, 1.0, reverse)
        d *= 2
    return u


def _rglru_kernel(xr_ref, gr_ref, cw_ref, cb_ref, wa_ref, ba_ref, wx_ref, bx_ref, lam_ref, o_ref, xp_s):
    seq = xr_ref.shape[0]
    pad = SUBLANES
    pad_l = RG_CONV // 2
    xp_s[0:pad, :] = jnp.zeros((pad, LANES), F32)
    xp_s[pad + seq:, :] = jnp.zeros((pad, LANES), F32)
    xp_s[pad:pad + seq, :] = xr_ref[...].astype(F32)
    xc = cb_ref[...]
    for j in range(RG_CONV):
        off = pad + j - pad_l
        xc = xc + xp_s[off:off + seq, :] * cw_ref[j:j + 1, :]
    xcb = xc.astype(BF16)
    h = None
    for d in range(2):
        r = jax.nn.sigmoid(jnp.dot(xcb, wa_ref[d, 0], preferred_element_type=F32) + ba_ref[d:d + 1, :])
        i = jax.nn.sigmoid(jnp.dot(xcb, wx_ref[d, 0], preferred_element_type=F32) + bx_ref[d:d + 1, :])
        log_a = (-RG_C) * r * jax.nn.softplus(-lam_ref[d:d + 1, :])
        a = jnp.exp(log_a)
        u = jnp.sqrt(-jnp.tanh(log_a) * (a * a + 1.0)) * (i * xc)
        hd = _linear_scan(a, u, reverse=(d == 1))
        h = hd if h is None else h + hd
    o_ref[...] = (h * jax.nn.gelu(gr_ref[...].astype(F32))).astype(o_ref.dtype)


def rglru_mixer(proj, conv_w, conv_b, wa, ba, wx, bx, lam, batch, seq):
    n = proj.shape[0]
    x_off = (3 * HG_KEY_WIDTH + 2 * HG_WIDTH) // LANES
    g_off = x_off + RG_WIDTH // LANES
    blk = lambda b, j: (0, j)
    return pl.pallas_call(
        _rglru_kernel,
        grid=(batch, RG_BLOCKS),
        in_specs=[
            pl.BlockSpec((seq, LANES), lambda b, j: (b, x_off + j)),
            pl.BlockSpec((seq, LANES), lambda b, j: (b, g_off + j)),
            pl.BlockSpec((RG_CONV, LANES), blk),
            pl.BlockSpec((1, LANES), blk),
            pl.BlockSpec((2, 1, RG_BLOCK_DIM, RG_BLOCK_DIM), lambda b, j: (0, j, 0, 0)),
            pl.BlockSpec((2, LANES), blk),
            pl.BlockSpec((2, 1, RG_BLOCK_DIM, RG_BLOCK_DIM), lambda b, j: (0, j, 0, 0)),
            pl.BlockSpec((2, LANES), blk),
            pl.BlockSpec((2, LANES), blk),
        ],
        out_specs=pl.BlockSpec((seq, LANES), lambda b, j: (b, j)),
        out_shape=jax.ShapeDtypeStruct((n, RG_WIDTH), BF16),
        scratch_shapes=[pltpu.VMEM((seq + 2 * SUBLANES, LANES), F32)],
        compiler_params=_cparams(("parallel", "parallel")),
        name="rglru_mixer",
    )(proj, proj, conv_w, conv_b.reshape(1, -1), wa.astype(BF16), ba, wx.astype(BF16), bx, lam)


def _outproj_kernel(yh_ref, yr_ref, x_ref, g0_ref, b0_ref, wo_ref, g1_ref, b1_ref, wr_ref,
                    h1_ref, h1p_ref, lg_ref):
    half = yh_ref.shape[1]
    h0 = _layer_norm(x_ref[...], g0_ref[...], b0_ref[...])
    mix = jnp.dot(yh_ref[...], wo_ref[0:half, :], preferred_element_type=F32)
    mix = mix + jnp.dot(yr_ref[...], wo_ref[half:, :], preferred_element_type=F32)
    h1 = _layer_norm(DN_ALPHA * h0 + mix, g1_ref[...], b1_ref[...])
    h1_ref[...] = h1
    _store_row_tiles(h1p_ref, _pack_bf16_pair(h1))
    lg_ref[...] = lax.dot_general(wr_ref[...], h1.astype(BF16), (((1,), (1,)), ((), ())),
                                  preferred_element_type=F32)


def outproj_ln_router(y_hg, y_rg, x2, g0, b0, w_out_bf16, g1, b1, w_router_t_bf16, tm=256):
    n, d = x2.shape
    half = y_hg.shape[1]
    n_e = w_router_t_bf16.shape[0]
    row = lambda i: (i, 0)
    const = lambda i: (0, 0)
    vec = pl.BlockSpec((1, d), const)
    return pl.pallas_call(
        _outproj_kernel,
        grid=(n // tm,),
        in_specs=[pl.BlockSpec((tm, half), row), pl.BlockSpec((tm, half), row), pl.BlockSpec((tm, d), row),
                  vec, vec, pl.BlockSpec((2 * half, d), const), vec, vec,
                  pl.BlockSpec((n_e, d), const)],
        out_specs=[pl.BlockSpec((tm, d), row), pl.BlockSpec((tm * SUBLANES, LANES), row),
                   pl.BlockSpec((n_e, tm), lambda i: (0, i))],
        out_shape=[jax.ShapeDtypeStruct((n, d), F32), jax.ShapeDtypeStruct((n * SUBLANES, LANES), U32),
                   jax.ShapeDtypeStruct((n_e, n), F32)],
        compiler_params=_cparams(("parallel",)),
        name="outproj_ln_router",
    )(y_hg, y_rg, x2, g0.reshape(1, d), b0.reshape(1, d), w_out_bf16, g1.reshape(1, d), b1.reshape(1, d),
      w_router_t_bf16)


def _first_argmax(vals, iota, n):
    m = jnp.max(vals, axis=0, keepdims=True)
    first = jnp.min(jnp.where(vals == m, iota, n), axis=0, keepdims=True)
    return m, first


def _route_kernel(lg_ref, bias_ref, idx_ref, gate_ref):
    n_e, tn = lg_ref.shape
    scores = jax.nn.sigmoid(lg_ref[...])
    choice = scores + bias_ref[...]
    neg = -jnp.inf
    gi = lax.broadcasted_iota(I32, (GROUP_SIZE, tn), 0)
    gs_rows = []
    for g in range(N_GROUPS):
        blk = choice[g * GROUP_SIZE:(g + 1) * GROUP_SIZE]
        m1, f1 = _first_argmax(blk, gi, GROUP_SIZE)
        m2 = jnp.max(jnp.where(gi == f1, neg, blk), axis=0, keepdims=True)
        gs_rows.append(m1 + m2)
    gs = jnp.concatenate(gs_rows, axis=0)
    ri = lax.broadcasted_iota(I32, (N_GROUPS, tn), 0)
    gmask = jnp.zeros((N_GROUPS, tn), jnp.bool_)
    for _ in range(TOPK_GROUPS):
        _, f = _first_argmax(gs, ri, N_GROUPS)
        hit = ri == f
        gmask = gmask | hit
        gs = jnp.where(hit, neg, gs)
    gm = jnp.where(gmask, 1.0, 0.0)
    emask = jnp.concatenate(
        [jnp.broadcast_to(gm[g:g + 1], (GROUP_SIZE, tn)) for g in range(N_GROUPS)], axis=0)
    masked = jnp.where(emask > 0.5, choice, neg)
    ei = lax.broadcasted_iota(I32, (n_e, tn), 0)
    idx_rows, gate_rows = [], []
    for _ in range(TOP_K):
        _, f = _first_argmax(masked, ei, n_e)
        hit = ei == f
        idx_rows.append(f)
        gate_rows.append(jnp.sum(jnp.where(hit, scores, 0.0), axis=0, keepdims=True))
        masked = jnp.where(hit, neg, masked)
    gate = jnp.concatenate(gate_rows, axis=0)
    gate = gate / jnp.sum(gate, axis=0, keepdims=True) * ROUTE_SCALE
    idx_ref[...] = jnp.concatenate(idx_rows, axis=0)
    gate_ref[...] = gate


def route_topk(logits_t, bias, tn=512):
    n_e, n = logits_t.shape
    return pl.pallas_call(
        _route_kernel,
        grid=(n // tn,),
        in_specs=[pl.BlockSpec((n_e, tn), lambda i: (0, i)), pl.BlockSpec((n_e, 1), lambda i: (0, 0))],
        out_specs=[pl.BlockSpec((TOP_K, tn), lambda i: (0, i)), pl.BlockSpec((TOP_K, tn), lambda i: (0, i))],
        out_shape=[jax.ShapeDtypeStruct((TOP_K, n), I32), jax.ShapeDtypeStruct((TOP_K, n), F32)],
        compiler_params=_cparams(("parallel",)),
        name="route_topk",
    )(logits_t, bias.reshape(n_e, 1))


def _rank_kernel(idx_ref, su_ref, rank_ref, cnt_ref, carry_s):
    n_e = cnt_ref.shape[0]
    tn = idx_ref.shape[1]

    @pl.when(pl.program_id(0) == 0)
    def _():
        carry_s[...] = jnp.zeros_like(carry_s)

    idx = idx_ref[...]
    ei = lax.broadcasted_iota(I32, (n_e, tn), 0)
    hits = [ei == idx[k:k + 1] for k in range(TOP_K)]
    member = jnp.zeros((n_e, tn), F32)
    for h in hits:
        member = member + jnp.where(h, 1.0, 0.0)
    prefix = jnp.dot(member.astype(BF16), su_ref[...], preferred_element_type=F32) + carry_s[:, 0:1]
    rank_ref[...] = jnp.concatenate(
        [jnp.sum(jnp.where(h, prefix, 0.0), axis=0, keepdims=True) for h in hits], axis=0).astype(I32)
    carry_s[...] = carry_s[...] + jnp.sum(member, axis=1, keepdims=True)
    cnt_ref[...] = carry_s[...].astype(I32)


def expert_ranks(idx_t, tn=512):
    k, n = idx_t.shape
    su = jnp.asarray(np.triu(np.ones((tn, tn), np.float32), 1), BF16)
    return pl.pallas_call(
        _rank_kernel,
        grid=(n // tn,),
        in_specs=[pl.BlockSpec((k, tn), lambda i: (0, i)), pl.BlockSpec((tn, tn), lambda i: (0, 0))],
        out_specs=[pl.BlockSpec((k, tn), lambda i: (0, i)), pl.BlockSpec((N_EXPERTS, LANES), lambda i: (0, 0))],
        out_shape=[jax.ShapeDtypeStruct((k, n), I32), jax.ShapeDtypeStruct((N_EXPERTS, LANES), I32)],
        scratch_shapes=[pltpu.VMEM((N_EXPERTS, LANES), F32)],
        compiler_params=_cparams(("arbitrary",)),
        name="expert_ranks",
    )(idx_t, su)


def _dest_kernel(idx_ref, rank_ref, start_ref, dest_ref):
    n_e = start_ref.shape[0]
    tn = idx_ref.shape[1]
    idx = idx_ref[...]
    ei = lax.broadcasted_iota(I32, (n_e, tn), 0)
    start = start_ref[...]
    base = jnp.concatenate(
        [jnp.sum(jnp.where(ei == idx[k:k + 1], start, 0.0), axis=0, keepdims=True) for k in range(TOP_K)],
        axis=0)
    dest_ref[...] = base.astype(I32) + rank_ref[...]


def assignment_rows(idx_t, rank_t, padded_start, tn=512):
    k, n = idx_t.shape
    blk = pl.BlockSpec((k, tn), lambda i: (0, i))
    return pl.pallas_call(
        _dest_kernel,
        grid=(n // tn,),
        in_specs=[blk, blk, pl.BlockSpec((N_EXPERTS, 1), lambda i: (0, 0))],
        out_specs=blk,
        out_shape=jax.ShapeDtypeStruct((k, n), I32),
        compiler_params=_cparams(("parallel",)),
        name="assignment_rows",
    )(idx_t, rank_t, padded_start.astype(F32).reshape(N_EXPERTS, 1))


WEIGHT_DMA_PRIORITY = 1
ROW_GROUP = 4


def _rowmap_kernel(dest_ref, tok_ref):
    n_tok = dest_ref.shape[0] // TOP_K

    def body(j, c):
        for k in range(TOP_K):
            tok_ref[dest_ref[k * n_tok + j]] = j
        return c

    lax.fori_loop(0, n_tok, body, 0, unroll=4)


def row_tokens(dest_flat, n_rows):
    return pl.pallas_call(
        _rowmap_kernel,
        grid_spec=pltpu.PrefetchScalarGridSpec(
            num_scalar_prefetch=1,
            grid=(1,),
            in_specs=[],
            out_specs=pl.BlockSpec(memory_space=pltpu.SMEM),
        ),
        out_shape=jax.ShapeDtypeStruct((n_rows,), I32),
        compiler_params=_cparams(("arbitrary",)),
        name="row_tokens",
    )(dest_flat)


def _experts_kernel(be_ref, nxt_ref, nxt2_ref, slot_ref, cnt_ref, tok_ref, nb_ref,
                    h1p_ref, wg_ref, wu_ref, wd_ref, ys_ref,
                    xbuf, wgb, wub, wdb, wg_s, wu_s, wd_s, gsem, wsem):
    tm = ys_ref.shape[0] // SUBLANES
    i = pl.program_id(0)
    nb = nb_ref[0]

    def weight_copies(e, slot):
        return (pltpu.make_async_copy(wg_ref.at[e], wgb.at[slot], wsem.at[slot]),
                pltpu.make_async_copy(wu_ref.at[e], wub.at[slot], wsem.at[slot]),
                pltpu.make_async_copy(wd_ref.at[e], wdb.at[slot], wsem.at[slot]))

    def start_group(grp, slot):
        for g in range(ROW_GROUP):
            blk = jnp.minimum(grp * ROW_GROUP + g, nb - 1)
            last = cnt_ref[blk] - 1

            def issue(r, c, blk=blk, last=last, g=g):
                t = tok_ref[blk * tm + jnp.minimum(r, last)]
                pltpu.make_async_copy(h1p_ref.at[pl.ds(t * SUBLANES, SUBLANES)],
                                      xbuf.at[slot, pl.ds((g * tm + r) * SUBLANES, SUBLANES)],
                                      gsem.at[slot]).start()
                return c

            lax.fori_loop(0, tm, issue, 0, unroll=8)

    @pl.when(i < nb)
    def _():
        e = be_ref[i]
        wslot = slot_ref[i]
        grp = lax.div(i, ROW_GROUP)
        sub = lax.rem(i, ROW_GROUP)
        gslot = lax.rem(grp, 2)

        @pl.when(i == 0)
        def _():
            for cp in weight_copies(e, wslot):
                cp.start(priority=WEIGHT_DMA_PRIORITY)
            nxt = nxt_ref[0]

            @pl.when(nxt >= 0)
            def _():
                for cp in weight_copies(nxt, 1 - wslot):
                    cp.start(priority=WEIGHT_DMA_PRIORITY)

            start_group(0, 0)

        @pl.when((i == 0) | (e != be_ref[jnp.maximum(i - 1, 0)]))
        def _():
            for cp in weight_copies(e, wslot):
                cp.wait()
            wg_s[...] = wgb[wslot].astype(BF16)
            wu_s[...] = wub[wslot].astype(BF16)
            wd_s[...] = wdb[wslot].astype(BF16)
            nxt2 = nxt2_ref[i]

            @pl.when(nxt2 >= 0)
            def _():
                for cp in weight_copies(nxt2, wslot):
                    cp.start(priority=WEIGHT_DMA_PRIORITY)

        @pl.when(sub == 0)
        def _():
            pltpu.make_async_copy(h1p_ref.at[pl.ds(0, ROW_GROUP * tm * SUBLANES)], xbuf.at[gslot],
                                  gsem.at[gslot]).wait()

            @pl.when((grp + 1) * ROW_GROUP < nb)
            def _():
                start_group(grp + 1, 1 - gslot)

        row0 = pl.multiple_of(sub * (tm * SUBLANES), tm * SUBLANES)
        x = _unpack_bf16_pair(_load_row_tiles(xbuf.at[gslot, pl.ds(row0, tm * SUBLANES)], tm))
        a = jnp.dot(x, wg_s[...], preferred_element_type=F32)
        u = jnp.dot(x, wu_s[...], preferred_element_type=F32)
        hid = (jax.nn.silu(a) * u).astype(BF16)
        y = jnp.dot(hid, wd_s[...], preferred_element_type=F32)
        _store_row_tiles(ys_ref, _pack_bf16_pair(y))


def grouped_experts(h1p_tiles, row_tok, block_expert, block_next, block_next2, block_slot, block_count,
                    n_blocks_used, w_gate, w_up, w_down, tm=MOE_BLOCK):
    n_rows = row_tok.shape[0]
    n_e, d, ff = w_gate.shape
    n_blocks = n_rows // tm
    rowmap = lambda i, be, nx, nx2, sl, cn, tok, nb: (jnp.minimum(i, jnp.maximum(nb[0] - 1, 0)), 0)
    hbm = pl.BlockSpec(memory_space=pl.ANY)
    return pl.pallas_call(
        _experts_kernel,
        grid_spec=pltpu.PrefetchScalarGridSpec(
            num_scalar_prefetch=7,
            grid=(n_blocks,),
            in_specs=[hbm, hbm, hbm, hbm],
            out_specs=pl.BlockSpec((tm * SUBLANES, LANES), rowmap),
            scratch_shapes=[pltpu.VMEM((2, ROW_GROUP * tm * SUBLANES, LANES), U32),
                            pltpu.VMEM((2, d, ff), F32), pltpu.VMEM((2, d, ff), F32), pltpu.VMEM((2, ff, d), F32),
                            pltpu.VMEM((d, ff), BF16), pltpu.VMEM((d, ff), BF16), pltpu.VMEM((ff, d), BF16),
                            pltpu.SemaphoreType.DMA((2,)), pltpu.SemaphoreType.DMA((2,))],
        ),
        out_shape=jax.ShapeDtypeStruct((n_rows * SUBLANES, LANES), U32),
        compiler_params=_cparams(("arbitrary",)),
        name="grouped_experts",
    )(block_expert, block_next, block_next2, block_slot, block_count, row_tok, n_blocks_used,
      h1p_tiles, w_gate, w_up, w_down)


def _combine_kernel(dest_ref, h1_ref, h1p_ref, gate_ref, wsg_ref, wsu_ref, wsd_ref, g2_ref, b2_ref, ys_ref,
                    o_ref, buf_s, sem):
    tm = h1_ref.shape[0]
    i = pl.program_id(0)
    n_steps = pl.num_programs(0)
    n_tok = n_steps * tm
    slot = lax.rem(i, 2)

    def start_rows(step, sl):
        def issue(r, c):
            for k in range(TOP_K):
                row = dest_ref[k * n_tok + step * tm + r]
                pltpu.make_async_copy(ys_ref.at[pl.ds(row * SUBLANES, SUBLANES)],
                                      buf_s.at[sl, k, pl.ds(r * SUBLANES, SUBLANES)], sem.at[sl]).start()
            return c

        lax.fori_loop(0, tm, issue, 0, unroll=2)

    @pl.when(i == 0)
    def _():
        start_rows(0, 0)

    @pl.when(i + 1 < n_steps)
    def _():
        start_rows(i + 1, 1 - slot)

    xb = _unpack_bf16_pair(_load_row_tiles(h1p_ref, tm))
    a = jnp.dot(xb, wsg_ref[...], preferred_element_type=F32)
    u = jnp.dot(xb, wsu_ref[...], preferred_element_type=F32)
    shared = jnp.dot((jax.nn.silu(a) * u).astype(BF16), wsd_ref[...], preferred_element_type=F32)

    for k in range(TOP_K):
        pltpu.make_async_copy(ys_ref.at[pl.ds(0, tm * SUBLANES)], buf_s.at[slot, k], sem.at[slot]).wait()

    gate = gate_ref[...]
    routed = jnp.zeros_like(shared)
    for k in range(TOP_K):
        routed = routed + _unpack_pair_f32(_load_row_tiles(buf_s.at[slot, k], tm)) * gate[:, k:k + 1]
    o_ref[...] = _layer_norm(DN_ALPHA * h1_ref[...] + routed + shared, g2_ref[...], b2_ref[...])


def combine_shared_ln(dest_flat, h1, h1p_tiles, gate_nk, ys_tiles, wsg, wsu, wsd, g2, b2, tm=128):
    n, d = h1.shape
    ff = wsg.shape[1]
    row = lambda i, dest: (i, 0)
    const = lambda i, dest: (0, 0)
    return pl.pallas_call(
        _combine_kernel,
        grid_spec=pltpu.PrefetchScalarGridSpec(
            num_scalar_prefetch=1,
            grid=(n // tm,),
            in_specs=[pl.BlockSpec((tm, d), row), pl.BlockSpec((tm * SUBLANES, LANES), row),
                      pl.BlockSpec((tm, TOP_K), row),
                      pl.BlockSpec((d, ff), const), pl.BlockSpec((d, ff), const), pl.BlockSpec((ff, d), const),
                      pl.BlockSpec((1, d), const), pl.BlockSpec((1, d), const),
                      pl.BlockSpec(memory_space=pl.ANY)],
            out_specs=pl.BlockSpec((tm, d), row),
            scratch_shapes=[pltpu.VMEM((2, TOP_K, tm * SUBLANES, LANES), U32), pltpu.SemaphoreType.DMA((2,))],
        ),
        out_shape=jax.ShapeDtypeStruct((n, d), F32),
        compiler_params=_cparams(("arbitrary",)),
        name="combine_shared_ln",
    )(dest_flat, h1, h1p_tiles, gate_nk, wsg, wsu, wsd, g2.reshape(1, d), b2.reshape(1, d), ys_tiles)


def kernel(x, ln_emb_g, ln_emb_b, w_in, lb_logits, hg_norm_w, conv_w, conv_b, rg_wa, rg_ba, rg_wx, rg_bx,
           rg_lam, w_out, ln1_g, ln1_b, w_router, router_bias, w_gate, w_up, w_down, ws_gate, ws_up, ws_down,
           ln2_g, ln2_b):
    batch, seq, d = x.shape
    assert d == 2 * SUBLANES * LANES, "packed rows must fill one 8x128 tile"
    n_tok = batch * seq
    layer = 0
    x2 = x.reshape(n_tok, d)
    lb = jnp.cumsum(jax.nn.softmax(lb_logits.astype(F32), axis=1), axis=1)

    proj = ln_inproj(x2, ln_emb_g, ln_emb_b, w_in[layer].astype(BF16))
    y_hg = hgrn2_mixer(proj, lb[0, layer], lb[1, layer], hg_norm_w[layer], batch, seq)
    y_rg = rglru_mixer(proj, conv_w[layer], conv_b[layer], rg_wa[layer], rg_ba[layer], rg_wx[layer],
                       rg_bx[layer], rg_lam[layer], batch, seq)
    h1, h1p_tiles, logits_t = outproj_ln_router(
        y_hg, y_rg, x2, ln_emb_g, ln_emb_b, w_out[layer].astype(BF16), ln1_g[layer], ln1_b[layer],
        w_router[layer].T.astype(BF16))

    idx_t, gate_t = route_topk(logits_t, router_bias[layer])
    rank_t, counts = expert_ranks(idx_t)
    counts = counts[:, 0]

    n_assign = n_tok * TOP_K
    n_blocks = -(-(n_assign + N_EXPERTS * (MOE_BLOCK - 1)) // MOE_BLOCK)
    n_rows = n_blocks * MOE_BLOCK
    padded = (counts + MOE_BLOCK - 1) // MOE_BLOCK * MOE_BLOCK
    padded_end = jnp.cumsum(padded)
    padded_start = padded_end - padded
    n_used_blocks = (padded_end[-1:] // MOE_BLOCK).astype(I32)
    blk_row = jnp.minimum(jnp.arange(n_blocks, dtype=I32), n_used_blocks[0] - 1) * MOE_BLOCK
    block_expert = jnp.minimum(
        jnp.sum((padded_end[None, :] <= blk_row[:, None]).astype(I32), axis=1), N_EXPERTS - 1)
    e_ids = jnp.arange(N_EXPERTS, dtype=I32)
    active = counts > 0
    later = lax.cummin(jnp.where(active, e_ids, N_EXPERTS)[::-1])[::-1]
    nxt = jnp.concatenate([later[1:], jnp.full((1,), N_EXPERTS, I32)])
    next1 = jnp.where(nxt < N_EXPERTS, nxt, -1).astype(I32)
    hot1 = (next1[:, None] == e_ids[None, :]).astype(I32)
    next2 = jnp.sum(hot1 * (next1 + 1)[None, :], axis=1) - 1
    ordinal = jnp.cumsum(active.astype(I32)) - 1
    onehot = (block_expert[:, None] == e_ids[None, :]).astype(I32)
    block_next = jnp.sum(onehot * next1[None, :], axis=1)
    block_next2 = jnp.sum(onehot * next2[None, :], axis=1)
    block_slot = jnp.sum(onehot * (ordinal % 2)[None, :], axis=1)
    block_count = jnp.clip(jnp.sum(onehot * (counts + padded_start)[None, :], axis=1) - blk_row, 1, MOE_BLOCK)

    dest_flat = assignment_rows(idx_t, rank_t, padded_start).reshape(-1)
    row_tok = row_tokens(dest_flat, n_rows)

    ys_tiles = grouped_experts(h1p_tiles, row_tok, block_expert, block_next, block_next2, block_slot,
                               block_count, n_used_blocks, w_gate[layer], w_up[layer], w_down[layer])
    out = combine_shared_ln(dest_flat, h1, h1p_tiles, gate_t.T, ys_tiles,
                            ws_gate[layer].astype(BF16), ws_up[layer].astype(BF16),
                            ws_down[layer].astype(BF16), ln2_g[layer], ln2_b[layer])
    return out.reshape(batch, seq, d)
```

```python
import functools

import numpy as np
import jax
import jax.numpy as jnp
from jax import lax
from jax.experimental import pallas as pl
from jax.experimental.pallas import tpu as pltpu

F32 = jnp.float32
BF16 = jnp.bfloat16
U32 = jnp.uint32
I32 = jnp.int32

D_MODEL = 2048
DEPTH = 1
HG_KEY_DIM = 128
HG_VAL_DIM = 128
HG_WIDTH = D_MODEL // 2
HG_HEADS = HG_WIDTH // HG_VAL_DIM
HG_KEY_WIDTH = HG_HEADS * HG_KEY_DIM
RG_WIDTH = D_MODEL - HG_WIDTH
RG_BLOCKS = 8
RG_BLOCK_DIM = RG_WIDTH // RG_BLOCKS
RG_CONV = 4
RG_C = 8.0
MIX_WIDTH = HG_WIDTH + RG_WIDTH
IN_WIDTHS = (HG_KEY_WIDTH, HG_KEY_WIDTH, HG_KEY_WIDTH, HG_WIDTH, HG_WIDTH, RG_WIDTH, RG_WIDTH)
IN_WIDTH = sum(IN_WIDTHS)
N_EXPERTS = 256
TOP_K = 8
N_GROUPS = 8
TOPK_GROUPS = 4
GROUP_SIZE = N_EXPERTS // N_GROUPS
EXPERT_FF = 512
SHARED_FF = 512
ROUTE_SCALE = 2.5
MOE_BLOCK = 128
DN_ALPHA = (2 * DEPTH) ** 0.25
LN_EPS = 1e-5
RMS_EPS = 1e-6

LANES = 128
SUBLANES = 8
VMEM_LIMIT_BYTES = 56 * 1024 * 1024

HG_CHUNK = 128
HG_SUB = 8
HG_LEVELS = (16, 32, 64, 128)


def _cparams(sem, vmem=VMEM_LIMIT_BYTES):
    return pltpu.CompilerParams(dimension_semantics=sem, vmem_limit_bytes=vmem)


def _layer_norm(x, g, b):
    mu = jnp.mean(x, axis=-1, keepdims=True)
    xc = x - mu
    var = jnp.mean(xc * xc, axis=-1, keepdims=True)
    return xc * lax.rsqrt(var + LN_EPS) * g + b


def _pack_bf16_pair(x):
    n = x.shape[1] // 2
    bits = lax.bitcast_convert_type(x.astype(BF16).astype(F32), U32)
    return (bits[:, :n] >> 16) | (bits[:, n:] & jnp.uint32(0xFFFF0000))


def _unpack_pair_f32(w):
    lo = lax.bitcast_convert_type(w << 16, F32)
    hi = lax.bitcast_convert_type(w & jnp.uint32(0xFFFF0000), F32)
    return jnp.concatenate([lo, hi], axis=1)


def _unpack_bf16_pair(w):
    return _unpack_pair_f32(w).astype(BF16)


def _store_row_tiles(ref, packed):
    m = packed.shape[0]
    for s in range(SUBLANES):
        ref[pl.ds(s, m, stride=SUBLANES), :] = packed[:, s * LANES:(s + 1) * LANES]


def _load_row_tiles(ref, m):
    return jnp.concatenate([ref[pl.ds(s, m, stride=SUBLANES), :] for s in range(SUBLANES)], axis=1)


def _ln_inproj_kernel(x_ref, g_ref, b_ref, w_ref, o_ref, hb_ref):
    @pl.when(pl.program_id(1) == 0)
    def _():
        hb_ref[...] = _layer_norm(x_ref[...], g_ref[...], b_ref[...]).astype(BF16)

    o_ref[...] = jnp.dot(hb_ref[...], w_ref[...], preferred_element_type=F32).astype(o_ref.dtype)


def ln_inproj(x2, g, b, w_bf16, tm=1024, tn=1024):
    n, d = x2.shape
    width = w_bf16.shape[1]
    return pl.pallas_call(
        _ln_inproj_kernel,
        grid=(n // tm, width // tn),
        in_specs=[
            pl.BlockSpec((tm, d), lambda i, j: (i, 0)),
            pl.BlockSpec((1, d), lambda i, j: (0, 0)),
            pl.BlockSpec((1, d), lambda i, j: (0, 0)),
            pl.BlockSpec((d, tn), lambda i, j: (0, j)),
        ],
        out_specs=pl.BlockSpec((tm, tn), lambda i, j: (i, j)),
        out_shape=jax.ShapeDtypeStruct((n, width), BF16),
        scratch_shapes=[pltpu.VMEM((tm, d), BF16)],
        compiler_params=_cparams(("parallel", "arbitrary")),
        name="ln_inproj",
    )(x2, g.reshape(1, d), b.reshape(1, d), w_bf16)


def _hgrn2_constants():
    c = HG_CHUNK
    t = np.arange(c)[:, None]
    u = np.arange(c)[None, :]
    fwd = [(u <= t), (u > t)]
    bwd = [(u >= t), (u < t)]
    for m in HG_LEVELS:
        start = (t // m) * m
        mid = start + m // 2 - 1
        second = t > mid
        fwd.append(np.where(second, (u > mid) & (u <= t), (u > t) & (u <= mid)))
        bwd.append(np.where(second, (u > mid) & (u < t), (u >= t) & (u <= mid)))
    fwd = np.concatenate(fwd, axis=0).astype(np.float32)
    bwd = np.concatenate(bwd, axis=0).astype(np.float32)
    fwd = np.concatenate([fwd, fwd], axis=1)
    bwd = np.concatenate([bwd, bwd], axis=1)
    lvl = np.full((c, c), -1, np.int32)
    tt = np.arange(c)[:, None]
    ss = np.arange(c)[None, :]
    for li in reversed(range(len(HG_LEVELS))):
        m = HG_LEVELS[li]
        lvl = np.where((tt // m) == (ss // m), li, lvl)
    lvl = np.where((tt // HG_SUB) == (ss // HG_SUB), -1, lvl).astype(np.int32)
    return fwd, bwd, lvl


def _hgrn2_kernel(q_ref, zf_ref, zb_ref, v_ref, g_ref, lbf_ref, lbb_ref, nw_ref,
                  mf_ref, mb_ref, lvl_ref, o_ref,
                  qs_s, v_s, lf_s, kf_s, lb_s, kb_s, cf_s, cb_s,
                  qif_s, kif_s, qib_s, kib_s, df_s, db_s, acc_s, stf_s, stb_s):
    seq = q_ref.shape[0]
    c = HG_CHUNK
    n_chunks = seq // c
    n_lvl = len(HG_LEVELS)

    qs_s[...] = jax.nn.silu(q_ref[...].astype(F32))
    v_s[...] = v_ref[...].astype(F32)
    for z_ref, lb_ref, lf_out, k_out in ((zf_ref, lbf_ref, lf_s, kf_s), (zb_ref, lbb_ref, lb_s, kb_s)):
        z = z_ref[...].astype(F32)
        lb = lb_ref[...]
        sig = jax.nn.sigmoid(z)
        lf_out[...] = jnp.log(lb + (1.0 - lb) * sig)
        k_out[...] = (1.0 - lb) * (1.0 - sig)

    ones_mat = jnp.ones((LANES, LANES), BF16)
    lvl = lvl_ref[...]
    row_c = lax.broadcasted_iota(I32, (c, LANES), 0)
    lane8 = lax.broadcasted_iota(I32, (HG_SUB, LANES), 1)
    sub8 = lax.broadcasted_iota(I32, (HG_SUB, LANES), 0)
    n_stack = (2 + n_lvl) * c

    def intra(n, carry):
        r0 = pl.multiple_of(n * c, c)
        rows = pl.ds(r0, c)
        qs = qs_s[rows, :]
        kf = kf_s[rows, :]
        kb = kb_s[rows, :]
        hl = []
        for lfs in (lf_s, lb_s):
            lf = lfs[rows, :]
            hi = lf.astype(BF16)
            hl.append(jnp.concatenate([hi, (lf - hi.astype(F32)).astype(BF16)], axis=0))
        ef = jnp.dot(mf_ref[...], hl[0], preferred_element_type=F32)
        eb = jnp.dot(mb_ref[...], hl[1], preferred_element_type=F32)
        for (e, kk, cs, qi_s, ki_s, d_s, tot_row) in ((ef, kf, cf_s, qif_s, kif_s, df_s, c - 1),
                                                    (eb, kb, cb_s, qib_s, kib_s, db_s, 0)):
            cum = e[0:c]
            cs[rows, :] = cum
            qi_s[rows, :] = (qs * jnp.exp(cum)).astype(BF16)
            ki_s[rows, :] = (kk * jnp.exp(e[c:2 * c])).astype(BF16)
            d_s[pl.ds(n, 1), :] = jnp.exp(cum[tot_row:tot_row + 1, :])

        a_mat = jnp.zeros((c, c), F32)
        for li, m in enumerate(HG_LEVELS):
            second = (row_c & (m // 2)) != 0
            wf = jnp.exp(ef[(2 + li) * c:(3 + li) * c])
            wb = jnp.exp(eb[(2 + li) * c:(3 + li) * c])
            qcat = jnp.concatenate([qs * jnp.where(second, wf, 0.0), qs * jnp.where(second, 0.0, wb)], axis=1)
            kcat = jnp.concatenate([kf * jnp.where(second, 0.0, wf), kb * jnp.where(second, wb, 0.0)], axis=1)
            sc = lax.dot_general(qcat.astype(BF16), kcat.astype(BF16),
                                 (((1,), (1,)), ((), ())), preferred_element_type=F32)
            a_mat = jnp.where(lvl == li, sc, a_mat)

        p_list = []
        for j in range(c // HG_SUB):
            rj = r0 + j * HG_SUB
            sl = pl.ds(rj, HG_SUB)
            cf = cf_s[sl, :]
            cb = cb_s[sl, :]
            qj = qs_s[sl, :]
            for s in range(HG_SUB):
                src = pl.ds(rj + s, 1)
                df = jnp.where(sub8 >= s, cf - cf_s[src, :], -jnp.inf)
                db = jnp.where(sub8 <= s, cb - cb_s[src, :], -jnp.inf)
                p_list.append(qj * (kf_s[src, :] * jnp.exp(df) + kb_s[src, :] * jnp.exp(db)))
        p_all = jnp.concatenate(p_list, axis=0).astype(BF16)
        red = jnp.dot(p_all, ones_mat, preferred_element_type=F32)
        diag_rows = []
        for j in range(c // HG_SUB):
            blk = jnp.zeros((HG_SUB, LANES), F32)
            for s in range(HG_SUB):
                idx = (j * HG_SUB + s) * HG_SUB
                blk = jnp.where(lane8 == j * HG_SUB + s, red[idx:idx + HG_SUB], blk)
            diag_rows.append(blk)
        a_mat = a_mat + jnp.concatenate(diag_rows, axis=0)
        acc_s[rows, :] = jnp.dot(a_mat.astype(BF16), v_s[rows, :].astype(BF16), preferred_element_type=F32)
        return carry

    lax.fori_loop(0, n_chunks, intra, 0, unroll=4)

    stf_s[...] = jnp.zeros_like(stf_s)
    stb_s[...] = jnp.zeros_like(stb_s)

    def inter(n, carry):
        for (st_s, qi_s, ki_s, d_s, idx) in ((stf_s, qif_s, kif_s, df_s, n),
                                             (stb_s, qib_s, kib_s, db_s, n_chunks - 1 - n)):
            rows = pl.ds(pl.multiple_of(idx * c, c), c)
            st = st_s[...]
            acc_s[rows, :] += lax.dot_general(qi_s[rows, :], st.astype(BF16),
                                              (((1,), (1,)), ((), ())), preferred_element_type=F32)
            upd = lax.dot_general(v_s[rows, :].astype(BF16), ki_s[rows, :],
                                  (((0,), (0,)), ((), ())), preferred_element_type=F32)
            st_s[...] = st * d_s[pl.ds(idx, 1), :] + upd
        return carry

    lax.fori_loop(0, n_chunks, inter, 0, unroll=4)

    o = acc_s[...]
    o = o * lax.rsqrt(jnp.mean(o * o, axis=-1, keepdims=True) + RMS_EPS) * nw_ref[...]
    o_ref[...] = (o * jax.nn.silu(g_ref[...].astype(F32))).astype(o_ref.dtype)


def hgrn2_mixer(proj, lb_f, lb_b, norm_w, batch, seq):
    n = proj.shape[0]
    heads = HG_HEADS
    c = HG_CHUNK
    n_chunks = seq // c
    mf, mb, lvl = _hgrn2_constants()
    n_stack = mf.shape[0]

    def col(off):
        return pl.BlockSpec((seq, LANES), lambda b, h, off=off: (b, off + h))

    def per_head(arr):
        return pl.BlockSpec((1, LANES), lambda b, h: (0, h))

    const2 = lambda b, h: (0, 0)
    seq_f32 = pltpu.VMEM((seq, LANES), F32)
    seq_bf16 = pltpu.VMEM((seq, LANES), BF16)
    dec = pltpu.VMEM((n_chunks, LANES), F32)
    state = pltpu.VMEM((LANES, LANES), F32)
    return pl.pallas_call(
        _hgrn2_kernel,
        grid=(batch, heads),
        in_specs=[col(0), col(heads), col(2 * heads), col(3 * heads), col(4 * heads),
                  per_head(lb_f), per_head(lb_b),
                  pl.BlockSpec((1, LANES), const2),
                  pl.BlockSpec((n_stack, 2 * c), const2),
                  pl.BlockSpec((n_stack, 2 * c), const2),
                  pl.BlockSpec((c, c), const2)],
        out_specs=pl.BlockSpec((seq, LANES), lambda b, h: (b, h)),
        out_shape=jax.ShapeDtypeStruct((n, HG_WIDTH), BF16),
        scratch_shapes=[seq_f32] * 8 + [seq_bf16] * 4 + [dec, dec, seq_f32, state, state],
        compiler_params=_cparams(("parallel", "parallel")),
        name="hgrn2_mixer",
    )(proj, proj, proj, proj, proj, lb_f.reshape(1, -1), lb_b.reshape(1, -1), norm_w.reshape(1, -1),
      jnp.asarray(mf, BF16), jnp.asarray(mb, BF16), jnp.asarray(lvl))


def _shift_rows(x, d, fill, reverse):
    pad = jnp.full((d, x.shape[1]), fill, x.dtype)
    if reverse:
        return jnp.concatenate([x[d:], pad], axis=0)
    return jnp.concatenate([pad, x[:-d]], axis=0)


def _linear_scan(a, u, reverse):
    seq = a.shape[0]
    d = 1
    while d < seq:
        u = u + a * _shift_rows(u, d, 0.0, reverse)
        if d * 2 < seq:
            a = a * _shift_rows(a, d, 1.0, reverse)
        d *= 2
    return u


def _rglru_kernel(xr_ref, gr_ref, cw_ref, cb_ref, wa_ref, ba_ref, wx_ref, bx_ref, lam_ref, o_ref, xp_s):
    seq = xr_ref.shape[0]
    pad = SUBLANES
    pad_l = RG_CONV // 2
    xp_s[0:pad, :] = jnp.zeros((pad, LANES), F32)
    xp_s[pad + seq:, :] = jnp.zeros((pad, LANES), F32)
    xp_s[pad:pad + seq, :] = xr_ref[...].astype(F32)
    xc = cb_ref[...]
    for j in range(RG_CONV):
        off = pad + j - pad_l
        xc = xc + xp_s[off:off + seq, :] * cw_ref[j:j + 1, :]
    xcb = xc.astype(BF16)
    h = None
    for d in range(2):
        r = jax.nn.sigmoid(jnp.dot(xcb, wa_ref[d, 0], preferred_element_type=F32) + ba_ref[d:d + 1, :])
        i = jax.nn.sigmoid(jnp.dot(xcb, wx_ref[d, 0], preferred_element_type=F32) + bx_ref[d:d + 1, :])
        log_a = (-RG_C) * r * jax.nn.softplus(-lam_ref[d:d + 1, :])
        a = jnp.exp(log_a)
        u = jnp.sqrt(-jnp.tanh(log_a) * (a * a + 1.0)) * (i * xc)
        hd = _linear_scan(a, u, reverse=(d == 1))
        h = hd if h is None else h + hd
    o_ref[...] = (h * jax.nn.gelu(gr_ref[...].astype(F32))).astype(o_ref.dtype)


def rglru_mixer(proj, conv_w, conv_b, wa, ba, wx, bx, lam, batch, seq):
    n = proj.shape[0]
    x_off = (3 * HG_KEY_WIDTH + 2 * HG_WIDTH) // LANES
    g_off = x_off + RG_WIDTH // LANES
    blk = lambda b, j: (0, j)
    return pl.pallas_call(
        _rglru_kernel,
        grid=(batch, RG_BLOCKS),
        in_specs=[
            pl.BlockSpec((seq, LANES), lambda b, j: (b, x_off + j)),
            pl.BlockSpec((seq, LANES), lambda b, j: (b, g_off + j)),
            pl.BlockSpec((RG_CONV, LANES), blk),
            pl.BlockSpec((1, LANES), blk),
            pl.BlockSpec((2, 1, RG_BLOCK_DIM, RG_BLOCK_DIM), lambda b, j: (0, j, 0, 0)),
            pl.BlockSpec((2, LANES), blk),
            pl.BlockSpec((2, 1, RG_BLOCK_DIM, RG_BLOCK_DIM), lambda b, j: (0, j, 0, 0)),
            pl.BlockSpec((2, LANES), blk),
            pl.BlockSpec((2, LANES), blk),
        ],
        out_specs=pl.BlockSpec((seq, LANES), lambda b, j: (b, j)),
        out_shape=jax.ShapeDtypeStruct((n, RG_WIDTH), BF16),
        scratch_shapes=[pltpu.VMEM((seq + 2 * SUBLANES, LANES), F32)],
        compiler_params=_cparams(("parallel", "parallel")),
        name="rglru_mixer",
    )(proj, proj, conv_w, conv_b.reshape(1, -1), wa.astype(BF16), ba, wx.astype(BF16), bx, lam)


def _outproj_kernel(yh_ref, yr_ref, x_ref, g0_ref, b0_ref, wo_ref, g1_ref, b1_ref, wr_ref,
                    h1_ref, h1p_ref, lg_ref):
    half = yh_ref.shape[1]
    h0 = _layer_norm(x_ref[...], g0_ref[...], b0_ref[...])
    mix = jnp.dot(yh_ref[...], wo_ref[0:half, :], preferred_element_type=F32)
    mix = mix + jnp.dot(yr_ref[...], wo_ref[half:, :], preferred_element_type=F32)
    h1 = _layer_norm(DN_ALPHA * h0 + mix, g1_ref[...], b1_ref[...])
    h1_ref[...] = h1
    _store_row_tiles(h1p_ref, _pack_bf16_pair(h1))
    lg_ref[...] = lax.dot_general(wr_ref[...], h1.astype(BF16), (((1,), (1,)), ((), ())),
                                  preferred_element_type=F32)


def outproj_ln_router(y_hg, y_rg, x2, g0, b0, w_out_bf16, g1, b1, w_router_t_bf16, tm=256):
    n, d = x2.shape
    half = y_hg.shape[1]
    n_e = w_router_t_bf16.shape[0]
    row = lambda i: (i, 0)
    const = lambda i: (0, 0)
    vec = pl.BlockSpec((1, d), const)
    return pl.pallas_call(
        _outproj_kernel,
        grid=(n // tm,),
        in_specs=[pl.BlockSpec((tm, half), row), pl.BlockSpec((tm, half), row), pl.BlockSpec((tm, d), row),
                  vec, vec, pl.BlockSpec((2 * half, d), const), vec, vec,
                  pl.BlockSpec((n_e, d), const)],
        out_specs=[pl.BlockSpec((tm, d), row), pl.BlockSpec((tm * SUBLANES, LANES), row),
                   pl.BlockSpec((n_e, tm), lambda i: (0, i))],
        out_shape=[jax.ShapeDtypeStruct((n, d), F32), jax.ShapeDtypeStruct((n * SUBLANES, LANES), U32),
                   jax.ShapeDtypeStruct((n_e, n), F32)],
        compiler_params=_cparams(("parallel",)),
        name="outproj_ln_router",
    )(y_hg, y_rg, x2, g0.reshape(1, d), b0.reshape(1, d), w_out_bf16, g1.reshape(1, d), b1.reshape(1, d),
      w_router_t_bf16)


def _first_argmax(vals, iota, n):
    m = jnp.max(vals, axis=0, keepdims=True)
    first = jnp.min(jnp.where(vals == m, iota, n), axis=0, keepdims=True)
    return m, first


def _route_kernel(lg_ref, bias_ref, idx_ref, gate_ref):
    n_e, tn = lg_ref.shape
    scores = jax.nn.sigmoid(lg_ref[...])
    choice = scores + bias_ref[...]
    neg = -jnp.inf
    gi = lax.broadcasted_iota(I32, (GROUP_SIZE, tn), 0)
    gs_rows = []
    for g in range(N_GROUPS):
        blk = choice[g * GROUP_SIZE:(g + 1) * GROUP_SIZE]
        m1, f1 = _first_argmax(blk, gi, GROUP_SIZE)
        m2 = jnp.max(jnp.where(gi == f1, neg, blk), axis=0, keepdims=True)
        gs_rows.append(m1 + m2)
    gs = jnp.concatenate(gs_rows, axis=0)
    ri = lax.broadcasted_iota(I32, (N_GROUPS, tn), 0)
    gmask = jnp.zeros((N_GROUPS, tn), jnp.bool_)
    for _ in range(TOPK_GROUPS):
        _, f = _first_argmax(gs, ri, N_GROUPS)
        hit = ri == f
        gmask = gmask | hit
        gs = jnp.where(hit, neg, gs)
    gm = jnp.where(gmask, 1.0, 0.0)
    emask = jnp.concatenate(
        [jnp.broadcast_to(gm[g:g + 1], (GROUP_SIZE, tn)) for g in range(N_GROUPS)], axis=0)
    masked = jnp.where(emask > 0.5, choice, neg)
    ei = lax.broadcasted_iota(I32, (n_e, tn), 0)
    idx_rows, gate_rows = [], []
    for _ in range(TOP_K):
        _, f = _first_argmax(masked, ei, n_e)
        hit = ei == f
        idx_rows.append(f)
        gate_rows.append(jnp.sum(jnp.where(hit, scores, 0.0), axis=0, keepdims=True))
        masked = jnp.where(hit, neg, masked)
    gate = jnp.concatenate(gate_rows, axis=0)
    gate = gate / jnp.sum(gate, axis=0, keepdims=True) * ROUTE_SCALE
    idx_ref[...] = jnp.concatenate(idx_rows, axis=0)
    gate_ref[...] = gate


def route_topk(logits_t, bias, tn=512):
    n_e, n = logits_t.shape
    return pl.pallas_call(
        _route_kernel,
        grid=(n // tn,),
        in_specs=[pl.BlockSpec((n_e, tn), lambda i: (0, i)), pl.BlockSpec((n_e, 1), lambda i: (0, 0))],
        out_specs=[pl.BlockSpec((TOP_K, tn), lambda i: (0, i)), pl.BlockSpec((TOP_K, tn), lambda i: (0, i))],
        out_shape=[jax.ShapeDtypeStruct((TOP_K, n), I32), jax.ShapeDtypeStruct((TOP_K, n), F32)],
        compiler_params=_cparams(("parallel",)),
        name="route_topk",
    )(logits_t, bias.reshape(n_e, 1))


def _rank_kernel(idx_ref, su_ref, rank_ref, cnt_ref, carry_s):
    n_e = cnt_ref.shape[0]
    tn = idx_ref.shape[1]

    @pl.when(pl.program_id(0) == 0)
    def _():
        carry_s[...] = jnp.zeros_like(carry_s)

    idx = idx_ref[...]
    ei = lax.broadcasted_iota(I32, (n_e, tn), 0)
    hits = [ei == idx[k:k + 1] for k in range(TOP_K)]
    member = jnp.zeros((n_e, tn), F32)
    for h in hits:
        member = member + jnp.where(h, 1.0, 0.0)
    prefix = jnp.dot(member.astype(BF16), su_ref[...], preferred_element_type=F32) + carry_s[:, 0:1]
    rank_ref[...] = jnp.concatenate(
        [jnp.sum(jnp.where(h, prefix, 0.0), axis=0, keepdims=True) for h in hits], axis=0).astype(I32)
    carry_s[...] = carry_s[...] + jnp.sum(member, axis=1, keepdims=True)
    cnt_ref[...] = carry_s[...].astype(I32)


def expert_ranks(idx_t, tn=512):
    k, n = idx_t.shape
    su = jnp.asarray(np.triu(np.ones((tn, tn), np.float32), 1), BF16)
    return pl.pallas_call(
        _rank_kernel,
        grid=(n // tn,),
        in_specs=[pl.BlockSpec((k, tn), lambda i: (0, i)), pl.BlockSpec((tn, tn), lambda i: (0, 0))],
        out_specs=[pl.BlockSpec((k, tn), lambda i: (0, i)), pl.BlockSpec((N_EXPERTS, LANES), lambda i: (0, 0))],
        out_shape=[jax.ShapeDtypeStruct((k, n), I32), jax.ShapeDtypeStruct((N_EXPERTS, LANES), I32)],
        scratch_shapes=[pltpu.VMEM((N_EXPERTS, LANES), F32)],
        compiler_params=_cparams(("arbitrary",)),
        name="expert_ranks",
    )(idx_t, su)


def _dest_kernel(idx_ref, rank_ref, start_ref, dest_ref):
    n_e = start_ref.shape[0]
    tn = idx_ref.shape[1]
    idx = idx_ref[...]
    ei = lax.broadcasted_iota(I32, (n_e, tn), 0)
    start = start_ref[...]
    base = jnp.concatenate(
        [jnp.sum(jnp.where(ei == idx[k:k + 1], start, 0.0), axis=0, keepdims=True) for k in range(TOP_K)],
        axis=0)
    dest_ref[...] = base.astype(I32) + rank_ref[...]


def assignment_rows(idx_t, rank_t, padded_start, tn=512):
    k, n = idx_t.shape
    blk = pl.BlockSpec((k, tn), lambda i: (0, i))
    return pl.pallas_call(
        _dest_kernel,
        grid=(n // tn,),
        in_specs=[blk, blk, pl.BlockSpec((N_EXPERTS, 1), lambda i: (0, 0))],
        out_specs=blk,
        out_shape=jax.ShapeDtypeStruct((k, n), I32),
        compiler_params=_cparams(("parallel",)),
        name="assignment_rows",
    )(idx_t, rank_t, padded_start.astype(F32).reshape(N_EXPERTS, 1))


WEIGHT_DMA_PRIORITY = 1
ROW_GROUP = 4


def _rowmap_kernel(dest_ref, tok_ref):
    n_tok = dest_ref.shape[0] // TOP_K

    def body(j, c):
        for k in range(TOP_K):
            tok_ref[dest_ref[k * n_tok + j]] = j
        return c

    lax.fori_loop(0, n_tok, body, 0, unroll=4)


def row_tokens(dest_flat, n_rows):
    return pl.pallas_call(
        _rowmap_kernel,
        grid_spec=pltpu.PrefetchScalarGridSpec(
            num_scalar_prefetch=1,
            grid=(1,),
            in_specs=[],
            out_specs=pl.BlockSpec(memory_space=pltpu.SMEM),
        ),
        out_shape=jax.ShapeDtypeStruct((n_rows,), I32),
        compiler_params=_cparams(("arbitrary",)),
        name="row_tokens",
    )(dest_flat)


def _experts_kernel(be_ref, nxt_ref, nxt2_ref, slot_ref, cnt_ref, tok_ref, nb_ref,
                    h1p_ref, wg_ref, wu_ref, wd_ref, ys_ref,
                    xbuf, wgb, wub, wdb, wg_s, wu_s, wd_s, gsem, wsem):
    tm = ys_ref.shape[0] // SUBLANES
    i = pl.program_id(0)
    nb = nb_ref[0]

    def weight_copies(e, slot):
        return (pltpu.make_async_copy(wg_ref.at[e], wgb.at[slot], wsem.at[slot]),
                pltpu.make_async_copy(wu_ref.at[e], wub.at[slot], wsem.at[slot]),
                pltpu.make_async_copy(wd_ref.at[e], wdb.at[slot], wsem.at[slot]))

    def start_block(grp, pos, slot):
        blk = jnp.minimum(grp * ROW_GROUP + pos, nb - 1)
        last = cnt_ref[blk] - 1
        base = blk * tm
        dst0 = pos * tm

        def issue(r, c):
            t = tok_ref[base + jnp.minimum(r, last)]
            pltpu.make_async_copy(h1p_ref.at[pl.ds(t * SUBLANES, SUBLANES)],
                                  xbuf.at[slot, pl.ds((dst0 + r) * SUBLANES, SUBLANES)],
                                  gsem.at[slot]).start()
            return c

        lax.fori_loop(0, tm, issue, 0, unroll=8)

    @pl.when(i < nb)
    def _():
        e = be_ref[i]
        wslot = slot_ref[i]
        grp = lax.div(i, ROW_GROUP)
        sub = lax.rem(i, ROW_GROUP)
        gslot = lax.rem(grp, 2)

        @pl.when(i == 0)
        def _():
            for cp in weight_copies(e, wslot):
                cp.start(priority=WEIGHT_DMA_PRIORITY)
            nxt = nxt_ref[0]

            @pl.when(nxt >= 0)
            def _():
                for cp in weight_copies(nxt, 1 - wslot):
                    cp.start(priority=WEIGHT_DMA_PRIORITY)

            for g in range(ROW_GROUP):
                start_block(0, g, 0)

        @pl.when((i == 0) | (e != be_ref[jnp.maximum(i - 1, 0)]))
        def _():
            for cp in weight_copies(e, wslot):
                cp.wait()
            wg_s[...] = wgb[wslot].astype(BF16)
            wu_s[...] = wub[wslot].astype(BF16)
            wd_s[...] = wdb[wslot].astype(BF16)
            nxt2 = nxt2_ref[i]

            @pl.when(nxt2 >= 0)
            def _():
                for cp in weight_copies(nxt2, wslot):
                    cp.start(priority=WEIGHT_DMA_PRIORITY)

        @pl.when(sub == 0)
        def _():
            pltpu.make_async_copy(h1p_ref.at[pl.ds(0, ROW_GROUP * tm * SUBLANES)], xbuf.at[gslot],
                                  gsem.at[gslot]).wait()

        @pl.when((grp + 1) * ROW_GROUP < nb)
        def _():
            start_block(grp + 1, sub, 1 - gslot)

        row0 = pl.multiple_of(sub * (tm * SUBLANES), tm * SUBLANES)
        x = _unpack_bf16_pair(_load_row_tiles(xbuf.at[gslot, pl.ds(row0, tm * SUBLANES)], tm))
        a = jnp.dot(x, wg_s[...], preferred_element_type=F32)
        u = jnp.dot(x, wu_s[...], preferred_element_type=F32)
        hid = (jax.nn.silu(a) * u).astype(BF16)
        y = jnp.dot(hid, wd_s[...], preferred_element_type=F32)
        _store_row_tiles(ys_ref, _pack_bf16_pair(y))


def grouped_experts(h1p_tiles, row_tok, block_expert, block_next, block_next2, block_slot, block_count,
                    n_blocks_used, w_gate, w_up, w_down, tm=MOE_BLOCK):
    n_rows = row_tok.shape[0]
    n_e, d, ff = w_gate.shape
    n_blocks = n_rows // tm
    rowmap = lambda i, be, nx, nx2, sl, cn, tok, nb: (jnp.minimum(i, jnp.maximum(nb[0] - 1, 0)), 0)
    hbm = pl.BlockSpec(memory_space=pl.ANY)
    return pl.pallas_call(
        _experts_kernel,
        grid_spec=pltpu.PrefetchScalarGridSpec(
            num_scalar_prefetch=7,
            grid=(n_blocks,),
            in_specs=[hbm, hbm, hbm, hbm],
            out_specs=pl.BlockSpec((tm * SUBLANES, LANES), rowmap),
            scratch_shapes=[pltpu.VMEM((2, ROW_GROUP * tm * SUBLANES, LANES), U32),
                            pltpu.VMEM((2, d, ff), F32), pltpu.VMEM((2, d, ff), F32), pltpu.VMEM((2, ff, d), F32),
                            pltpu.VMEM((d, ff), BF16), pltpu.VMEM((d, ff), BF16), pltpu.VMEM((ff, d), BF16),
                            pltpu.SemaphoreType.DMA((2,)), pltpu.SemaphoreType.DMA((2,))],
        ),
        out_shape=jax.ShapeDtypeStruct((n_rows * SUBLANES, LANES), U32),
        compiler_params=_cparams(("arbitrary",)),
        name="grouped_experts",
    )(block_expert, block_next, block_next2, block_slot, block_count, row_tok, n_blocks_used,
      h1p_tiles, w_gate, w_up, w_down)


def _combine_kernel(dest_ref, h1_ref, h1p_ref, gate_ref, wsg_ref, wsu_ref, wsd_ref, g2_ref, b2_ref, ys_ref,
                    o_ref, buf_s, sem):
    tm = h1_ref.shape[0]
    i = pl.program_id(0)
    n_steps = pl.num_programs(0)
    n_tok = n_steps * tm
    slot = lax.rem(i, 2)

    def start_rows(step, sl):
        def issue(r, c):
            for k in range(TOP_K):
                row = dest_ref[k * n_tok + step * tm + r]
                pltpu.make_async_copy(ys_ref.at[pl.ds(row * SUBLANES, SUBLANES)],
                                      buf_s.at[sl, k, pl.ds(r * SUBLANES, SUBLANES)],
                                      sem.at[sl]).start(priority=k % 2)
            return c

        lax.fori_loop(0, tm, issue, 0, unroll=2)

    @pl.when(i == 0)
    def _():
        start_rows(0, 0)

    @pl.when(i + 1 < n_steps)
    def _():
        start_rows(i + 1, 1 - slot)

    xb = _unpack_bf16_pair(_load_row_tiles(h1p_ref, tm))
    a = jnp.dot(xb, wsg_ref[...], preferred_element_type=F32)
    u = jnp.dot(xb, wsu_ref[...], preferred_element_type=F32)
    shared = jnp.dot((jax.nn.silu(a) * u).astype(BF16), wsd_ref[...], preferred_element_type=F32)

    for k in range(TOP_K):
        pltpu.make_async_copy(ys_ref.at[pl.ds(0, tm * SUBLANES)], buf_s.at[slot, k], sem.at[slot]).wait()

    gate = gate_ref[...]
    routed = jnp.zeros_like(shared)
    for k in range(TOP_K):
        routed = routed + _unpack_pair_f32(_load_row_tiles(buf_s.at[slot, k], tm)) * gate[:, k:k + 1]
    o_ref[...] = _layer_norm(DN_ALPHA * h1_ref[...] + routed + shared, g2_ref[...], b2_ref[...])


def combine_shared_ln(dest_flat, h1, h1p_tiles, gate_nk, ys_tiles, wsg, wsu, wsd, g2, b2, tm=128):
    n, d = h1.shape
    ff = wsg.shape[1]
    row = lambda i, dest: (i, 0)
    const = lambda i, dest: (0, 0)
    return pl.pallas_call(
        _combine_kernel,
        grid_spec=pltpu.PrefetchScalarGridSpec(
            num_scalar_prefetch=1,
            grid=(n // tm,),
            in_specs=[pl.BlockSpec((tm, d), row), pl.BlockSpec((tm * SUBLANES, LANES), row),
                      pl.BlockSpec((tm, TOP_K), row),
                      pl.BlockSpec((d, ff), const), pl.BlockSpec((d, ff), const), pl.BlockSpec((ff, d), const),
                      pl.BlockSpec((1, d), const), pl.BlockSpec((1, d), const),
                      pl.BlockSpec(memory_space=pl.ANY)],
            out_specs=pl.BlockSpec((tm, d), row),
            scratch_shapes=[pltpu.VMEM((2, TOP_K, tm * SUBLANES, LANES), U32), pltpu.SemaphoreType.DMA((2,))],
        ),
        out_shape=jax.ShapeDtypeStruct((n, d), F32),
        compiler_params=_cparams(("arbitrary",)),
        name="combine_shared_ln",
    )(dest_flat, h1, h1p_tiles, gate_nk, wsg, wsu, wsd, g2.reshape(1, d), b2.reshape(1, d), ys_tiles)


def kernel(x, ln_emb_g, ln_emb_b, w_in, lb_logits, hg_norm_w, conv_w, conv_b, rg_wa, rg_ba, rg_wx, rg_bx,
           rg_lam, w_out, ln1_g, ln1_b, w_router, router_bias, w_gate, w_up, w_down, ws_gate, ws_up, ws_down,
           ln2_g, ln2_b):
    batch, seq, d = x.shape
    assert d == 2 * SUBLANES * LANES, "packed rows must fill one 8x128 tile"
    n_tok = batch * seq
    layer = 0
    x2 = x.reshape(n_tok, d)
    lb = jnp.cumsum(jax.nn.softmax(lb_logits.astype(F32), axis=1), axis=1)

    proj = ln_inproj(x2, ln_emb_g, ln_emb_b, w_in[layer].astype(BF16))
    y_hg = hgrn2_mixer(proj, lb[0, layer], lb[1, layer], hg_norm_w[layer], batch, seq)
    y_rg = rglru_mixer(proj, conv_w[layer], conv_b[layer], rg_wa[layer], rg_ba[layer], rg_wx[layer],
                       rg_bx[layer], rg_lam[layer], batch, seq)
    h1, h1p_tiles, logits_t = outproj_ln_router(
        y_hg, y_rg, x2, ln_emb_g, ln_emb_b, w_out[layer].astype(BF16), ln1_g[layer], ln1_b[layer],
        w_router[layer].T.astype(BF16))

    idx_t, gate_t = route_topk(logits_t, router_bias[layer])
    rank_t, counts = expert_ranks(idx_t)
    counts = counts[:, 0]

    n_assign = n_tok * TOP_K
    n_blocks = -(-(n_assign + N_EXPERTS * (MOE_BLOCK - 1)) // MOE_BLOCK)
    n_rows = n_blocks * MOE_BLOCK
    padded = (counts + MOE_BLOCK - 1) // MOE_BLOCK * MOE_BLOCK
    padded_end = jnp.cumsum(padded)
    padded_start = padded_end - padded
    n_used_blocks = (padded_end[-1:] // MOE_BLOCK).astype(I32)
    blk_row = jnp.minimum(jnp.arange(n_blocks, dtype=I32), n_used_blocks[0] - 1) * MOE_BLOCK
    block_expert = jnp.minimum(
        jnp.sum((padded_end[None, :] <= blk_row[:, None]).astype(I32), axis=1), N_EXPERTS - 1)
    e_ids = jnp.arange(N_EXPERTS, dtype=I32)
    active = counts > 0
    later = lax.cummin(jnp.where(active, e_ids, N_EXPERTS)[::-1])[::-1]
    nxt = jnp.concatenate([later[1:], jnp.full((1,), N_EXPERTS, I32)])
    next1 = jnp.where(nxt < N_EXPERTS, nxt, -1).astype(I32)
    hot1 = (next1[:, None] == e_ids[None, :]).astype(I32)
    next2 = jnp.sum(hot1 * (next1 + 1)[None, :], axis=1) - 1
    ordinal = jnp.cumsum(active.astype(I32)) - 1
    onehot = (block_expert[:, None] == e_ids[None, :]).astype(I32)
    block_next = jnp.sum(onehot * next1[None, :], axis=1)
    block_next2 = jnp.sum(onehot * next2[None, :], axis=1)
    block_slot = jnp.sum(onehot * (ordinal % 2)[None, :], axis=1)
    block_count = jnp.clip(jnp.sum(onehot * (counts + padded_start)[None, :], axis=1) - blk_row, 1, MOE_BLOCK)

    dest_flat = assignment_rows(idx_t, rank_t, padded_start).reshape(-1)
    row_tok = row_tokens(dest_flat, n_rows)

    ys_tiles = grouped_experts(h1p_tiles, row_tok, block_expert, block_next, block_next2, block_slot,
                               block_count, n_used_blocks, w_gate[layer], w_up[layer], w_down[layer])
    out = combine_shared_ln(dest_flat, h1, h1p_tiles, gate_t.T, ys_tiles,
                            ws_gate[layer].astype(BF16), ws_up[layer].astype(BF16),
                            ws_down[layer].astype(BF16), ln2_g[layer], ln2_b[layer])
    return out.reshape(batch, seq, d)
```

```python
import functools

import numpy as np
import jax
import jax.numpy as jnp
from jax import lax
from jax.experimental import pallas as pl
from jax.experimental.pallas import tpu as pltpu

F32 = jnp.float32
BF16 = jnp.bfloat16
U32 = jnp.uint32
I32 = jnp.int32

D_MODEL = 2048
DEPTH = 1
HG_KEY_DIM = 128
HG_VAL_DIM = 128
HG_WIDTH = D_MODEL // 2
HG_HEADS = HG_WIDTH // HG_VAL_DIM
HG_KEY_WIDTH = HG_HEADS * HG_KEY_DIM
RG_WIDTH = D_MODEL - HG_WIDTH
RG_BLOCKS = 8
RG_BLOCK_DIM = RG_WIDTH // RG_BLOCKS
RG_CONV = 4
RG_C = 8.0
MIX_WIDTH = HG_WIDTH + RG_WIDTH
IN_WIDTHS = (HG_KEY_WIDTH, HG_KEY_WIDTH, HG_KEY_WIDTH, HG_WIDTH, HG_WIDTH, RG_WIDTH, RG_WIDTH)
IN_WIDTH = sum(IN_WIDTHS)
N_EXPERTS = 256
TOP_K = 8
N_GROUPS = 8
TOPK_GROUPS = 4
GROUP_SIZE = N_EXPERTS // N_GROUPS
EXPERT_FF = 512
SHARED_FF = 512
ROUTE_SCALE = 2.5
MOE_BLOCK = 128
DN_ALPHA = (2 * DEPTH) ** 0.25
LN_EPS = 1e-5
RMS_EPS = 1e-6

LANES = 128
SUBLANES = 8
VMEM_LIMIT_BYTES = 56 * 1024 * 1024

HG_CHUNK = 128
HG_SUB = 8
HG_LEVELS = (16, 32, 64, 128)


def _cparams(sem, vmem=VMEM_LIMIT_BYTES):
    return pltpu.CompilerParams(dimension_semantics=sem, vmem_limit_bytes=vmem)


def _layer_norm(x, g, b):
    mu = jnp.mean(x, axis=-1, keepdims=True)
    xc = x - mu
    var = jnp.mean(xc * xc, axis=-1, keepdims=True)
    return xc * lax.rsqrt(var + LN_EPS) * g + b


def _pack_bf16_pair(x):
    n = x.shape[1] // 2
    bits = lax.bitcast_convert_type(x.astype(BF16).astype(F32), U32)
    return (bits[:, :n] >> 16) | (bits[:, n:] & jnp.uint32(0xFFFF0000))


def _unpack_pair_f32(w):
    lo = lax.bitcast_convert_type(w << 16, F32)
    hi = lax.bitcast_convert_type(w & jnp.uint32(0xFFFF0000), F32)
    return jnp.concatenate([lo, hi], axis=1)


def _unpack_bf16_pair(w):
    return _unpack_pair_f32(w).astype(BF16)


def _store_row_tiles(ref, packed):
    m = packed.shape[0]
    for s in range(SUBLANES):
        ref[pl.ds(s, m, stride=SUBLANES), :] = packed[:, s * LANES:(s + 1) * LANES]


def _load_row_tiles(ref, m):
    return jnp.concatenate([ref[pl.ds(s, m, stride=SUBLANES), :] for s in range(SUBLANES)], axis=1)


def _ln_inproj_kernel(x_ref, g_ref, b_ref, w_ref, o_ref, hb_ref):
    @pl.when(pl.program_id(1) == 0)
    def _():
        hb_ref[...] = _layer_norm(x_ref[...], g_ref[...], b_ref[...]).astype(BF16)

    o_ref[...] = jnp.dot(hb_ref[...], w_ref[...], preferred_element_type=F32).astype(o_ref.dtype)


def ln_inproj(x2, g, b, w_bf16, tm=1024, tn=1024):
    n, d = x2.shape
    width = w_bf16.shape[1]
    return pl.pallas_call(
        _ln_inproj_kernel,
        grid=(n // tm, width // tn),
        in_specs=[
            pl.BlockSpec((tm, d), lambda i, j: (i, 0)),
            pl.BlockSpec((1, d), lambda i, j: (0, 0)),
            pl.BlockSpec((1, d), lambda i, j: (0, 0)),
            pl.BlockSpec((d, tn), lambda i, j: (0, j)),
        ],
        out_specs=pl.BlockSpec((tm, tn), lambda i, j: (i, j)),
        out_shape=jax.ShapeDtypeStruct((n, width), BF16),
        scratch_shapes=[pltpu.VMEM((tm, d), BF16)],
        compiler_params=_cparams(("parallel", "arbitrary")),
        name="ln_inproj",
    )(x2, g.reshape(1, d), b.reshape(1, d), w_bf16)


def _hgrn2_constants():
    c = HG_CHUNK
    t = np.arange(c)[:, None]
    u = np.arange(c)[None, :]
    fwd = [(u <= t), (u > t)]
    bwd = [(u >= t), (u < t)]
    for m in HG_LEVELS:
        start = (t // m) * m
        mid = start + m // 2 - 1
        second = t > mid
        fwd.append(np.where(second, (u > mid) & (u <= t), (u > t) & (u <= mid)))
        bwd.append(np.where(second, (u > mid) & (u < t), (u >= t) & (u <= mid)))
    fwd = np.concatenate(fwd, axis=0).astype(np.float32)
    bwd = np.concatenate(bwd, axis=0).astype(np.float32)
    fwd = np.concatenate([fwd, fwd], axis=1)
    bwd = np.concatenate([bwd, bwd], axis=1)
    lvl = np.full((c, c), -1, np.int32)
    tt = np.arange(c)[:, None]
    ss = np.arange(c)[None, :]
    for li in reversed(range(len(HG_LEVELS))):
        m = HG_LEVELS[li]
        lvl = np.where((tt // m) == (ss // m), li, lvl)
    lvl = np.where((tt // HG_SUB) == (ss // HG_SUB), -1, lvl).astype(np.int32)
    return fwd, bwd, lvl


def _hgrn2_kernel(q_ref, zf_ref, zb_ref, v_ref, g_ref, lbf_ref, lbb_ref, nw_ref,
                  mf_ref, mb_ref, lvl_ref, o_ref,
                  qs_s, v_s, lf_s, kf_s, lb_s, kb_s, cf_s, cb_s,
                  qif_s, kif_s, qib_s, kib_s, df_s, db_s, acc_s, stf_s, stb_s):
    seq = q_ref.shape[0]
    c = HG_CHUNK
    n_chunks = seq // c
    n_lvl = len(HG_LEVELS)

    qs_s[...] = jax.nn.silu(q_ref[...].astype(F32))
    v_s[...] = v_ref[...].astype(F32)
    for z_ref, lb_ref, lf_out, k_out in ((zf_ref, lbf_ref, lf_s, kf_s), (zb_ref, lbb_ref, lb_s, kb_s)):
        z = z_ref[...].astype(F32)
        lb = lb_ref[...]
        sig = jax.nn.sigmoid(z)
        lf_out[...] = jnp.log(lb + (1.0 - lb) * sig)
        k_out[...] = (1.0 - lb) * (1.0 - sig)

    ones_mat = jnp.ones((LANES, LANES), BF16)
    lvl = lvl_ref[...]
    row_c = lax.broadcasted_iota(I32, (c, LANES), 0)
    lane8 = lax.broadcasted_iota(I32, (HG_SUB, LANES), 1)
    sub8 = lax.broadcasted_iota(I32, (HG_SUB, LANES), 0)
    n_stack = (2 + n_lvl) * c

    def intra(n, carry):
        r0 = pl.multiple_of(n * c, c)
        rows = pl.ds(r0, c)
        qs = qs_s[rows, :]
        kf = kf_s[rows, :]
        kb = kb_s[rows, :]
        hl = []
        for lfs in (lf_s, lb_s):
            lf = lfs[rows, :]
            hi = lf.astype(BF16)
            hl.append(jnp.concatenate([hi, (lf - hi.astype(F32)).astype(BF16)], axis=0))
        ef = jnp.dot(mf_ref[...], hl[0], preferred_element_type=F32)
        eb = jnp.dot(mb_ref[...], hl[1], preferred_element_type=F32)
        for (e, kk, cs, qi_s, ki_s, d_s, tot_row) in ((ef, kf, cf_s, qif_s, kif_s, df_s, c - 1),
                                                    (eb, kb, cb_s, qib_s, kib_s, db_s, 0)):
            cum = e[0:c]
            cs[rows, :] = cum
            qi_s[rows, :] = (qs * jnp.exp(cum)).astype(BF16)
            ki_s[rows, :] = (kk * jnp.exp(e[c:2 * c])).astype(BF16)
            d_s[pl.ds(n, 1), :] = jnp.exp(cum[tot_row:tot_row + 1, :])

        a_mat = jnp.zeros((c, c), F32)
        for li, m in enumerate(HG_LEVELS):
            second = (row_c & (m // 2)) != 0
            wf = jnp.exp(ef[(2 + li) * c:(3 + li) * c])
            wb = jnp.exp(eb[(2 + li) * c:(3 + li) * c])
            qcat = jnp.concatenate([qs * jnp.where(second, wf, 0.0), qs * jnp.where(second, 0.0, wb)], axis=1)
            kcat = jnp.concatenate([kf * jnp.where(second, 0.0, wf), kb * jnp.where(second, wb, 0.0)], axis=1)
            sc = lax.dot_general(qcat.astype(BF16), kcat.astype(BF16),
                                 (((1,), (1,)), ((), ())), preferred_element_type=F32)
            a_mat = jnp.where(lvl == li, sc, a_mat)

        p_list = []
        for j in range(c // HG_SUB):
            rj = r0 + j * HG_SUB
            sl = pl.ds(rj, HG_SUB)
            cf = cf_s[sl, :]
            cb = cb_s[sl, :]
            qj = qs_s[sl, :]
            for s in range(HG_SUB):
                src = pl.ds(rj + s, 1)
                df = jnp.where(sub8 >= s, cf - cf_s[src, :], -jnp.inf)
                db = jnp.where(sub8 <= s, cb - cb_s[src, :], -jnp.inf)
                p_list.append(qj * (kf_s[src, :] * jnp.exp(df) + kb_s[src, :] * jnp.exp(db)))
        p_all = jnp.concatenate(p_list, axis=0).astype(BF16)
        red = jnp.dot(p_all, ones_mat, preferred_element_type=F32)
        diag_rows = []
        for j in range(c // HG_SUB):
            blk = jnp.zeros((HG_SUB, LANES), F32)
            for s in range(HG_SUB):
                idx = (j * HG_SUB + s) * HG_SUB
                blk = jnp.where(lane8 == j * HG_SUB + s, red[idx:idx + HG_SUB], blk)
            diag_rows.append(blk)
        a_mat = a_mat + jnp.concatenate(diag_rows, axis=0)
        acc_s[rows, :] = jnp.dot(a_mat.astype(BF16), v_s[rows, :].astype(BF16), preferred_element_type=F32)
        return carry

    lax.fori_loop(0, n_chunks, intra, 0, unroll=4)

    stf_s[...] = jnp.zeros_like(stf_s)
    stb_s[...] = jnp.zeros_like(stb_s)

    def inter(n, carry):
        for (st_s, qi_s, ki_s, d_s, idx) in ((stf_s, qif_s, kif_s, df_s, n),
                                             (stb_s, qib_s, kib_s, db_s, n_chunks - 1 - n)):
            rows = pl.ds(pl.multiple_of(idx * c, c), c)
            st = st_s[...]
            acc_s[rows, :] += lax.dot_general(qi_s[rows, :], st.astype(BF16),
                                              (((1,), (1,)), ((), ())), preferred_element_type=F32)
            upd = lax.dot_general(v_s[rows, :].astype(BF16), ki_s[rows, :],
                                  (((0,), (0,)), ((), ())), preferred_element_type=F32)
            st_s[...] = st * d_s[pl.ds(idx, 1), :] + upd
        return carry

    lax.fori_loop(0, n_chunks, inter, 0, unroll=4)

    o = acc_s[...]
    o = o * lax.rsqrt(jnp.mean(o * o, axis=-1, keepdims=True) + RMS_EPS) * nw_ref[...]
    o_ref[...] = (o * jax.nn.silu(g_ref[...].astype(F32))).astype(o_ref.dtype)


def hgrn2_mixer(proj, lb_f, lb_b, norm_w, batch, seq):
    n = proj.shape[0]
    heads = HG_HEADS
    c = HG_CHUNK
    n_chunks = seq // c
    mf, mb, lvl = _hgrn2_constants()
    n_stack = mf.shape[0]

    def col(off):
        return pl.BlockSpec((seq, LANES), lambda b, h, off=off: (b, off + h))

    def per_head(arr):
        return pl.BlockSpec((1, LANES), lambda b, h: (0, h))

    const2 = lambda b, h: (0, 0)
    seq_f32 = pltpu.VMEM((seq, LANES), F32)
    seq_bf16 = pltpu.VMEM((seq, LANES), BF16)
    dec = pltpu.VMEM((n_chunks, LANES), F32)
    state = pltpu.VMEM((LANES, LANES), F32)
    return pl.pallas_call(
        _hgrn2_kernel,
        grid=(batch, heads),
        in_specs=[col(0), col(heads), col(2 * heads), col(3 * heads), col(4 * heads),
                  per_head(lb_f), per_head(lb_b),
                  pl.BlockSpec((1, LANES), const2),
                  pl.BlockSpec((n_stack, 2 * c), const2),
                  pl.BlockSpec((n_stack, 2 * c), const2),
                  pl.BlockSpec((c, c), const2)],
        out_specs=pl.BlockSpec((seq, LANES), lambda b, h: (b, h)),
        out_shape=jax.ShapeDtypeStruct((n, HG_WIDTH), BF16),
        scratch_shapes=[seq_f32] * 8 + [seq_bf16] * 4 + [dec, dec, seq_f32, state, state],
        compiler_params=_cparams(("parallel", "parallel")),
        name="hgrn2_mixer",
    )(proj, proj, proj, proj, proj, lb_f.reshape(1, -1), lb_b.reshape(1, -1), norm_w.reshape(1, -1),
      jnp.asarray(mf, BF16), jnp.asarray(mb, BF16), jnp.asarray(lvl))


def _shift_rows(x, d, fill, reverse):
    pad = jnp.full((d, x.shape[1]), fill, x.dtype)
    if reverse:
        return jnp.concatenate([x[d:], pad], axis=0)
    return jnp.concatenate([pad, x[:-d]], axis=0)


def _linear_scan(a, u, reverse):
    seq = a.shape[0]
    d = 1
    while d < seq:
        u = u + a * _shift_rows(u, d, 0.0, reverse)
        if d * 2 < seq:
            a = a * _shift_rows(a, d, 1.0, reverse)
        d *= 2
    return u


def _rglru_kernel(xr_ref, gr_ref, cw_ref, cb_ref, wa_ref, ba_ref, wx_ref, bx_ref, lam_ref, o_ref, xp_s):
    seq = xr_ref.shape[0]
    pad = SUBLANES
    pad_l = RG_CONV // 2
    xp_s[0:pad, :] = jnp.zeros((pad, LANES), F32)
    xp_s[pad + seq:, :] = jnp.zeros((pad, LANES), F32)
    xp_s[pad:pad + seq, :] = xr_ref[...].astype(F32)
    xc = cb_ref[...]
    for j in range(RG_CONV):
        off = pad + j - pad_l
        xc = xc + xp_s[off:off + seq, :] * cw_ref[j:j + 1, :]
    xcb = xc.astype(BF16)
    h = None
    for d in range(2):
        r = jax.nn.sigmoid(jnp.dot(xcb, wa_ref[d, 0], preferred_element_type=F32) + ba_ref[d:d + 1, :])
        i = jax.nn.sigmoid(jnp.dot(xcb, wx_ref[d, 0], preferred_element_type=F32) + bx_ref[d:d + 1, :])
        log_a = (-RG_C) * r * jax.nn.softplus(-lam_ref[d:d + 1, :])
        a = jnp.exp(log_a)
        u = jnp.sqrt(-jnp.tanh(log_a) * (a * a + 1.0)) * (i * xc)
        hd = _linear_scan(a, u, reverse=(d == 1))
        h = hd if h is None else h + hd
    o_ref[...] = (h * jax.nn.gelu(gr_ref[...].astype(F32))).astype(o_ref.dtype)


def rglru_mixer(proj, conv_w, conv_b, wa, ba, wx, bx, lam, batch, seq):
    n = proj.shape[0]
    x_off = (3 * HG_KEY_WIDTH + 2 * HG_WIDTH) // LANES
    g_off = x_off + RG_WIDTH // LANES
    blk = lambda b, j: (0, j)
    return pl.pallas_call(
        _rglru_kernel,
        grid=(batch, RG_BLOCKS),
        in_specs=[
            pl.BlockSpec((seq, LANES), lambda b, j: (b, x_off + j)),
            pl.BlockSpec((seq, LANES), lambda b, j: (b, g_off + j)),
            pl.BlockSpec((RG_CONV, LANES), blk),
            pl.BlockSpec((1, LANES), blk),
            pl.BlockSpec((2, 1, RG_BLOCK_DIM, RG_BLOCK_DIM), lambda b, j: (0, j, 0, 0)),
            pl.BlockSpec((2, LANES), blk),
            pl.BlockSpec((2, 1, RG_BLOCK_DIM, RG_BLOCK_DIM), lambda b, j: (0, j, 0, 0)),
            pl.BlockSpec((2, LANES), blk),
            pl.BlockSpec((2, LANES), blk),
        ],
        out_specs=pl.BlockSpec((seq, LANES), lambda b, j: (b, j)),
        out_shape=jax.ShapeDtypeStruct((n, RG_WIDTH), BF16),
        scratch_shapes=[pltpu.VMEM((seq + 2 * SUBLANES, LANES), F32)],
        compiler_params=_cparams(("parallel", "parallel")),
        name="rglru_mixer",
    )(proj, proj, conv_w, conv_b.reshape(1, -1), wa.astype(BF16), ba, wx.astype(BF16), bx, lam)


def _outproj_kernel(yh_ref, yr_ref, x_ref, g0_ref, b0_ref, wo_ref, g1_ref, b1_ref, wr_ref,
                    h1_ref, h1p_ref, lg_ref):
    half = yh_ref.shape[1]
    h0 = _layer_norm(x_ref[...], g0_ref[...], b0_ref[...])
    mix = jnp.dot(yh_ref[...], wo_ref[0:half, :], preferred_element_type=F32)
    mix = mix + jnp.dot(yr_ref[...], wo_ref[half:, :], preferred_element_type=F32)
    h1 = _layer_norm(DN_ALPHA * h0 + mix, g1_ref[...], b1_ref[...])
    h1_ref[...] = h1
    _store_row_tiles(h1p_ref, _pack_bf16_pair(h1))
    lg_ref[...] = lax.dot_general(wr_ref[...], h1.astype(BF16), (((1,), (1,)), ((), ())),
                                  preferred_element_type=F32)


def outproj_ln_router(y_hg, y_rg, x2, g0, b0, w_out_bf16, g1, b1, w_router_t_bf16, tm=256):
    n, d = x2.shape
    half = y_hg.shape[1]
    n_e = w_router_t_bf16.shape[0]
    row = lambda i: (i, 0)
    const = lambda i: (0, 0)
    vec = pl.BlockSpec((1, d), const)
    return pl.pallas_call(
        _outproj_kernel,
        grid=(n // tm,),
        in_specs=[pl.BlockSpec((tm, half), row), pl.BlockSpec((tm, half), row), pl.BlockSpec((tm, d), row),
                  vec, vec, pl.BlockSpec((2 * half, d), const), vec, vec,
                  pl.BlockSpec((n_e, d), const)],
        out_specs=[pl.BlockSpec((tm, d), row), pl.BlockSpec((tm * SUBLANES, LANES), row),
                   pl.BlockSpec((n_e, tm), lambda i: (0, i))],
        out_shape=[jax.ShapeDtypeStruct((n, d), F32), jax.ShapeDtypeStruct((n * SUBLANES, LANES), U32),
                   jax.ShapeDtypeStruct((n_e, n), F32)],
        compiler_params=_cparams(("parallel",)),
        name="outproj_ln_router",
    )(y_hg, y_rg, x2, g0.reshape(1, d), b0.reshape(1, d), w_out_bf16, g1.reshape(1, d), b1.reshape(1, d),
      w_router_t_bf16)


def _first_argmax(vals, iota, n):
    m = jnp.max(vals, axis=0, keepdims=True)
    first = jnp.min(jnp.where(vals == m, iota, n), axis=0, keepdims=True)
    return m, first


def _route_kernel(lg_ref, bias_ref, idx_ref, gate_ref):
    n_e, tn = lg_ref.shape
    scores = jax.nn.sigmoid(lg_ref[...])
    choice = scores + bias_ref[...]
    neg = -jnp.inf
    gi = lax.broadcasted_iota(I32, (GROUP_SIZE, tn), 0)
    gs_rows = []
    for g in range(N_GROUPS):
        blk = choice[g * GROUP_SIZE:(g + 1) * GROUP_SIZE]
        m1, f1 = _first_argmax(blk, gi, GROUP_SIZE)
        m2 = jnp.max(jnp.where(gi == f1, neg, blk), axis=0, keepdims=True)
        gs_rows.append(m1 + m2)
    gs = jnp.concatenate(gs_rows, axis=0)
    ri = lax.broadcasted_iota(I32, (N_GROUPS, tn), 0)
    gmask = jnp.zeros((N_GROUPS, tn), jnp.bool_)
    for _ in range(TOPK_GROUPS):
        _, f = _first_argmax(gs, ri, N_GROUPS)
        hit = ri == f
        gmask = gmask | hit
        gs = jnp.where(hit, neg, gs)
    gm = jnp.where(gmask, 1.0, 0.0)
    emask = jnp.concatenate(
        [jnp.broadcast_to(gm[g:g + 1], (GROUP_SIZE, tn)) for g in range(N_GROUPS)], axis=0)
    masked = jnp.where(emask > 0.5, choice, neg)
    ei = lax.broadcasted_iota(I32, (n_e, tn), 0)
    idx_rows, gate_rows = [], []
    for _ in range(TOP_K):
        _, f = _first_argmax(masked, ei, n_e)
        hit = ei == f
        idx_rows.append(f)
        gate_rows.append(jnp.sum(jnp.where(hit, scores, 0.0), axis=0, keepdims=True))
        masked = jnp.where(hit, neg, masked)
    gate = jnp.concatenate(gate_rows, axis=0)
    gate = gate / jnp.sum(gate, axis=0, keepdims=True) * ROUTE_SCALE
    idx_ref[...] = jnp.concatenate(idx_rows, axis=0)
    gate_ref[...] = gate


def route_topk(logits_t, bias, tn=512):
    n_e, n = logits_t.shape
    return pl.pallas_call(
        _route_kernel,
        grid=(n // tn,),
        in_specs=[pl.BlockSpec((n_e, tn), lambda i: (0, i)), pl.BlockSpec((n_e, 1), lambda i: (0, 0))],
        out_specs=[pl.BlockSpec((TOP_K, tn), lambda i: (0, i)), pl.BlockSpec((TOP_K, tn), lambda i: (0, i))],
        out_shape=[jax.ShapeDtypeStruct((TOP_K, n), I32), jax.ShapeDtypeStruct((TOP_K, n), F32)],
        compiler_params=_cparams(("parallel",)),
        name="route_topk",
    )(logits_t, bias.reshape(n_e, 1))


def _rank_kernel(idx_ref, su_ref, rank_ref, cnt_ref, carry_s):
    n_e = cnt_ref.shape[0]
    tn = idx_ref.shape[1]

    @pl.when(pl.program_id(0) == 0)
    def _():
        carry_s[...] = jnp.zeros_like(carry_s)

    idx = idx_ref[...]
    ei = lax.broadcasted_iota(I32, (n_e, tn), 0)
    hits = [ei == idx[k:k + 1] for k in range(TOP_K)]
    member = jnp.zeros((n_e, tn), F32)
    for h in hits:
        member = member + jnp.where(h, 1.0, 0.0)
    prefix = jnp.dot(member.astype(BF16), su_ref[...], preferred_element_type=F32) + carry_s[:, 0:1]
    rank_ref[...] = jnp.concatenate(
        [jnp.sum(jnp.where(h, prefix, 0.0), axis=0, keepdims=True) for h in hits], axis=0).astype(I32)
    carry_s[...] = carry_s[...] + jnp.sum(member, axis=1, keepdims=True)
    cnt_ref[...] = carry_s[...].astype(I32)


def expert_ranks(idx_t, tn=512):
    k, n = idx_t.shape
    su = jnp.asarray(np.triu(np.ones((tn, tn), np.float32), 1), BF16)
    return pl.pallas_call(
        _rank_kernel,
        grid=(n // tn,),
        in_specs=[pl.BlockSpec((k, tn), lambda i: (0, i)), pl.BlockSpec((tn, tn), lambda i: (0, 0))],
        out_specs=[pl.BlockSpec((k, tn), lambda i: (0, i)), pl.BlockSpec((N_EXPERTS, LANES), lambda i: (0, 0))],
        out_shape=[jax.ShapeDtypeStruct((k, n), I32), jax.ShapeDtypeStruct((N_EXPERTS, LANES), I32)],
        scratch_shapes=[pltpu.VMEM((N_EXPERTS, LANES), F32)],
        compiler_params=_cparams(("arbitrary",)),
        name="expert_ranks",
    )(idx_t, su)


def _dest_kernel(idx_ref, rank_ref, start_ref, dest_ref):
    n_e = start_ref.shape[0]
    tn = idx_ref.shape[1]
    idx = idx_ref[...]
    ei = lax.broadcasted_iota(I32, (n_e, tn), 0)
    start = start_ref[...]
    base = jnp.concatenate(
        [jnp.sum(jnp.where(ei == idx[k:k + 1], start, 0.0), axis=0, keepdims=True) for k in range(TOP_K)],
        axis=0)
    dest_ref[...] = base.astype(I32) + rank_ref[...]


def assignment_rows(idx_t, rank_t, padded_start, tn=512):
    k, n = idx_t.shape
    blk = pl.BlockSpec((k, tn), lambda i: (0, i))
    return pl.pallas_call(
        _dest_kernel,
        grid=(n // tn,),
        in_specs=[blk, blk, pl.BlockSpec((N_EXPERTS, 1), lambda i: (0, 0))],
        out_specs=blk,
        out_shape=jax.ShapeDtypeStruct((k, n), I32),
        compiler_params=_cparams(("parallel",)),
        name="assignment_rows",
    )(idx_t, rank_t, padded_start.astype(F32).reshape(N_EXPERTS, 1))


WEIGHT_DMA_PRIORITY = 1
ROW_GROUP = 8
ROW_CHUNK = 8


def _rowmap_kernel(dest_ref, tok_ref):
    n_tok = dest_ref.shape[0] // TOP_K

    def body(j, c):
        for k in range(TOP_K):
            tok_ref[dest_ref[k * n_tok + j]] = j
        return c

    lax.fori_loop(0, n_tok, body, 0, unroll=4)


def row_tokens(dest_flat, n_rows):
    return pl.pallas_call(
        _rowmap_kernel,
        grid_spec=pltpu.PrefetchScalarGridSpec(
            num_scalar_prefetch=1,
            grid=(1,),
            in_specs=[],
            out_specs=pl.BlockSpec(memory_space=pltpu.SMEM),
        ),
        out_shape=jax.ShapeDtypeStruct((n_rows,), I32),
        compiler_params=_cparams(("arbitrary",)),
        name="row_tokens",
    )(dest_flat)


def _experts_kernel(be_ref, nxt_ref, nxt2_ref, slot_ref, cnt_ref, tok_ref, nb_ref,
                    h1p_ref, wg_ref, wu_ref, wd_ref, ys_ref,
                    xbuf, wgb, wub, wdb, wg_s, wu_s, wd_s, gsem, wsem):
    tm = ys_ref.shape[0] // SUBLANES
    i = pl.program_id(0)
    nb = nb_ref[0]

    def weight_copies(e, slot):
        return (pltpu.make_async_copy(wg_ref.at[e], wgb.at[slot], wsem.at[slot]),
                pltpu.make_async_copy(wu_ref.at[e], wub.at[slot], wsem.at[slot]),
                pltpu.make_async_copy(wd_ref.at[e], wdb.at[slot], wsem.at[slot]))

    def start_group(grp, slot):
        for g in range(ROW_GROUP):
            blk = jnp.minimum(grp * ROW_GROUP + g, nb - 1)
            cnt = cnt_ref[blk]
            last = cnt - 1
            base = blk * tm
            n_chunks = lax.div(cnt + (ROW_CHUNK - 1), ROW_CHUNK)

            def issue(ch, c, last=last, base=base, g=g):
                for j in range(ROW_CHUNK):
                    r = ch * ROW_CHUNK + j
                    t = tok_ref[base + jnp.minimum(r, last)]
                    pltpu.make_async_copy(h1p_ref.at[pl.ds(t * SUBLANES, SUBLANES)],
                                          xbuf.at[slot, pl.ds((g * tm + r) * SUBLANES, SUBLANES)],
                                          gsem.at[slot]).start()
                return c

            lax.fori_loop(0, n_chunks, issue, 0)
            done = n_chunks * ROW_CHUNK
            pad = tm - done
            size = tm // 2
            while size >= ROW_CHUNK:
                @pl.when((pad & size) != 0)
                def _(done=done, size=size, g=g):
                    pltpu.make_async_copy(
                        h1p_ref.at[pl.ds(0, size * SUBLANES)],
                        xbuf.at[slot, pl.ds((g * tm + done) * SUBLANES, size * SUBLANES)],
                        gsem.at[slot]).start()
                done = done + jnp.where((pad & size) != 0, size, 0)
                size //= 2

    @pl.when(i < nb)
    def _():
        e = be_ref[i]
        wslot = slot_ref[i]
        grp = lax.div(i, ROW_GROUP)
        sub = lax.rem(i, ROW_GROUP)
        gslot = lax.rem(grp, 2)

        @pl.when(i == 0)
        def _():
            for cp in weight_copies(e, wslot):
                cp.start(priority=WEIGHT_DMA_PRIORITY)
            nxt = nxt_ref[0]

            @pl.when(nxt >= 0)
            def _():
                for cp in weight_copies(nxt, 1 - wslot):
                    cp.start(priority=WEIGHT_DMA_PRIORITY)

            start_group(0, 0)

        @pl.when((i == 0) | (e != be_ref[jnp.maximum(i - 1, 0)]))
        def _():
            for cp in weight_copies(e, wslot):
                cp.wait()
            wg_s[...] = wgb[wslot].astype(BF16)
            wu_s[...] = wub[wslot].astype(BF16)
            wd_s[...] = wdb[wslot].astype(BF16)
            nxt2 = nxt2_ref[i]

            @pl.when(nxt2 >= 0)
            def _():
                for cp in weight_copies(nxt2, wslot):
                    cp.start(priority=WEIGHT_DMA_PRIORITY)

        @pl.when(sub == 0)
        def _():
            pltpu.make_async_copy(h1p_ref.at[pl.ds(0, ROW_GROUP * tm * SUBLANES)], xbuf.at[gslot],
                                  gsem.at[gslot]).wait()

            @pl.when((grp + 1) * ROW_GROUP < nb)
            def _():
                start_group(grp + 1, 1 - gslot)

        row0 = pl.multiple_of(sub * (tm * SUBLANES), tm * SUBLANES)
        x = _unpack_bf16_pair(_load_row_tiles(xbuf.at[gslot, pl.ds(row0, tm * SUBLANES)], tm))
        a = jnp.dot(x, wg_s[...], preferred_element_type=F32)
        u = jnp.dot(x, wu_s[...], preferred_element_type=F32)
        hid = (jax.nn.silu(a) * u).astype(BF16)
        y = jnp.dot(hid, wd_s[...], preferred_element_type=F32)
        _store_row_tiles(ys_ref, _pack_bf16_pair(y))


def grouped_experts(h1p_tiles, row_tok, block_expert, block_next, block_next2, block_slot, block_count,
                    n_blocks_used, w_gate, w_up, w_down, tm=MOE_BLOCK):
    n_rows = row_tok.shape[0]
    n_e, d, ff = w_gate.shape
    n_blocks = n_rows // tm
    rowmap = lambda i, be, nx, nx2, sl, cn, tok, nb: (jnp.minimum(i, jnp.maximum(nb[0] - 1, 0)), 0)
    hbm = pl.BlockSpec(memory_space=pl.ANY)
    return pl.pallas_call(
        _experts_kernel,
        grid_spec=pltpu.PrefetchScalarGridSpec(
            num_scalar_prefetch=7,
            grid=(n_blocks,),
            in_specs=[hbm, hbm, hbm, hbm],
            out_specs=pl.BlockSpec((tm * SUBLANES, LANES), rowmap),
            scratch_shapes=[pltpu.VMEM((2, ROW_GROUP * tm * SUBLANES, LANES), U32),
                            pltpu.VMEM((2, d, ff), F32), pltpu.VMEM((2, d, ff), F32), pltpu.VMEM((2, ff, d), F32),
                            pltpu.VMEM((d, ff), BF16), pltpu.VMEM((d, ff), BF16), pltpu.VMEM((ff, d), BF16),
                            pltpu.SemaphoreType.DMA((2,)), pltpu.SemaphoreType.DMA((2,))],
        ),
        out_shape=jax.ShapeDtypeStruct((n_rows * SUBLANES, LANES), U32),
        compiler_params=_cparams(("arbitrary",)),
        name="grouped_experts",
    )(block_expert, block_next, block_next2, block_slot, block_count, row_tok, n_blocks_used,
      h1p_tiles, w_gate, w_up, w_down)


def _combine_kernel(dest_ref, h1_ref, h1p_ref, gate_ref, wsg_ref, wsu_ref, wsd_ref, g2_ref, b2_ref, ys_ref,
                    o_ref, buf_s, sem):
    tm = h1_ref.shape[0]
    i = pl.program_id(0)
    n_steps = pl.num_programs(0)
    n_tok = n_steps * tm
    slot = lax.rem(i, 2)

    def start_rows(step, sl):
        def issue(r, c):
            for k in range(TOP_K):
                row = dest_ref[k * n_tok + step * tm + r]
                pltpu.make_async_copy(ys_ref.at[pl.ds(row * SUBLANES, SUBLANES)],
                                      buf_s.at[sl, k, pl.ds(r * SUBLANES, SUBLANES)], sem.at[sl]).start()
            return c

        lax.fori_loop(0, tm, issue, 0, unroll=2)

    @pl.when(i == 0)
    def _():
        start_rows(0, 0)

    @pl.when(i + 1 < n_steps)
    def _():
        start_rows(i + 1, 1 - slot)

    xb = _unpack_bf16_pair(_load_row_tiles(h1p_ref, tm))
    a = jnp.dot(xb, wsg_ref[...], preferred_element_type=F32)
    u = jnp.dot(xb, wsu_ref[...], preferred_element_type=F32)
    shared = jnp.dot((jax.nn.silu(a) * u).astype(BF16), wsd_ref[...], preferred_element_type=F32)

    for k in range(TOP_K):
        pltpu.make_async_copy(ys_ref.at[pl.ds(0, tm * SUBLANES)], buf_s.at[slot, k], sem.at[slot]).wait()

    gate = gate_ref[...]
    routed = jnp.zeros_like(shared)
    for k in range(TOP_K):
        routed = routed + _unpack_pair_f32(_load_row_tiles(buf_s.at[slot, k], tm)) * gate[:, k:k + 1]
    o_ref[...] = _layer_norm(DN_ALPHA * h1_ref[...] + routed + shared, g2_ref[...], b2_ref[...])


def combine_shared_ln(dest_flat, h1, h1p_tiles, gate_nk, ys_tiles, wsg, wsu, wsd, g2, b2, tm=256):
    n, d = h1.shape
    ff = wsg.shape[1]
    row = lambda i, dest: (i, 0)
    const = lambda i, dest: (0, 0)
    return pl.pallas_call(
        _combine_kernel,
        grid_spec=pltpu.PrefetchScalarGridSpec(
            num_scalar_prefetch=1,
            grid=(n // tm,),
            in_specs=[pl.BlockSpec((tm, d), row), pl.BlockSpec((tm * SUBLANES, LANES), row),
                      pl.BlockSpec((tm, TOP_K), row),
                      pl.BlockSpec((d, ff), const), pl.BlockSpec((d, ff), const), pl.BlockSpec((ff, d), const),
                      pl.BlockSpec((1, d), const), pl.BlockSpec((1, d), const),
                      pl.BlockSpec(memory_space=pl.ANY)],
            out_specs=pl.BlockSpec((tm, d), row),
            scratch_shapes=[pltpu.VMEM((2, TOP_K, tm * SUBLANES, LANES), U32), pltpu.SemaphoreType.DMA((2,))],
        ),
        out_shape=jax.ShapeDtypeStruct((n, d), F32),
        compiler_params=_cparams(("arbitrary",)),
        name="combine_shared_ln",
    )(dest_flat, h1, h1p_tiles, gate_nk, wsg, wsu, wsd, g2.reshape(1, d), b2.reshape(1, d), ys_tiles)


def kernel(x, ln_emb_g, ln_emb_b, w_in, lb_logits, hg_norm_w, conv_w, conv_b, rg_wa, rg_ba, rg_wx, rg_bx,
           rg_lam, w_out, ln1_g, ln1_b, w_router, router_bias, w_gate, w_up, w_down, ws_gate, ws_up, ws_down,
           ln2_g, ln2_b):
    batch, seq, d = x.shape
    assert d == 2 * SUBLANES * LANES, "packed rows must fill one 8x128 tile"
    n_tok = batch * seq
    layer = 0
    x2 = x.reshape(n_tok, d)
    lb = jnp.cumsum(jax.nn.softmax(lb_logits.astype(F32), axis=1), axis=1)

    proj = ln_inproj(x2, ln_emb_g, ln_emb_b, w_in[layer].astype(BF16))
    y_hg = hgrn2_mixer(proj, lb[0, layer], lb[1, layer], hg_norm_w[layer], batch, seq)
    y_rg = rglru_mixer(proj, conv_w[layer], conv_b[layer], rg_wa[layer], rg_ba[layer], rg_wx[layer],
                       rg_bx[layer], rg_lam[layer], batch, seq)
    h1, h1p_tiles, logits_t = outproj_ln_router(
        y_hg, y_rg, x2, ln_emb_g, ln_emb_b, w_out[layer].astype(BF16), ln1_g[layer], ln1_b[layer],
        w_router[layer].T.astype(BF16))

    idx_t, gate_t = route_topk(logits_t, router_bias[layer])
    rank_t, counts = expert_ranks(idx_t)
    counts = counts[:, 0]

    n_assign = n_tok * TOP_K
    n_blocks = -(-(n_assign + N_EXPERTS * (MOE_BLOCK - 1)) // MOE_BLOCK)
    n_rows = n_blocks * MOE_BLOCK
    padded = (counts + MOE_BLOCK - 1) // MOE_BLOCK * MOE_BLOCK
    padded_end = jnp.cumsum(padded)
    padded_start = padded_end - padded
    n_used_blocks = (padded_end[-1:] // MOE_BLOCK).astype(I32)
    blk_row = jnp.minimum(jnp.arange(n_blocks, dtype=I32), n_used_blocks[0] - 1) * MOE_BLOCK
    block_expert = jnp.minimum(
        jnp.sum((padded_end[None, :] <= blk_row[:, None]).astype(I32), axis=1), N_EXPERTS - 1)
    e_ids = jnp.arange(N_EXPERTS, dtype=I32)
    active = counts > 0
    later = lax.cummin(jnp.where(active, e_ids, N_EXPERTS)[::-1])[::-1]
    nxt = jnp.concatenate([later[1:], jnp.full((1,), N_EXPERTS, I32)])
    next1 = jnp.where(nxt < N_EXPERTS, nxt, -1).astype(I32)
    hot1 = (next1[:, None] == e_ids[None, :]).astype(I32)
    next2 = jnp.sum(hot1 * (next1 + 1)[None, :], axis=1) - 1
    ordinal = jnp.cumsum(active.astype(I32)) - 1
    onehot = (block_expert[:, None] == e_ids[None, :]).astype(I32)
    block_next = jnp.sum(onehot * next1[None, :], axis=1)
    block_next2 = jnp.sum(onehot * next2[None, :], axis=1)
    block_slot = jnp.sum(onehot * (ordinal % 2)[None, :], axis=1)
    block_count = jnp.clip(jnp.sum(onehot * (counts + padded_start)[None, :], axis=1) - blk_row, 1, MOE_BLOCK)

    dest_flat = assignment_rows(idx_t, rank_t, padded_start).reshape(-1)
    row_tok = row_tokens(dest_flat, n_rows)

    ys_tiles = grouped_experts(h1p_tiles, row_tok, block_expert, block_next, block_next2, block_slot,
                               block_count, n_used_blocks, w_gate[layer], w_up[layer], w_down[layer])
    out = combine_shared_ln(dest_flat, h1, h1p_tiles, gate_t.T, ys_tiles,
                            ws_gate[layer].astype(BF16), ws_up[layer].astype(BF16),
                            ws_down[layer].astype(BF16), ln2_g[layer], ln2_b[layer])
    return out.reshape(batch, seq, d)
```

```python
import functools

import numpy as np
import jax
import jax.numpy as jnp
from jax import lax
from jax.experimental import pallas as pl
from jax.experimental.pallas import tpu as pltpu

F32 = jnp.float32
BF16 = jnp.bfloat16
U32 = jnp.uint32
I32 = jnp.int32

D_MODEL = 2048
DEPTH = 1
HG_KEY_DIM = 128
HG_VAL_DIM = 128
HG_WIDTH = D_MODEL // 2
HG_HEADS = HG_WIDTH // HG_VAL_DIM
HG_KEY_WIDTH = HG_HEADS * HG_KEY_DIM
RG_WIDTH = D_MODEL - HG_WIDTH
RG_BLOCKS = 8
RG_BLOCK_DIM = RG_WIDTH // RG_BLOCKS
RG_CONV = 4
RG_C = 8.0
MIX_WIDTH = HG_WIDTH + RG_WIDTH
IN_WIDTHS = (HG_KEY_WIDTH, HG_KEY_WIDTH, HG_KEY_WIDTH, HG_WIDTH, HG_WIDTH, RG_WIDTH, RG_WIDTH)
IN_WIDTH = sum(IN_WIDTHS)
N_EXPERTS = 256
TOP_K = 8
N_GROUPS = 8
TOPK_GROUPS = 4
GROUP_SIZE = N_EXPERTS // N_GROUPS
EXPERT_FF = 512
SHARED_FF = 512
ROUTE_SCALE = 2.5
MOE_BLOCK = 128
DN_ALPHA = (2 * DEPTH) ** 0.25
LN_EPS = 1e-5
RMS_EPS = 1e-6

LANES = 128
SUBLANES = 8
VMEM_LIMIT_BYTES = 56 * 1024 * 1024

HG_CHUNK = 128
HG_SUB = 8
HG_LEVELS = (16, 32, 64, 128)


def _cparams(sem, vmem=VMEM_LIMIT_BYTES):
    return pltpu.CompilerParams(dimension_semantics=sem, vmem_limit_bytes=vmem)


def _layer_norm(x, g, b):
    mu = jnp.mean(x, axis=-1, keepdims=True)
    xc = x - mu
    var = jnp.mean(xc * xc, axis=-1, keepdims=True)
    return xc * lax.rsqrt(var + LN_EPS) * g + b


def _pack_bf16_pair(x):
    n = x.shape[1] // 2
    bits = lax.bitcast_convert_type(x.astype(BF16).astype(F32), U32)
    return (bits[:, :n] >> 16) | (bits[:, n:] & jnp.uint32(0xFFFF0000))


def _unpack_pair_f32(w):
    lo = lax.bitcast_convert_type(w << 16, F32)
    hi = lax.bitcast_convert_type(w & jnp.uint32(0xFFFF0000), F32)
    return jnp.concatenate([lo, hi], axis=1)


def _unpack_bf16_pair(w):
    return _unpack_pair_f32(w).astype(BF16)


def _store_row_tiles(ref, packed):
    m = packed.shape[0]
    for s in range(SUBLANES):
        ref[pl.ds(s, m, stride=SUBLANES), :] = packed[:, s * LANES:(s + 1) * LANES]


def _load_row_tiles(ref, m):
    return jnp.concatenate([ref[pl.ds(s, m, stride=SUBLANES), :] for s in range(SUBLANES)], axis=1)


def _ln_inproj_kernel(x_ref, g_ref, b_ref, w_ref, o_ref, hb_ref):
    @pl.when(pl.program_id(1) == 0)
    def _():
        hb_ref[...] = _layer_norm(x_ref[...], g_ref[...], b_ref[...]).astype(BF16)

    o_ref[...] = jnp.dot(hb_ref[...], w_ref[...], preferred_element_type=F32).astype(o_ref.dtype)


def ln_inproj(x2, g, b, w_bf16, tm=1024, tn=1024):
    n, d = x2.shape
    width = w_bf16.shape[1]
    return pl.pallas_call(
        _ln_inproj_kernel,
        grid=(n // tm, width // tn),
        in_specs=[
            pl.BlockSpec((tm, d), lambda i, j: (i, 0)),
            pl.BlockSpec((1, d), lambda i, j: (0, 0)),
            pl.BlockSpec((1, d), lambda i, j: (0, 0)),
            pl.BlockSpec((d, tn), lambda i, j: (0, j)),
        ],
        out_specs=pl.BlockSpec((tm, tn), lambda i, j: (i, j)),
        out_shape=jax.ShapeDtypeStruct((n, width), BF16),
        scratch_shapes=[pltpu.VMEM((tm, d), BF16)],
        compiler_params=_cparams(("parallel", "arbitrary")),
        name="ln_inproj",
    )(x2, g.reshape(1, d), b.reshape(1, d), w_bf16)


def _hgrn2_constants():
    c = HG_CHUNK
    t = np.arange(c)[:, None]
    u = np.arange(c)[None, :]
    fwd = [(u <= t), (u > t)]
    bwd = [(u >= t), (u < t)]
    for m in HG_LEVELS:
        start = (t // m) * m
        mid = start + m // 2 - 1
        second = t > mid
        fwd.append(np.where(second, (u > mid) & (u <= t), (u > t) & (u <= mid)))
        bwd.append(np.where(second, (u > mid) & (u < t), (u >= t) & (u <= mid)))
    fwd = np.concatenate(fwd, axis=0).astype(np.float32)
    bwd = np.concatenate(bwd, axis=0).astype(np.float32)
    fwd = np.concatenate([fwd, fwd], axis=1)
    bwd = np.concatenate([bwd, bwd], axis=1)
    lvl = np.full((c, c), -1, np.int32)
    tt = np.arange(c)[:, None]
    ss = np.arange(c)[None, :]
    for li in reversed(range(len(HG_LEVELS))):
        m = HG_LEVELS[li]
        lvl = np.where((tt // m) == (ss // m), li, lvl)
    lvl = np.where((tt // HG_SUB) == (ss // HG_SUB), -1, lvl).astype(np.int32)
    return fwd, bwd, lvl


def _hgrn2_kernel(q_ref, zf_ref, zb_ref, v_ref, g_ref, lbf_ref, lbb_ref, nw_ref,
                  mf_ref, mb_ref, lvl_ref, o_ref,
                  qs_s, v_s, lf_s, kf_s, lb_s, kb_s, cf_s, cb_s,
                  qif_s, kif_s, qib_s, kib_s, df_s, db_s, acc_s, stf_s, stb_s):
    seq = q_ref.shape[0]
    c = HG_CHUNK
    n_chunks = seq // c
    n_lvl = len(HG_LEVELS)

    qs_s[...] = jax.nn.silu(q_ref[...].astype(F32))
    v_s[...] = v_ref[...].astype(F32)
    for z_ref, lb_ref, lf_out, k_out in ((zf_ref, lbf_ref, lf_s, kf_s), (zb_ref, lbb_ref, lb_s, kb_s)):
        z = z_ref[...].astype(F32)
        lb = lb_ref[...]
        sig = jax.nn.sigmoid(z)
        lf_out[...] = jnp.log(lb + (1.0 - lb) * sig)
        k_out[...] = (1.0 - lb) * (1.0 - sig)

    ones_mat = jnp.ones((LANES, LANES), BF16)
    lvl = lvl_ref[...]
    row_c = lax.broadcasted_iota(I32, (c, LANES), 0)
    lane8 = lax.broadcasted_iota(I32, (HG_SUB, LANES), 1)
    sub8 = lax.broadcasted_iota(I32, (HG_SUB, LANES), 0)
    n_stack = (2 + n_lvl) * c

    def intra(n, carry):
        r0 = pl.multiple_of(n * c, c)
        rows = pl.ds(r0, c)
        qs = qs_s[rows, :]
        kf = kf_s[rows, :]
        kb = kb_s[rows, :]
        hl = []
        for lfs in (lf_s, lb_s):
            lf = lfs[rows, :]
            hi = lf.astype(BF16)
            hl.append(jnp.concatenate([hi, (lf - hi.astype(F32)).astype(BF16)], axis=0))
        ef = jnp.dot(mf_ref[...], hl[0], preferred_element_type=F32)
        eb = jnp.dot(mb_ref[...], hl[1], preferred_element_type=F32)
        for (e, kk, cs, qi_s, ki_s, d_s, tot_row) in ((ef, kf, cf_s, qif_s, kif_s, df_s, c - 1),
                                                    (eb, kb, cb_s, qib_s, kib_s, db_s, 0)):
            cum = e[0:c]
            cs[rows, :] = cum
            qi_s[rows, :] = (qs * jnp.exp(cum)).astype(BF16)
            ki_s[rows, :] = (kk * jnp.exp(e[c:2 * c])).astype(BF16)
            d_s[pl.ds(n, 1), :] = jnp.exp(cum[tot_row:tot_row + 1, :])

        a_mat = jnp.zeros((c, c), F32)
        for li, m in enumerate(HG_LEVELS):
            second = (row_c & (m // 2)) != 0
            wf = jnp.exp(ef[(2 + li) * c:(3 + li) * c])
            wb = jnp.exp(eb[(2 + li) * c:(3 + li) * c])
            qcat = jnp.concatenate([qs * jnp.where(second, wf, 0.0), qs * jnp.where(second, 0.0, wb)], axis=1)
            kcat = jnp.concatenate([kf * jnp.where(second, 0.0, wf), kb * jnp.where(second, wb, 0.0)], axis=1)
            sc = lax.dot_general(qcat.astype(BF16), kcat.astype(BF16),
                                 (((1,), (1,)), ((), ())), preferred_element_type=F32)
            a_mat = jnp.where(lvl == li, sc, a_mat)

        p_list = []
        for j in range(c // HG_SUB):
            rj = r0 + j * HG_SUB
            sl = pl.ds(rj, HG_SUB)
            cf = cf_s[sl, :]
            cb = cb_s[sl, :]
            qj = qs_s[sl, :]
            for s in range(HG_SUB):
                src = pl.ds(rj + s, 1)
                df = jnp.where(sub8 >= s, cf - cf_s[src, :], -jnp.inf)
                db = jnp.where(sub8 <= s, cb - cb_s[src, :], -jnp.inf)
                p_list.append(qj * (kf_s[src, :] * jnp.exp(df) + kb_s[src, :] * jnp.exp(db)))
        p_all = jnp.concatenate(p_list, axis=0).astype(BF16)
        red = jnp.dot(p_all, ones_mat, preferred_element_type=F32)
        diag_rows = []
        for j in range(c // HG_SUB):
            blk = jnp.zeros((HG_SUB, LANES), F32)
            for s in range(HG_SUB):
                idx = (j * HG_SUB + s) * HG_SUB
                blk = jnp.where(lane8 == j * HG_SUB + s, red[idx:idx + HG_SUB], blk)
            diag_rows.append(blk)
        a_mat = a_mat + jnp.concatenate(diag_rows, axis=0)
        acc_s[rows, :] = jnp.dot(a_mat.astype(BF16), v_s[rows, :].astype(BF16), preferred_element_type=F32)
        return carry

    lax.fori_loop(0, n_chunks, intra, 0, unroll=4)

    stf_s[...] = jnp.zeros_like(stf_s)
    stb_s[...] = jnp.zeros_like(stb_s)

    def inter(n, carry):
        for (st_s, qi_s, ki_s, d_s, idx) in ((stf_s, qif_s, kif_s, df_s, n),
                                             (stb_s, qib_s, kib_s, db_s, n_chunks - 1 - n)):
            rows = pl.ds(pl.multiple_of(idx * c, c), c)
            st = st_s[...]
            acc_s[rows, :] += lax.dot_general(qi_s[rows, :], st.astype(BF16),
                                              (((1,), (1,)), ((), ())), preferred_element_type=F32)
            upd = lax.dot_general(v_s[rows, :].astype(BF16), ki_s[rows, :],
                                  (((0,), (0,)), ((), ())), preferred_element_type=F32)
            st_s[...] = st * d_s[pl.ds(idx, 1), :] + upd
        return carry

    lax.fori_loop(0, n_chunks, inter, 0, unroll=4)

    o = acc_s[...]
    o = o * lax.rsqrt(jnp.mean(o * o, axis=-1, keepdims=True) + RMS_EPS) * nw_ref[...]
    o_ref[...] = (o * jax.nn.silu(g_ref[...].astype(F32))).astype(o_ref.dtype)


def hgrn2_mixer(proj, lb_f, lb_b, norm_w, batch, seq):
    n = proj.shape[0]
    heads = HG_HEADS
    c = HG_CHUNK
    n_chunks = seq // c
    mf, mb, lvl = _hgrn2_constants()
    n_stack = mf.shape[0]

    def col(off):
        return pl.BlockSpec((seq, LANES), lambda b, h, off=off: (b, off + h))

    def per_head(arr):
        return pl.BlockSpec((1, LANES), lambda b, h: (0, h))

    const2 = lambda b, h: (0, 0)
    seq_f32 = pltpu.VMEM((seq, LANES), F32)
    seq_bf16 = pltpu.VMEM((seq, LANES), BF16)
    dec = pltpu.VMEM((n_chunks, LANES), F32)
    state = pltpu.VMEM((LANES, LANES), F32)
    return pl.pallas_call(
        _hgrn2_kernel,
        grid=(batch, heads),
        in_specs=[col(0), col(heads), col(2 * heads), col(3 * heads), col(4 * heads),
                  per_head(lb_f), per_head(lb_b),
                  pl.BlockSpec((1, LANES), const2),
                  pl.BlockSpec((n_stack, 2 * c), const2),
                  pl.BlockSpec((n_stack, 2 * c), const2),
                  pl.BlockSpec((c, c), const2)],
        out_specs=pl.BlockSpec((seq, LANES), lambda b, h: (b, h)),
        out_shape=jax.ShapeDtypeStruct((n, HG_WIDTH), BF16),
        scratch_shapes=[seq_f32] * 8 + [seq_bf16] * 4 + [dec, dec, seq_f32, state, state],
        compiler_params=_cparams(("parallel", "parallel")),
        name="hgrn2_mixer",
    )(proj, proj, proj, proj, proj, lb_f.reshape(1, -1), lb_b.reshape(1, -1), norm_w.reshape(1, -1),
      jnp.asarray(mf, BF16), jnp.asarray(mb, BF16), jnp.asarray(lvl))


def _shift_rows(x, d, fill, reverse):
    pad = jnp.full((d, x.shape[1]), fill, x.dtype)
    if reverse:
        return jnp.concatenate([x[d:], pad], axis=0)
    return jnp.concatenate([pad, x[:-d]], axis=0)


def _linear_scan(a, u, reverse):
    seq = a.shape[0]
    d = 1
    while d < seq:
        u = u + a * _shift_rows(u, d, 0.0, reverse)
        if d * 2 < seq:
            a = a * _shift_rows(a, d, 1.0, reverse)
        d *= 2
    return u


def _rglru_kernel(xr_ref, gr_ref, cw_ref, cb_ref, wa_ref, ba_ref, wx_ref, bx_ref, lam_ref, o_ref, xp_s):
    seq = xr_ref.shape[0]
    pad = SUBLANES
    pad_l = RG_CONV // 2
    xp_s[0:pad, :] = jnp.zeros((pad, LANES), F32)
    xp_s[pad + seq:, :] = jnp.zeros((pad, LANES), F32)
    xp_s[pad:pad + seq, :] = xr_ref[...].astype(F32)
    xc = cb_ref[...]
    for j in range(RG_CONV):
        off = pad + j - pad_l
        xc = xc + xp_s[off:off + seq, :] * cw_ref[j:j + 1, :]
    xcb = xc.astype(BF16)
    h = None
    for d in range(2):
        r = jax.nn.sigmoid(jnp.dot(xcb, wa_ref[d, 0], preferred_element_type=F32) + ba_ref[d:d + 1, :])
        i = jax.nn.sigmoid(jnp.dot(xcb, wx_ref[d, 0], preferred_element_type=F32) + bx_ref[d:d + 1, :])
        log_a = (-RG_C) * r * jax.nn.softplus(-lam_ref[d:d + 1, :])
        a = jnp.exp(log_a)
        u = jnp.sqrt(-jnp.tanh(log_a) * (a * a + 1.0)) * (i * xc)
        hd = _linear_scan(a, u, reverse=(d == 1))
        h = hd if h is None else h + hd
    o_ref[...] = (h * jax.nn.gelu(gr_ref[...].astype(F32))).astype(o_ref.dtype)


def rglru_mixer(proj, conv_w, conv_b, wa, ba, wx, bx, lam, batch, seq):
    n = proj.shape[0]
    x_off = (3 * HG_KEY_WIDTH + 2 * HG_WIDTH) // LANES
    g_off = x_off + RG_WIDTH // LANES
    blk = lambda b, j: (0, j)
    return pl.pallas_call(
        _rglru_kernel,
        grid=(batch, RG_BLOCKS),
        in_specs=[
            pl.BlockSpec((seq, LANES), lambda b, j: (b, x_off + j)),
            pl.BlockSpec((seq, LANES), lambda b, j: (b, g_off + j)),
            pl.BlockSpec((RG_CONV, LANES), blk),
            pl.BlockSpec((1, LANES), blk),
            pl.BlockSpec((2, 1, RG_BLOCK_DIM, RG_BLOCK_DIM), lambda b, j: (0, j, 0, 0)),
            pl.BlockSpec((2, LANES), blk),
            pl.BlockSpec((2, 1, RG_BLOCK_DIM, RG_BLOCK_DIM), lambda b, j: (0, j, 0, 0)),
            pl.BlockSpec((2, LANES), blk),
            pl.BlockSpec((2, LANES), blk),
        ],
        out_specs=pl.BlockSpec((seq, LANES), lambda b, j: (b, j)),
        out_shape=jax.ShapeDtypeStruct((n, RG_WIDTH), BF16),
        scratch_shapes=[pltpu.VMEM((seq + 2 * SUBLANES, LANES), F32)],
        compiler_params=_cparams(("parallel", "parallel")),
        name="rglru_mixer",
    )(proj, proj, conv_w, conv_b.reshape(1, -1), wa.astype(BF16), ba, wx.astype(BF16), bx, lam)


def _outproj_kernel(yh_ref, yr_ref, x_ref, g0_ref, b0_ref, wo_ref, g1_ref, b1_ref, wr_ref,
                    h1_ref, h1p_ref, lg_ref):
    half = yh_ref.shape[1]
    h0 = _layer_norm(x_ref[...], g0_ref[...], b0_ref[...])
    mix = jnp.dot(yh_ref[...], wo_ref[0:half, :], preferred_element_type=F32)
    mix = mix + jnp.dot(yr_ref[...], wo_ref[half:, :], preferred_element_type=F32)
    h1 = _layer_norm(DN_ALPHA * h0 + mix, g1_ref[...], b1_ref[...])
    h1_ref[...] = h1
    _store_row_tiles(h1p_ref, _pack_bf16_pair(h1))
    lg_ref[...] = lax.dot_general(wr_ref[...], h1.astype(BF16), (((1,), (1,)), ((), ())),
                                  preferred_element_type=F32)


def outproj_ln_router(y_hg, y_rg, x2, g0, b0, w_out_bf16, g1, b1, w_router_t_bf16, tm=256):
    n, d = x2.shape
    half = y_hg.shape[1]
    n_e = w_router_t_bf16.shape[0]
    row = lambda i: (i, 0)
    const = lambda i: (0, 0)
    vec = pl.BlockSpec((1, d), const)
    return pl.pallas_call(
        _outproj_kernel,
        grid=(n // tm,),
        in_specs=[pl.BlockSpec((tm, half), row), pl.BlockSpec((tm, half), row), pl.BlockSpec((tm, d), row),
                  vec, vec, pl.BlockSpec((2 * half, d), const), vec, vec,
                  pl.BlockSpec((n_e, d), const)],
        out_specs=[pl.BlockSpec((tm, d), row), pl.BlockSpec((tm * SUBLANES, LANES), row),
                   pl.BlockSpec((n_e, tm), lambda i: (0, i))],
        out_shape=[jax.ShapeDtypeStruct((n, d), F32), jax.ShapeDtypeStruct((n * SUBLANES, LANES), U32),
                   jax.ShapeDtypeStruct((n_e, n), F32)],
        compiler_params=_cparams(("parallel",)),
        name="outproj_ln_router",
    )(y_hg, y_rg, x2, g0.reshape(1, d), b0.reshape(1, d), w_out_bf16, g1.reshape(1, d), b1.reshape(1, d),
      w_router_t_bf16)


def _first_argmax(vals, iota, n):
    m = jnp.max(vals, axis=0, keepdims=True)
    first = jnp.min(jnp.where(vals == m, iota, n), axis=0, keepdims=True)
    return m, first


def _route_kernel(lg_ref, bias_ref, idx_ref, gate_ref):
    n_e, tn = lg_ref.shape
    scores = jax.nn.sigmoid(lg_ref[...])
    choice = scores + bias_ref[...]
    neg = -jnp.inf
    gi = lax.broadcasted_iota(I32, (GROUP_SIZE, tn), 0)
    gs_rows = []
    for g in range(N_GROUPS):
        blk = choice[g * GROUP_SIZE:(g + 1) * GROUP_SIZE]
        m1, f1 = _first_argmax(blk, gi, GROUP_SIZE)
        m2 = jnp.max(jnp.where(gi == f1, neg, blk), axis=0, keepdims=True)
        gs_rows.append(m1 + m2)
    gs = jnp.concatenate(gs_rows, axis=0)
    ri = lax.broadcasted_iota(I32, (N_GROUPS, tn), 0)
    gmask = jnp.zeros((N_GROUPS, tn), jnp.bool_)
    for _ in range(TOPK_GROUPS):
        _, f = _first_argmax(gs, ri, N_GROUPS)
        hit = ri == f
        gmask = gmask | hit
        gs = jnp.where(hit, neg, gs)
    gm = jnp.where(gmask, 1.0, 0.0)
    emask = jnp.concatenate(
        [jnp.broadcast_to(gm[g:g + 1], (GROUP_SIZE, tn)) for g in range(N_GROUPS)], axis=0)
    masked = jnp.where(emask > 0.5, choice, neg)
    ei = lax.broadcasted_iota(I32, (n_e, tn), 0)
    idx_rows, gate_rows = [], []
    for _ in range(TOP_K):
        _, f = _first_argmax(masked, ei, n_e)
        hit = ei == f
        idx_rows.append(f)
        gate_rows.append(jnp.sum(jnp.where(hit, scores, 0.0), axis=0, keepdims=True))
        masked = jnp.where(hit, neg, masked)
    gate = jnp.concatenate(gate_rows, axis=0)
    gate = gate / jnp.sum(gate, axis=0, keepdims=True) * ROUTE_SCALE
    idx_ref[...] = jnp.concatenate(idx_rows, axis=0)
    gate_ref[...] = gate


def route_topk(logits_t, bias, tn=512):
    n_e, n = logits_t.shape
    return pl.pallas_call(
        _route_kernel,
        grid=(n // tn,),
        in_specs=[pl.BlockSpec((n_e, tn), lambda i: (0, i)), pl.BlockSpec((n_e, 1), lambda i: (0, 0))],
        out_specs=[pl.BlockSpec((TOP_K, tn), lambda i: (0, i)), pl.BlockSpec((TOP_K, tn), lambda i: (0, i))],
        out_shape=[jax.ShapeDtypeStruct((TOP_K, n), I32), jax.ShapeDtypeStruct((TOP_K, n), F32)],
        compiler_params=_cparams(("parallel",)),
        name="route_topk",
    )(logits_t, bias.reshape(n_e, 1))


def _rank_kernel(idx_ref, su_ref, rank_ref, cnt_ref, carry_s):
    n_e = cnt_ref.shape[0]
    tn = idx_ref.shape[1]

    @pl.when(pl.program_id(0) == 0)
    def _():
        carry_s[...] = jnp.zeros_like(carry_s)

    idx = idx_ref[...]
    ei = lax.broadcasted_iota(I32, (n_e, tn), 0)
    hits = [ei == idx[k:k + 1] for k in range(TOP_K)]
    member = jnp.zeros((n_e, tn), F32)
    for h in hits:
        member = member + jnp.where(h, 1.0, 0.0)
    prefix = jnp.dot(member.astype(BF16), su_ref[...], preferred_element_type=F32) + carry_s[:, 0:1]
    rank_ref[...] = jnp.concatenate(
        [jnp.sum(jnp.where(h, prefix, 0.0), axis=0, keepdims=True) for h in hits], axis=0).astype(I32)
    carry_s[...] = carry_s[...] + jnp.sum(member, axis=1, keepdims=True)
    cnt_ref[...] = carry_s[...].astype(I32)


def expert_ranks(idx_t, tn=512):
    k, n = idx_t.shape
    su = jnp.asarray(np.triu(np.ones((tn, tn), np.float32), 1), BF16)
    return pl.pallas_call(
        _rank_kernel,
        grid=(n // tn,),
        in_specs=[pl.BlockSpec((k, tn), lambda i: (0, i)), pl.BlockSpec((tn, tn), lambda i: (0, 0))],
        out_specs=[pl.BlockSpec((k, tn), lambda i: (0, i)), pl.BlockSpec((N_EXPERTS, LANES), lambda i: (0, 0))],
        out_shape=[jax.ShapeDtypeStruct((k, n), I32), jax.ShapeDtypeStruct((N_EXPERTS, LANES), I32)],
        scratch_shapes=[pltpu.VMEM((N_EXPERTS, LANES), F32)],
        compiler_params=_cparams(("arbitrary",)),
        name="expert_ranks",
    )(idx_t, su)


def _dest_kernel(idx_ref, rank_ref, start_ref, dest_ref):
    n_e = start_ref.shape[0]
    tn = idx_ref.shape[1]
    idx = idx_ref[...]
    ei = lax.broadcasted_iota(I32, (n_e, tn), 0)
    start = start_ref[...]
    base = jnp.concatenate(
        [jnp.sum(jnp.where(ei == idx[k:k + 1], start, 0.0), axis=0, keepdims=True) for k in range(TOP_K)],
        axis=0)
    dest_ref[...] = base.astype(I32) + rank_ref[...]


def assignment_rows(idx_t, rank_t, padded_start, tn=512):
    k, n = idx_t.shape
    blk = pl.BlockSpec((k, tn), lambda i: (0, i))
    return pl.pallas_call(
        _dest_kernel,
        grid=(n // tn,),
        in_specs=[blk, blk, pl.BlockSpec((N_EXPERTS, 1), lambda i: (0, 0))],
        out_specs=blk,
        out_shape=jax.ShapeDtypeStruct((k, n), I32),
        compiler_params=_cparams(("parallel",)),
        name="assignment_rows",
    )(idx_t, rank_t, padded_start.astype(F32).reshape(N_EXPERTS, 1))


WEIGHT_DMA_PRIORITY = 1
ROW_GROUP = 16
ROW_CHUNK = 8
BLOCKS_PER_STEP = 2


def _rowmap_kernel(dest_ref, tok_ref):
    n_tok = dest_ref.shape[0] // TOP_K

    def body(j, c):
        for k in range(TOP_K):
            tok_ref[dest_ref[k * n_tok + j]] = j
        return c

    lax.fori_loop(0, n_tok, body, 0, unroll=4)


def row_tokens(dest_flat, n_rows):
    return pl.pallas_call(
        _rowmap_kernel,
        grid_spec=pltpu.PrefetchScalarGridSpec(
            num_scalar_prefetch=1,
            grid=(1,),
            in_specs=[],
            out_specs=pl.BlockSpec(memory_space=pltpu.SMEM),
        ),
        out_shape=jax.ShapeDtypeStruct((n_rows,), I32),
        compiler_params=_cparams(("arbitrary",)),
        name="row_tokens",
    )(dest_flat)


def _experts_kernel(be_ref, nxt_ref, nxt2_ref, slot_ref, cnt_ref, tok_ref, nb_ref,
                    h1p_ref, wg_ref, wu_ref, wd_ref, ys_ref,
                    xbuf, wgb, wub, wdb, wg_s, wu_s, wd_s, gsem, wsem):
    tm = ys_ref.shape[0] // (SUBLANES * BLOCKS_PER_STEP)
    step = pl.program_id(0)
    nb = nb_ref[0]

    def weight_copies(e, slot):
        return (pltpu.make_async_copy(wg_ref.at[e], wgb.at[slot], wsem.at[slot]),
                pltpu.make_async_copy(wu_ref.at[e], wub.at[slot], wsem.at[slot]),
                pltpu.make_async_copy(wd_ref.at[e], wdb.at[slot], wsem.at[slot]))

    def start_group(grp, slot):
        def one_block(g, carry):
            blk = jnp.minimum(grp * ROW_GROUP + g, nb - 1)
            cnt = cnt_ref[blk]
            last = cnt - 1
            base = blk * tm
            n_chunks = lax.div(cnt + (ROW_CHUNK - 1), ROW_CHUNK)

            def issue(ch, c):
                for j in range(ROW_CHUNK):
                    r = ch * ROW_CHUNK + j
                    t = tok_ref[base + jnp.minimum(r, last)]
                    pltpu.make_async_copy(h1p_ref.at[pl.ds(t * SUBLANES, SUBLANES)],
                                          xbuf.at[slot, pl.ds((g * tm + r) * SUBLANES, SUBLANES)],
                                          gsem.at[slot]).start()
                return c

            lax.fori_loop(0, n_chunks, issue, 0)
            done = n_chunks * ROW_CHUNK
            pad = tm - done
            size = tm // 2
            while size >= ROW_CHUNK:
                @pl.when((pad & size) != 0)
                def _(done=done, size=size):
                    pltpu.make_async_copy(
                        h1p_ref.at[pl.ds(0, size * SUBLANES)],
                        xbuf.at[slot, pl.ds((g * tm + done) * SUBLANES, size * SUBLANES)],
                        gsem.at[slot]).start()
                done = done + jnp.where((pad & size) != 0, size, 0)
                size //= 2
            return carry

        lax.fori_loop(0, ROW_GROUP, one_block, 0)

    def process(i, out_ref):
        e = be_ref[i]
        wslot = slot_ref[i]
        grp = lax.div(i, ROW_GROUP)
        sub = lax.rem(i, ROW_GROUP)
        gslot = lax.rem(grp, 2)

        @pl.when(i == 0)
        def _():
            for cp in weight_copies(e, wslot):
                cp.start(priority=WEIGHT_DMA_PRIORITY)
            nxt = nxt_ref[0]

            @pl.when(nxt >= 0)
            def _():
                for cp in weight_copies(nxt, 1 - wslot):
                    cp.start(priority=WEIGHT_DMA_PRIORITY)

            start_group(0, 0)

        @pl.when((i == 0) | (e != be_ref[jnp.maximum(i - 1, 0)]))
        def _():
            for cp in weight_copies(e, wslot):
                cp.wait()
            wg_s[...] = wgb[wslot].astype(BF16)
            wu_s[...] = wub[wslot].astype(BF16)
            wd_s[...] = wdb[wslot].astype(BF16)
            nxt2 = nxt2_ref[i]

            @pl.when(nxt2 >= 0)
            def _():
                for cp in weight_copies(nxt2, wslot):
                    cp.start(priority=WEIGHT_DMA_PRIORITY)

        @pl.when(sub == 0)
        def _():
            pltpu.make_async_copy(h1p_ref.at[pl.ds(0, ROW_GROUP * tm * SUBLANES)], xbuf.at[gslot],
                                  gsem.at[gslot]).wait()

            @pl.when((grp + 1) * ROW_GROUP < nb)
            def _():
                start_group(grp + 1, 1 - gslot)

        row0 = pl.multiple_of(sub * (tm * SUBLANES), tm * SUBLANES)
        x = _unpack_bf16_pair(_load_row_tiles(xbuf.at[gslot, pl.ds(row0, tm * SUBLANES)], tm))
        a = jnp.dot(x, wg_s[...], preferred_element_type=F32)
        u = jnp.dot(x, wu_s[...], preferred_element_type=F32)
        hid = (jax.nn.silu(a) * u).astype(BF16)
        y = jnp.dot(hid, wd_s[...], preferred_element_type=F32)
        _store_row_tiles(out_ref, _pack_bf16_pair(y))

    for h in range(BLOCKS_PER_STEP):
        blk = step * BLOCKS_PER_STEP + h

        @pl.when(blk < nb)
        def _(blk=blk, h=h):
            process(blk, ys_ref.at[pl.ds(h * tm * SUBLANES, tm * SUBLANES)])


def grouped_experts(h1p_tiles, row_tok, block_expert, block_next, block_next2, block_slot, block_count,
                    n_blocks_used, w_gate, w_up, w_down, tm=MOE_BLOCK):
    n_rows = row_tok.shape[0]
    n_e, d, ff = w_gate.shape
    n_blocks = n_rows // tm
    assert n_blocks % BLOCKS_PER_STEP == 0
    bps = BLOCKS_PER_STEP
    rowmap = lambda i, be, nx, nx2, sl, cn, tok, nb: (jnp.minimum(i, jnp.maximum(nb[0] - 1, 0) // bps), 0)
    hbm = pl.BlockSpec(memory_space=pl.ANY)
    return pl.pallas_call(
        _experts_kernel,
        grid_spec=pltpu.PrefetchScalarGridSpec(
            num_scalar_prefetch=7,
            grid=(n_blocks // bps,),
            in_specs=[hbm, hbm, hbm, hbm],
            out_specs=pl.BlockSpec((bps * tm * SUBLANES, LANES), rowmap),
            scratch_shapes=[pltpu.VMEM((2, ROW_GROUP * tm * SUBLANES, LANES), U32),
                            pltpu.VMEM((2, d, ff), F32), pltpu.VMEM((2, d, ff), F32), pltpu.VMEM((2, ff, d), F32),
                            pltpu.VMEM((d, ff), BF16), pltpu.VMEM((d, ff), BF16), pltpu.VMEM((ff, d), BF16),
                            pltpu.SemaphoreType.DMA((2,)), pltpu.SemaphoreType.DMA((2,))],
        ),
        out_shape=jax.ShapeDtypeStruct((n_rows * SUBLANES, LANES), U32),
        compiler_params=_cparams(("arbitrary",)),
        name="grouped_experts",
    )(block_expert, block_next, block_next2, block_slot, block_count, row_tok, n_blocks_used,
      h1p_tiles, w_gate, w_up, w_down)


def _combine_kernel(dest_ref, h1_ref, h1p_ref, gate_ref, wsg_ref, wsu_ref, wsd_ref, g2_ref, b2_ref, ys_ref,
                    o_ref, buf_s, sem):
    tm = h1_ref.shape[0]
    i = pl.program_id(0)
    n_steps = pl.num_programs(0)
    n_tok = n_steps * tm
    slot = lax.rem(i, 2)

    def start_rows(step, sl):
        def issue(r, c):
            for k in range(TOP_K):
                row = dest_ref[k * n_tok + step * tm + r]
                pltpu.make_async_copy(ys_ref.at[pl.ds(row * SUBLANES, SUBLANES)],
                                      buf_s.at[sl, k, pl.ds(r * SUBLANES, SUBLANES)], sem.at[sl]).start()
            return c

        lax.fori_loop(0, tm, issue, 0, unroll=2)

    @pl.when(i == 0)
    def _():
        start_rows(0, 0)

    @pl.when(i + 1 < n_steps)
    def _():
        start_rows(i + 1, 1 - slot)

    xb = _unpack_bf16_pair(_load_row_tiles(h1p_ref, tm))
    a = jnp.dot(xb, wsg_ref[...], preferred_element_type=F32)
    u = jnp.dot(xb, wsu_ref[...], preferred_element_type=F32)
    shared = jnp.dot((jax.nn.silu(a) * u).astype(BF16), wsd_ref[...], preferred_element_type=F32)

    for k in range(TOP_K):
        pltpu.make_async_copy(ys_ref.at[pl.ds(0, tm * SUBLANES)], buf_s.at[slot, k], sem.at[slot]).wait()

    gate = gate_ref[...]
    routed = jnp.zeros_like(shared)
    for k in range(TOP_K):
        routed = routed + _unpack_pair_f32(_load_row_tiles(buf_s.at[slot, k], tm)) * gate[:, k:k + 1]
    o_ref[...] = _layer_norm(DN_ALPHA * h1_ref[...] + routed + shared, g2_ref[...], b2_ref[...])


def combine_shared_ln(dest_flat, h1, h1p_tiles, gate_nk, ys_tiles, wsg, wsu, wsd, g2, b2, tm=128):
    n, d = h1.shape
    ff = wsg.shape[1]
    row = lambda i, dest: (i, 0)
    const = lambda i, dest: (0, 0)
    return pl.pallas_call(
        _combine_kernel,
        grid_spec=pltpu.PrefetchScalarGridSpec(
            num_scalar_prefetch=1,
            grid=(n // tm,),
            in_specs=[pl.BlockSpec((tm, d), row), pl.BlockSpec((tm * SUBLANES, LANES), row),
                      pl.BlockSpec((tm, TOP_K), row),
                      pl.BlockSpec((d, ff), const), pl.BlockSpec((d, ff), const), pl.BlockSpec((ff, d), const),
                      pl.BlockSpec((1, d), const), pl.BlockSpec((1, d), const),
                      pl.BlockSpec(memory_space=pl.ANY)],
            out_specs=pl.BlockSpec((tm, d), row),
            scratch_shapes=[pltpu.VMEM((2, TOP_K, tm * SUBLANES, LANES), U32), pltpu.SemaphoreType.DMA((2,))],
        ),
        out_shape=jax.ShapeDtypeStruct((n, d), F32),
        compiler_params=_cparams(("arbitrary",)),
        name="combine_shared_ln",
    )(dest_flat, h1, h1p_tiles, gate_nk, wsg, wsu, wsd, g2.reshape(1, d), b2.reshape(1, d), ys_tiles)


def kernel(x, ln_emb_g, ln_emb_b, w_in, lb_logits, hg_norm_w, conv_w, conv_b, rg_wa, rg_ba, rg_wx, rg_bx,
           rg_lam, w_out, ln1_g, ln1_b, w_router, router_bias, w_gate, w_up, w_down, ws_gate, ws_up, ws_down,
           ln2_g, ln2_b):
    batch, seq, d = x.shape
    assert d == 2 * SUBLANES * LANES, "packed rows must fill one 8x128 tile"
    n_tok = batch * seq
    layer = 0
    x2 = x.reshape(n_tok, d)
    lb = jnp.cumsum(jax.nn.softmax(lb_logits.astype(F32), axis=1), axis=1)

    proj = ln_inproj(x2, ln_emb_g, ln_emb_b, w_in[layer].astype(BF16))
    y_hg = hgrn2_mixer(proj, lb[0, layer], lb[1, layer], hg_norm_w[layer], batch, seq)
    y_rg = rglru_mixer(proj, conv_w[layer], conv_b[layer], rg_wa[layer], rg_ba[layer], rg_wx[layer],
                       rg_bx[layer], rg_lam[layer], batch, seq)
    h1, h1p_tiles, logits_t = outproj_ln_router(
        y_hg, y_rg, x2, ln_emb_g, ln_emb_b, w_out[layer].astype(BF16), ln1_g[layer], ln1_b[layer],
        w_router[layer].T.astype(BF16))

    idx_t, gate_t = route_topk(logits_t, router_bias[layer])
    rank_t, counts = expert_ranks(idx_t)
    counts = counts[:, 0]

    n_assign = n_tok * TOP_K
    n_blocks = -(-(n_assign + N_EXPERTS * (MOE_BLOCK - 1)) // MOE_BLOCK)
    n_rows = n_blocks * MOE_BLOCK
    padded = (counts + MOE_BLOCK - 1) // MOE_BLOCK * MOE_BLOCK
    padded_end = jnp.cumsum(padded)
    padded_start = padded_end - padded
    n_used_blocks = (padded_end[-1:] // MOE_BLOCK).astype(I32)
    blk_row = jnp.minimum(jnp.arange(n_blocks, dtype=I32), n_used_blocks[0] - 1) * MOE_BLOCK
    block_expert = jnp.minimum(
        jnp.sum((padded_end[None, :] <= blk_row[:, None]).astype(I32), axis=1), N_EXPERTS - 1)
    e_ids = jnp.arange(N_EXPERTS, dtype=I32)
    active = counts > 0
    later = lax.cummin(jnp.where(active, e_ids, N_EXPERTS)[::-1])[::-1]
    nxt = jnp.concatenate([later[1:], jnp.full((1,), N_EXPERTS, I32)])
    next1 = jnp.where(nxt < N_EXPERTS, nxt, -1).astype(I32)
    hot1 = (next1[:, None] == e_ids[None, :]).astype(I32)
    next2 = jnp.sum(hot1 * (next1 + 1)[None, :], axis=1) - 1
    ordinal = jnp.cumsum(active.astype(I32)) - 1
    onehot = (block_expert[:, None] == e_ids[None, :]).astype(I32)
    block_next = jnp.sum(onehot * next1[None, :], axis=1)
    block_next2 = jnp.sum(onehot * next2[None, :], axis=1)
    block_slot = jnp.sum(onehot * (ordinal % 2)[None, :], axis=1)
    block_count = jnp.clip(jnp.sum(onehot * (counts + padded_start)[None, :], axis=1) - blk_row, 1, MOE_BLOCK)

    dest_flat = assignment_rows(idx_t, rank_t, padded_start).reshape(-1)
    row_tok = row_tokens(dest_flat, n_rows)

    ys_tiles = grouped_experts(h1p_tiles, row_tok, block_expert, block_next, block_next2, block_slot,
                               block_count, n_used_blocks, w_gate[layer], w_up[layer], w_down[layer])
    out = combine_shared_ln(dest_flat, h1, h1p_tiles, gate_t.T, ys_tiles,
                            ws_gate[layer].astype(BF16), ws_up[layer].astype(BF16),
                            ws_down[layer].astype(BF16), ln2_g[layer], ln2_b[layer])
    return out.reshape(batch, seq, d)
```

```python
import functools

import numpy as np
import jax
import jax.numpy as jnp
from jax import lax
from jax.experimental import pallas as pl
from jax.experimental.pallas import tpu as pltpu

F32 = jnp.float32
BF16 = jnp.bfloat16
U32 = jnp.uint32
I32 = jnp.int32

D_MODEL = 2048
DEPTH = 1
HG_KEY_DIM = 128
HG_VAL_DIM = 128
HG_WIDTH = D_MODEL // 2
HG_HEADS = HG_WIDTH // HG_VAL_DIM
HG_KEY_WIDTH = HG_HEADS * HG_KEY_DIM
RG_WIDTH = D_MODEL - HG_WIDTH
RG_BLOCKS = 8
RG_BLOCK_DIM = RG_WIDTH // RG_BLOCKS
RG_CONV = 4
RG_C = 8.0
MIX_WIDTH = HG_WIDTH + RG_WIDTH
IN_WIDTHS = (HG_KEY_WIDTH, HG_KEY_WIDTH, HG_KEY_WIDTH, HG_WIDTH, HG_WIDTH, RG_WIDTH, RG_WIDTH)
IN_WIDTH = sum(IN_WIDTHS)
N_EXPERTS = 256
TOP_K = 8
N_GROUPS = 8
TOPK_GROUPS = 4
GROUP_SIZE = N_EXPERTS // N_GROUPS
EXPERT_FF = 512
SHARED_FF = 512
ROUTE_SCALE = 2.5
MOE_BLOCK = 128
DN_ALPHA = (2 * DEPTH) ** 0.25
LN_EPS = 1e-5
RMS_EPS = 1e-6

LANES = 128
SUBLANES = 8
VMEM_LIMIT_BYTES = 56 * 1024 * 1024

HG_CHUNK = 128
HG_SUB = 8
HG_LEVELS = (16, 32, 64, 128)


def _cparams(sem, vmem=VMEM_LIMIT_BYTES):
    return pltpu.CompilerParams(dimension_semantics=sem, vmem_limit_bytes=vmem)


def _layer_norm(x, g, b):
    mu = jnp.mean(x, axis=-1, keepdims=True)
    xc = x - mu
    var = jnp.mean(xc * xc, axis=-1, keepdims=True)
    return xc * lax.rsqrt(var + LN_EPS) * g + b


def _pack_bf16_pair(x):
    n = x.shape[1] // 2
    bits = lax.bitcast_convert_type(x.astype(BF16).astype(F32), U32)
    return (bits[:, :n] >> 16) | (bits[:, n:] & jnp.uint32(0xFFFF0000))


def _unpack_pair_f32(w):
    lo = lax.bitcast_convert_type(w << 16, F32)
    hi = lax.bitcast_convert_type(w & jnp.uint32(0xFFFF0000), F32)
    return jnp.concatenate([lo, hi], axis=1)


def _unpack_bf16_pair(w):
    return _unpack_pair_f32(w).astype(BF16)


def _store_row_tiles(ref, packed):
    m = packed.shape[0]
    for s in range(SUBLANES):
        ref[pl.ds(s, m, stride=SUBLANES), :] = packed[:, s * LANES:(s + 1) * LANES]


def _load_row_tiles(ref, m):
    return jnp.concatenate([ref[pl.ds(s, m, stride=SUBLANES), :] for s in range(SUBLANES)], axis=1)


def _ln_inproj_kernel(x_ref, g_ref, b_ref, w_ref, o_ref, hb_ref):
    @pl.when(pl.program_id(1) == 0)
    def _():
        hb_ref[...] = _layer_norm(x_ref[...], g_ref[...], b_ref[...]).astype(BF16)

    o_ref[...] = jnp.dot(hb_ref[...], w_ref[...], preferred_element_type=F32).astype(o_ref.dtype)


def ln_inproj(x2, g, b, w_bf16, tm=1024, tn=1024):
    n, d = x2.shape
    width = w_bf16.shape[1]
    return pl.pallas_call(
        _ln_inproj_kernel,
        grid=(n // tm, width // tn),
        in_specs=[
            pl.BlockSpec((tm, d), lambda i, j: (i, 0)),
            pl.BlockSpec((1, d), lambda i, j: (0, 0)),
            pl.BlockSpec((1, d), lambda i, j: (0, 0)),
            pl.BlockSpec((d, tn), lambda i, j: (0, j)),
        ],
        out_specs=pl.BlockSpec((tm, tn), lambda i, j: (i, j)),
        out_shape=jax.ShapeDtypeStruct((n, width), BF16),
        scratch_shapes=[pltpu.VMEM((tm, d), BF16)],
        compiler_params=_cparams(("parallel", "arbitrary")),
        name="ln_inproj",
    )(x2, g.reshape(1, d), b.reshape(1, d), w_bf16)


def _hgrn2_constants():
    c = HG_CHUNK
    t = np.arange(c)[:, None]
    u = np.arange(c)[None, :]
    fwd = [(u <= t), (u > t)]
    bwd = [(u >= t), (u < t)]
    for m in HG_LEVELS:
        start = (t // m) * m
        mid = start + m // 2 - 1
        second = t > mid
        fwd.append(np.where(second, (u > mid) & (u <= t), (u > t) & (u <= mid)))
        bwd.append(np.where(second, (u > mid) & (u < t), (u >= t) & (u <= mid)))
    fwd = np.concatenate(fwd, axis=0).astype(np.float32)
    bwd = np.concatenate(bwd, axis=0).astype(np.float32)
    fwd = np.concatenate([fwd, fwd], axis=1)
    bwd = np.concatenate([bwd, bwd], axis=1)
    lvl = np.full((c, c), -1, np.int32)
    tt = np.arange(c)[:, None]
    ss = np.arange(c)[None, :]
    for li in reversed(range(len(HG_LEVELS))):
        m = HG_LEVELS[li]
        lvl = np.where((tt // m) == (ss // m), li, lvl)
    lvl = np.where((tt // HG_SUB) == (ss // HG_SUB), -1, lvl).astype(np.int32)
    return fwd, bwd, lvl


def _hgrn2_kernel(q_ref, zf_ref, zb_ref, v_ref, g_ref, lbf_ref, lbb_ref, nw_ref,
                  mf_ref, mb_ref, lvl_ref, o_ref,
                  qs_s, v_s, lf_s, kf_s, lb_s, kb_s, cf_s, cb_s,
                  qif_s, kif_s, qib_s, kib_s, df_s, db_s, acc_s, stf_s, stb_s):
    seq = q_ref.shape[0]
    c = HG_CHUNK
    n_chunks = seq // c
    n_lvl = len(HG_LEVELS)

    qs_s[...] = jax.nn.silu(q_ref[...].astype(F32))
    v_s[...] = v_ref[...].astype(F32)
    for z_ref, lb_ref, lf_out, k_out in ((zf_ref, lbf_ref, lf_s, kf_s), (zb_ref, lbb_ref, lb_s, kb_s)):
        z = z_ref[...].astype(F32)
        lb = lb_ref[...]
        sig = jax.nn.sigmoid(z)
        lf_out[...] = jnp.log(lb + (1.0 - lb) * sig)
        k_out[...] = (1.0 - lb) * (1.0 - sig)

    ones_mat = jnp.ones((LANES, LANES), BF16)
    lvl = lvl_ref[...]
    row_c = lax.broadcasted_iota(I32, (c, LANES), 0)
    lane8 = lax.broadcasted_iota(I32, (HG_SUB, LANES), 1)
    sub8 = lax.broadcasted_iota(I32, (HG_SUB, LANES), 0)
    n_stack = (2 + n_lvl) * c

    def intra(n, carry):
        r0 = pl.multiple_of(n * c, c)
        rows = pl.ds(r0, c)
        qs = qs_s[rows, :]
        kf = kf_s[rows, :]
        kb = kb_s[rows, :]
        hl = []
        for lfs in (lf_s, lb_s):
            lf = lfs[rows, :]
            hi = lf.astype(BF16)
            hl.append(jnp.concatenate([hi, (lf - hi.astype(F32)).astype(BF16)], axis=0))
        ef = jnp.dot(mf_ref[...], hl[0], preferred_element_type=F32)
        eb = jnp.dot(mb_ref[...], hl[1], preferred_element_type=F32)
        for (e, kk, cs, qi_s, ki_s, d_s, tot_row) in ((ef, kf, cf_s, qif_s, kif_s, df_s, c - 1),
                                                    (eb, kb, cb_s, qib_s, kib_s, db_s, 0)):
            cum = e[0:c]
            cs[rows, :] = cum
            qi_s[rows, :] = (qs * jnp.exp(cum)).astype(BF16)
            ki_s[rows, :] = (kk * jnp.exp(e[c:2 * c])).astype(BF16)
            d_s[pl.ds(n, 1), :] = jnp.exp(cum[tot_row:tot_row + 1, :])

        a_mat = jnp.zeros((c, c), F32)
        for li, m in enumerate(HG_LEVELS):
            second = (row_c & (m // 2)) != 0
            wf = jnp.exp(ef[(2 + li) * c:(3 + li) * c])
            wb = jnp.exp(eb[(2 + li) * c:(3 + li) * c])
            qcat = jnp.concatenate([qs * jnp.where(second, wf, 0.0), qs * jnp.where(second, 0.0, wb)], axis=1)
            kcat = jnp.concatenate([kf * jnp.where(second, 0.0, wf), kb * jnp.where(second, wb, 0.0)], axis=1)
            sc = lax.dot_general(qcat.astype(BF16), kcat.astype(BF16),
                                 (((1,), (1,)), ((), ())), preferred_element_type=F32)
            a_mat = jnp.where(lvl == li, sc, a_mat)

        p_list = []
        for j in range(c // HG_SUB):
            rj = r0 + j * HG_SUB
            sl = pl.ds(rj, HG_SUB)
            cf = cf_s[sl, :]
            cb = cb_s[sl, :]
            qj = qs_s[sl, :]
            for s in range(HG_SUB):
                src = pl.ds(rj + s, 1)
                df = jnp.where(sub8 >= s, cf - cf_s[src, :], -jnp.inf)
                db = jnp.where(sub8 <= s, cb - cb_s[src, :], -jnp.inf)
                p_list.append(qj * (kf_s[src, :] * jnp.exp(df) + kb_s[src, :] * jnp.exp(db)))
        p_all = jnp.concatenate(p_list, axis=0).astype(BF16)
        red = jnp.dot(p_all, ones_mat, preferred_element_type=F32)
        diag_rows = []
        for j in range(c // HG_SUB):
            blk = jnp.zeros((HG_SUB, LANES), F32)
            for s in range(HG_SUB):
                idx = (j * HG_SUB + s) * HG_SUB
                blk = jnp.where(lane8 == j * HG_SUB + s, red[idx:idx + HG_SUB], blk)
            diag_rows.append(blk)
        a_mat = a_mat + jnp.concatenate(diag_rows, axis=0)
        acc_s[rows, :] = jnp.dot(a_mat.astype(BF16), v_s[rows, :].astype(BF16), preferred_element_type=F32)
        return carry

    lax.fori_loop(0, n_chunks, intra, 0, unroll=4)

    stf_s[...] = jnp.zeros_like(stf_s)
    stb_s[...] = jnp.zeros_like(stb_s)

    def inter(n, carry):
        for (st_s, qi_s, ki_s, d_s, idx) in ((stf_s, qif_s, kif_s, df_s, n),
                                             (stb_s, qib_s, kib_s, db_s, n_chunks - 1 - n)):
            rows = pl.ds(pl.multiple_of(idx * c, c), c)
            st = st_s[...]
            acc_s[rows, :] += lax.dot_general(qi_s[rows, :], st.astype(BF16),
                                              (((1,), (1,)), ((), ())), preferred_element_type=F32)
            upd = lax.dot_general(v_s[rows, :].astype(BF16), ki_s[rows, :],
                                  (((0,), (0,)), ((), ())), preferred_element_type=F32)
            st_s[...] = st * d_s[pl.ds(idx, 1), :] + upd
        return carry

    lax.fori_loop(0, n_chunks, inter, 0, unroll=4)

    o = acc_s[...]
    o = o * lax.rsqrt(jnp.mean(o * o, axis=-1, keepdims=True) + RMS_EPS) * nw_ref[...]
    o_ref[...] = (o * jax.nn.silu(g_ref[...].astype(F32))).astype(o_ref.dtype)


def hgrn2_mixer(proj, lb_f, lb_b, norm_w, batch, seq):
    n = proj.shape[0]
    heads = HG_HEADS
    c = HG_CHUNK
    n_chunks = seq // c
    mf, mb, lvl = _hgrn2_constants()
    n_stack = mf.shape[0]

    def col(off):
        return pl.BlockSpec((seq, LANES), lambda b, h, off=off: (b, off + h))

    def per_head(arr):
        return pl.BlockSpec((1, LANES), lambda b, h: (0, h))

    const2 = lambda b, h: (0, 0)
    seq_f32 = pltpu.VMEM((seq, LANES), F32)
    seq_bf16 = pltpu.VMEM((seq, LANES), BF16)
    dec = pltpu.VMEM((n_chunks, LANES), F32)
    state = pltpu.VMEM((LANES, LANES), F32)
    return pl.pallas_call(
        _hgrn2_kernel,
        grid=(batch, heads),
        in_specs=[col(0), col(heads), col(2 * heads), col(3 * heads), col(4 * heads),
                  per_head(lb_f), per_head(lb_b),
                  pl.BlockSpec((1, LANES), const2),
                  pl.BlockSpec((n_stack, 2 * c), const2),
                  pl.BlockSpec((n_stack, 2 * c), const2),
                  pl.BlockSpec((c, c), const2)],
        out_specs=pl.BlockSpec((seq, LANES), lambda b, h: (b, h)),
        out_shape=jax.ShapeDtypeStruct((n, HG_WIDTH), BF16),
        scratch_shapes=[seq_f32] * 8 + [seq_bf16] * 4 + [dec, dec, seq_f32, state, state],
        compiler_params=_cparams(("parallel", "parallel")),
        name="hgrn2_mixer",
    )(proj, proj, proj, proj, proj, lb_f.reshape(1, -1), lb_b.reshape(1, -1), norm_w.reshape(1, -1),
      jnp.asarray(mf, BF16), jnp.asarray(mb, BF16), jnp.asarray(lvl))


def _shift_rows(x, d, fill, reverse):
    pad = jnp.full((d, x.shape[1]), fill, x.dtype)
    if reverse:
        return jnp.concatenate([x[d:], pad], axis=0)
    return jnp.concatenate([pad, x[:-d]], axis=0)


def _linear_scan(a, u, reverse):
    seq = a.shape[0]
    d = 1
    while d < seq:
        u = u + a * _shift_rows(u, d, 0.0, reverse)
        if d * 2 < seq:
            a = a * _shift_rows(a, d, 1.0, reverse)
        d *= 2
    return u


def _rglru_kernel(xr_ref, gr_ref, cw_ref, cb_ref, wa_ref, ba_ref, wx_ref, bx_ref, lam_ref, o_ref, xp_s):
    seq = xr_ref.shape[0]
    pad = SUBLANES
    pad_l = RG_CONV // 2
    xp_s[0:pad, :] = jnp.zeros((pad, LANES), F32)
    xp_s[pad + seq:, :] = jnp.zeros((pad, LANES), F32)
    xp_s[pad:pad + seq, :] = xr_ref[...].astype(F32)
    xc = cb_ref[...]
    for j in range(RG_CONV):
        off = pad + j - pad_l
        xc = xc + xp_s[off:off + seq, :] * cw_ref[j:j + 1, :]
    xcb = xc.astype(BF16)
    h = None
    for d in range(2):
        r = jax.nn.sigmoid(jnp.dot(xcb, wa_ref[d, 0], preferred_element_type=F32) + ba_ref[d:d + 1, :])
        i = jax.nn.sigmoid(jnp.dot(xcb, wx_ref[d, 0], preferred_element_type=F32) + bx_ref[d:d + 1, :])
        log_a = (-RG_C) * r * jax.nn.softplus(-lam_ref[d:d + 1, :])
        a = jnp.exp(log_a)
        u = jnp.sqrt(-jnp.tanh(log_a) * (a * a + 1.0)) * (i * xc)
        hd = _linear_scan(a, u, reverse=(d == 1))
        h = hd if h is None else h + hd
    o_ref[...] = (h * jax.nn.gelu(gr_ref[...].astype(F32))).astype(o_ref.dtype)


def rglru_mixer(proj, conv_w, conv_b, wa, ba, wx, bx, lam, batch, seq):
    n = proj.shape[0]
    x_off = (3 * HG_KEY_WIDTH + 2 * HG_WIDTH) // LANES
    g_off = x_off + RG_WIDTH // LANES
    blk = lambda b, j: (0, j)
    return pl.pallas_call(
        _rglru_kernel,
        grid=(batch, RG_BLOCKS),
        in_specs=[
            pl.BlockSpec((seq, LANES), lambda b, j: (b, x_off + j)),
            pl.BlockSpec((seq, LANES), lambda b, j: (b, g_off + j)),
            pl.BlockSpec((RG_CONV, LANES), blk),
            pl.BlockSpec((1, LANES), blk),
            pl.BlockSpec((2, 1, RG_BLOCK_DIM, RG_BLOCK_DIM), lambda b, j: (0, j, 0, 0)),
            pl.BlockSpec((2, LANES), blk),
            pl.BlockSpec((2, 1, RG_BLOCK_DIM, RG_BLOCK_DIM), lambda b, j: (0, j, 0, 0)),
            pl.BlockSpec((2, LANES), blk),
            pl.BlockSpec((2, LANES), blk),
        ],
        out_specs=pl.BlockSpec((seq, LANES), lambda b, j: (b, j)),
        out_shape=jax.ShapeDtypeStruct((n, RG_WIDTH), BF16),
        scratch_shapes=[pltpu.VMEM((seq + 2 * SUBLANES, LANES), F32)],
        compiler_params=_cparams(("parallel", "parallel")),
        name="rglru_mixer",
    )(proj, proj, conv_w, conv_b.reshape(1, -1), wa.astype(BF16), ba, wx.astype(BF16), bx, lam)


def _outproj_kernel(yh_ref, yr_ref, x_ref, g0_ref, b0_ref, wo_ref, g1_ref, b1_ref, wr_ref,
                    h1_ref, h1p_ref, lg_ref):
    half = yh_ref.shape[1]
    h0 = _layer_norm(x_ref[...], g0_ref[...], b0_ref[...])
    mix = jnp.dot(yh_ref[...], wo_ref[0:half, :], preferred_element_type=F32)
    mix = mix + jnp.dot(yr_ref[...], wo_ref[half:, :], preferred_element_type=F32)
    h1 = _layer_norm(DN_ALPHA * h0 + mix, g1_ref[...], b1_ref[...])
    h1_ref[...] = h1
    _store_row_tiles(h1p_ref, _pack_bf16_pair(h1))
    lg_ref[...] = lax.dot_general(wr_ref[...], h1.astype(BF16), (((1,), (1,)), ((), ())),
                                  preferred_element_type=F32)


def outproj_ln_router(y_hg, y_rg, x2, g0, b0, w_out_bf16, g1, b1, w_router_t_bf16, tm=256):
    n, d = x2.shape
    half = y_hg.shape[1]
    n_e = w_router_t_bf16.shape[0]
    row = lambda i: (i, 0)
    const = lambda i: (0, 0)
    vec = pl.BlockSpec((1, d), const)
    return pl.pallas_call(
        _outproj_kernel,
        grid=(n // tm,),
        in_specs=[pl.BlockSpec((tm, half), row), pl.BlockSpec((tm, half), row), pl.BlockSpec((tm, d), row),
                  vec, vec, pl.BlockSpec((2 * half, d), const), vec, vec,
                  pl.BlockSpec((n_e, d), const)],
        out_specs=[pl.BlockSpec((tm, d), row), pl.BlockSpec((tm * SUBLANES, LANES), row),
                   pl.BlockSpec((n_e, tm), lambda i: (0, i))],
        out_shape=[jax.ShapeDtypeStruct((n, d), F32), jax.ShapeDtypeStruct((n * SUBLANES, LANES), U32),
                   jax.ShapeDtypeStruct((n_e, n), F32)],
        compiler_params=_cparams(("parallel",)),
        name="outproj_ln_router",
    )(y_hg, y_rg, x2, g0.reshape(1, d), b0.reshape(1, d), w_out_bf16, g1.reshape(1, d), b1.reshape(1, d),
      w_router_t_bf16)


def _first_argmax(vals, iota, n):
    m = jnp.max(vals, axis=0, keepdims=True)
    first = jnp.min(jnp.where(vals == m, iota, n), axis=0, keepdims=True)
    return m, first


def _route_kernel(lg_ref, bias_ref, idx_ref, gate_ref):
    n_e, tn = lg_ref.shape
    scores = jax.nn.sigmoid(lg_ref[...])
    choice = scores + bias_ref[...]
    neg = -jnp.inf
    gi = lax.broadcasted_iota(I32, (GROUP_SIZE, tn), 0)
    gs_rows = []
    for g in range(N_GROUPS):
        blk = choice[g * GROUP_SIZE:(g + 1) * GROUP_SIZE]
        m1, f1 = _first_argmax(blk, gi, GROUP_SIZE)
        m2 = jnp.max(jnp.where(gi == f1, neg, blk), axis=0, keepdims=True)
        gs_rows.append(m1 + m2)
    gs = jnp.concatenate(gs_rows, axis=0)
    ri = lax.broadcasted_iota(I32, (N_GROUPS, tn), 0)
    gmask = jnp.zeros((N_GROUPS, tn), jnp.bool_)
    for _ in range(TOPK_GROUPS):
        _, f = _first_argmax(gs, ri, N_GROUPS)
        hit = ri == f
        gmask = gmask | hit
        gs = jnp.where(hit, neg, gs)
    gm = jnp.where(gmask, 1.0, 0.0)
    emask = jnp.concatenate(
        [jnp.broadcast_to(gm[g:g + 1], (GROUP_SIZE, tn)) for g in range(N_GROUPS)], axis=0)
    masked = jnp.where(emask > 0.5, choice, neg)
    ei = lax.broadcasted_iota(I32, (n_e, tn), 0)
    idx_rows, gate_rows = [], []
    for _ in range(TOP_K):
        _, f = _first_argmax(masked, ei, n_e)
        hit = ei == f
        idx_rows.append(f)
        gate_rows.append(jnp.sum(jnp.where(hit, scores, 0.0), axis=0, keepdims=True))
        masked = jnp.where(hit, neg, masked)
    gate = jnp.concatenate(gate_rows, axis=0)
    gate = gate / jnp.sum(gate, axis=0, keepdims=True) * ROUTE_SCALE
    idx_ref[...] = jnp.concatenate(idx_rows, axis=0)
    gate_ref[...] = gate


def route_topk(logits_t, bias, tn=512):
    n_e, n = logits_t.shape
    return pl.pallas_call(
        _route_kernel,
        grid=(n // tn,),
        in_specs=[pl.BlockSpec((n_e, tn), lambda i: (0, i)), pl.BlockSpec((n_e, 1), lambda i: (0, 0))],
        out_specs=[pl.BlockSpec((TOP_K, tn), lambda i: (0, i)), pl.BlockSpec((TOP_K, tn), lambda i: (0, i))],
        out_shape=[jax.ShapeDtypeStruct((TOP_K, n), I32), jax.ShapeDtypeStruct((TOP_K, n), F32)],
        compiler_params=_cparams(("parallel",)),
        name="route_topk",
    )(logits_t, bias.reshape(n_e, 1))


def _rank_kernel(idx_ref, su_ref, rank_ref, cnt_ref, carry_s):
    n_e = cnt_ref.shape[0]
    tn = idx_ref.shape[1]

    @pl.when(pl.program_id(0) == 0)
    def _():
        carry_s[...] = jnp.zeros_like(carry_s)

    idx = idx_ref[...]
    ei = lax.broadcasted_iota(I32, (n_e, tn), 0)
    hits = [ei == idx[k:k + 1] for k in range(TOP_K)]
    member = jnp.zeros((n_e, tn), F32)
    for h in hits:
        member = member + jnp.where(h, 1.0, 0.0)
    prefix = jnp.dot(member.astype(BF16), su_ref[...], preferred_element_type=F32) + carry_s[:, 0:1]
    rank_ref[...] = jnp.concatenate(
        [jnp.sum(jnp.where(h, prefix, 0.0), axis=0, keepdims=True) for h in hits], axis=0).astype(I32)
    carry_s[...] = carry_s[...] + jnp.sum(member, axis=1, keepdims=True)
    cnt_ref[...] = carry_s[...].astype(I32)


def expert_ranks(idx_t, tn=512):
    k, n = idx_t.shape
    su = jnp.asarray(np.triu(np.ones((tn, tn), np.float32), 1), BF16)
    return pl.pallas_call(
        _rank_kernel,
        grid=(n // tn,),
        in_specs=[pl.BlockSpec((k, tn), lambda i: (0, i)), pl.BlockSpec((tn, tn), lambda i: (0, 0))],
        out_specs=[pl.BlockSpec((k, tn), lambda i: (0, i)), pl.BlockSpec((N_EXPERTS, LANES), lambda i: (0, 0))],
        out_shape=[jax.ShapeDtypeStruct((k, n), I32), jax.ShapeDtypeStruct((N_EXPERTS, LANES), I32)],
        scratch_shapes=[pltpu.VMEM((N_EXPERTS, LANES), F32)],
        compiler_params=_cparams(("arbitrary",)),
        name="expert_ranks",
    )(idx_t, su)


def _dest_kernel(idx_ref, rank_ref, start_ref, dest_ref):
    n_e = start_ref.shape[0]
    tn = idx_ref.shape[1]
    idx = idx_ref[...]
    ei = lax.broadcasted_iota(I32, (n_e, tn), 0)
    start = start_ref[...]
    base = jnp.concatenate(
        [jnp.sum(jnp.where(ei == idx[k:k + 1], start, 0.0), axis=0, keepdims=True) for k in range(TOP_K)],
        axis=0)
    dest_ref[...] = base.astype(I32) + rank_ref[...]


def assignment_rows(idx_t, rank_t, padded_start, tn=512):
    k, n = idx_t.shape
    blk = pl.BlockSpec((k, tn), lambda i: (0, i))
    return pl.pallas_call(
        _dest_kernel,
        grid=(n // tn,),
        in_specs=[blk, blk, pl.BlockSpec((N_EXPERTS, 1), lambda i: (0, 0))],
        out_specs=blk,
        out_shape=jax.ShapeDtypeStruct((k, n), I32),
        compiler_params=_cparams(("parallel",)),
        name="assignment_rows",
    )(idx_t, rank_t, padded_start.astype(F32).reshape(N_EXPERTS, 1))


WEIGHT_DMA_PRIORITY = 1
WEIGHT_SLOTS = 3
ROW_GROUP = 8
ROW_CHUNK = 8
BLOCKS_PER_STEP = 2


def _rowmap_kernel(dest_ref, tok_ref):
    n_tok = dest_ref.shape[0] // TOP_K

    def body(j, c):
        for k in range(TOP_K):
            tok_ref[dest_ref[k * n_tok + j]] = j
        return c

    lax.fori_loop(0, n_tok, body, 0, unroll=4)


def row_tokens(dest_flat, n_rows):
    return pl.pallas_call(
        _rowmap_kernel,
        grid_spec=pltpu.PrefetchScalarGridSpec(
            num_scalar_prefetch=1,
            grid=(1,),
            in_specs=[],
            out_specs=pl.BlockSpec(memory_space=pltpu.SMEM),
        ),
        out_shape=jax.ShapeDtypeStruct((n_rows,), I32),
        compiler_params=_cparams(("arbitrary",)),
        name="row_tokens",
    )(dest_flat)


def _experts_kernel(be_ref, nxt_ref, nxt2_ref, slot_ref, cnt_ref, tok_ref, nb_ref,
                    h1p_ref, wg_ref, wu_ref, wd_ref, ys_ref,
                    xbuf, wgb, wub, wdb, gsem, wsem):
    tm = ys_ref.shape[0] // (SUBLANES * BLOCKS_PER_STEP)
    step = pl.program_id(0)
    nb = nb_ref[0]

    def weight_copies(e, slot):
        return (pltpu.make_async_copy(wg_ref.at[e], wgb.at[slot], wsem.at[slot]),
                pltpu.make_async_copy(wu_ref.at[e], wub.at[slot], wsem.at[slot]),
                pltpu.make_async_copy(wd_ref.at[e], wdb.at[slot], wsem.at[slot]))

    def start_weights(e, slot):
        for cp in weight_copies(e, slot):
            cp.start(priority=WEIGHT_DMA_PRIORITY)

    def start_group(grp, slot):
        def one_block(g, carry):
            blk = jnp.minimum(grp * ROW_GROUP + g, nb - 1)
            cnt = cnt_ref[blk]
            last = cnt - 1
            base = blk * tm
            n_chunks = lax.div(cnt + (ROW_CHUNK - 1), ROW_CHUNK)

            def issue(ch, c):
                for j in range(ROW_CHUNK):
                    r = ch * ROW_CHUNK + j
                    t = tok_ref[base + jnp.minimum(r, last)]
                    pltpu.make_async_copy(h1p_ref.at[pl.ds(t * SUBLANES, SUBLANES)],
                                          xbuf.at[slot, pl.ds((g * tm + r) * SUBLANES, SUBLANES)],
                                          gsem.at[slot]).start()
                return c

            lax.fori_loop(0, n_chunks, issue, 0)
            done = n_chunks * ROW_CHUNK
            pad = tm - done
            size = tm // 2
            while size >= ROW_CHUNK:
                @pl.when((pad & size) != 0)
                def _(done=done, size=size):
                    pltpu.make_async_copy(
                        h1p_ref.at[pl.ds(0, size * SUBLANES)],
                        xbuf.at[slot, pl.ds((g * tm + done) * SUBLANES, size * SUBLANES)],
                        gsem.at[slot]).start()
                done = done + jnp.where((pad & size) != 0, size, 0)
                size //= 2
            return carry

        lax.fori_loop(0, ROW_GROUP, one_block, 0)

    def process(i, out_ref):
        e = be_ref[i]
        wslot = slot_ref[i]
        grp = lax.div(i, ROW_GROUP)
        sub = lax.rem(i, ROW_GROUP)
        gslot = lax.rem(grp, 2)

        @pl.when(i == 0)
        def _():
            start_weights(e, wslot)
            nxt = nxt_ref[0]

            @pl.when(nxt >= 0)
            def _():
                start_weights(nxt, lax.rem(wslot + 1, WEIGHT_SLOTS))

            start_group(0, 0)

        @pl.when((i == 0) | (e != be_ref[jnp.maximum(i - 1, 0)]))
        def _():
            for cp in weight_copies(e, wslot):
                cp.wait()
            nxt2 = nxt2_ref[i]

            @pl.when(nxt2 >= 0)
            def _():
                start_weights(nxt2, lax.rem(wslot + 2, WEIGHT_SLOTS))

        @pl.when(sub == 0)
        def _():
            pltpu.make_async_copy(h1p_ref.at[pl.ds(0, ROW_GROUP * tm * SUBLANES)], xbuf.at[gslot],
                                  gsem.at[gslot]).wait()

            @pl.when((grp + 1) * ROW_GROUP < nb)
            def _():
                start_group(grp + 1, 1 - gslot)

        row0 = pl.multiple_of(sub * (tm * SUBLANES), tm * SUBLANES)
        x = _unpack_bf16_pair(_load_row_tiles(xbuf.at[gslot, pl.ds(row0, tm * SUBLANES)], tm))
        a = jnp.dot(x, wgb[wslot].astype(BF16), preferred_element_type=F32)
        u = jnp.dot(x, wub[wslot].astype(BF16), preferred_element_type=F32)
        hid = (jax.nn.silu(a) * u).astype(BF16)
        y = jnp.dot(hid, wdb[wslot].astype(BF16), preferred_element_type=F32)
        _store_row_tiles(out_ref, _pack_bf16_pair(y))

    for h in range(BLOCKS_PER_STEP):
        blk = step * BLOCKS_PER_STEP + h

        @pl.when(blk < nb)
        def _(blk=blk, h=h):
            process(blk, ys_ref.at[pl.ds(h * tm * SUBLANES, tm * SUBLANES)])


def grouped_experts(h1p_tiles, row_tok, block_expert, block_next, block_next2, block_slot, block_count,
                    n_blocks_used, w_gate, w_up, w_down, tm=MOE_BLOCK):
    n_rows = row_tok.shape[0]
    n_e, d, ff = w_gate.shape
    n_blocks = n_rows // tm
    assert n_blocks % BLOCKS_PER_STEP == 0
    bps = BLOCKS_PER_STEP
    rowmap = lambda i, be, nx, nx2, sl, cn, tok, nb: (jnp.minimum(i, jnp.maximum(nb[0] - 1, 0) // bps), 0)
    hbm = pl.BlockSpec(memory_space=pl.ANY)
    return pl.pallas_call(
        _experts_kernel,
        grid_spec=pltpu.PrefetchScalarGridSpec(
            num_scalar_prefetch=7,
            grid=(n_blocks // bps,),
            in_specs=[hbm, hbm, hbm, hbm],
            out_specs=pl.BlockSpec((bps * tm * SUBLANES, LANES), rowmap),
            scratch_shapes=[pltpu.VMEM((2, ROW_GROUP * tm * SUBLANES, LANES), U32),
                            pltpu.VMEM((WEIGHT_SLOTS, d, ff), F32), pltpu.VMEM((WEIGHT_SLOTS, d, ff), F32),
                            pltpu.VMEM((WEIGHT_SLOTS, ff, d), F32),
                            pltpu.SemaphoreType.DMA((2,)), pltpu.SemaphoreType.DMA((WEIGHT_SLOTS,))],
        ),
        out_shape=jax.ShapeDtypeStruct((n_rows * SUBLANES, LANES), U32),
        compiler_params=_cparams(("arbitrary",)),
        name="grouped_experts",
    )(block_expert, block_next, block_next2, block_slot, block_count, row_tok, n_blocks_used,
      h1p_tiles, w_gate, w_up, w_down)


def _combine_kernel(dest_ref, h1_ref, h1p_ref, gate_ref, wsg_ref, wsu_ref, wsd_ref, g2_ref, b2_ref, ys_ref,
                    o_ref, buf_s, sem):
    tm = h1_ref.shape[0]
    i = pl.program_id(0)
    n_steps = pl.num_programs(0)
    n_tok = n_steps * tm
    slot = lax.rem(i, 2)

    def start_rows(step, sl):
        def issue(r, c):
            for k in range(TOP_K):
                row = dest_ref[k * n_tok + step * tm + r]
                pltpu.make_async_copy(ys_ref.at[pl.ds(row * SUBLANES, SUBLANES)],
                                      buf_s.at[sl, k, pl.ds(r * SUBLANES, SUBLANES)], sem.at[sl]).start()
            return c

        lax.fori_loop(0, tm, issue, 0, unroll=2)

    @pl.when(i == 0)
    def _():
        start_rows(0, 0)

    @pl.when(i + 1 < n_steps)
    def _():
        start_rows(i + 1, 1 - slot)

    xb = _unpack_bf16_pair(_load_row_tiles(h1p_ref, tm))
    a = jnp.dot(xb, wsg_ref[...], preferred_element_type=F32)
    u = jnp.dot(xb, wsu_ref[...], preferred_element_type=F32)
    shared = jnp.dot((jax.nn.silu(a) * u).astype(BF16), wsd_ref[...], preferred_element_type=F32)

    for k in range(TOP_K):
        pltpu.make_async_copy(ys_ref.at[pl.ds(0, tm * SUBLANES)], buf_s.at[slot, k], sem.at[slot]).wait()

    gate = gate_ref[...]
    routed = jnp.zeros_like(shared)
    for k in range(TOP_K):
        routed = routed + _unpack_pair_f32(_load_row_tiles(buf_s.at[slot, k], tm)) * gate[:, k:k + 1]
    o_ref[...] = _layer_norm(DN_ALPHA * h1_ref[...] + routed + shared, g2_ref[...], b2_ref[...])


def combine_shared_ln(dest_flat, h1, h1p_tiles, gate_nk, ys_tiles, wsg, wsu, wsd, g2, b2, tm=128):
    n, d = h1.shape
    ff = wsg.shape[1]
    row = lambda i, dest: (i, 0)
    const = lambda i, dest: (0, 0)
    return pl.pallas_call(
        _combine_kernel,
        grid_spec=pltpu.PrefetchScalarGridSpec(
            num_scalar_prefetch=1,
            grid=(n // tm,),
            in_specs=[pl.BlockSpec((tm, d), row), pl.BlockSpec((tm * SUBLANES, LANES), row),
                      pl.BlockSpec((tm, TOP_K), row),
                      pl.BlockSpec((d, ff), const), pl.BlockSpec((d, ff), const), pl.BlockSpec((ff, d), const),
                      pl.BlockSpec((1, d), const), pl.BlockSpec((1, d), const),
                      pl.BlockSpec(memory_space=pl.ANY)],
            out_specs=pl.BlockSpec((tm, d), row),
            scratch_shapes=[pltpu.VMEM((2, TOP_K, tm * SUBLANES, LANES), U32), pltpu.SemaphoreType.DMA((2,))],
        ),
        out_shape=jax.ShapeDtypeStruct((n, d), F32),
        compiler_params=_cparams(("arbitrary",)),
        name="combine_shared_ln",
    )(dest_flat, h1, h1p_tiles, gate_nk, wsg, wsu, wsd, g2.reshape(1, d), b2.reshape(1, d), ys_tiles)


def kernel(x, ln_emb_g, ln_emb_b, w_in, lb_logits, hg_norm_w, conv_w, conv_b, rg_wa, rg_ba, rg_wx, rg_bx,
           rg_lam, w_out, ln1_g, ln1_b, w_router, router_bias, w_gate, w_up, w_down, ws_gate, ws_up, ws_down,
           ln2_g, ln2_b):
    batch, seq, d = x.shape
    assert d == 2 * SUBLANES * LANES, "packed rows must fill one 8x128 tile"
    n_tok = batch * seq
    layer = 0
    x2 = x.reshape(n_tok, d)
    lb = jnp.cumsum(jax.nn.softmax(lb_logits.astype(F32), axis=1), axis=1)

    proj = ln_inproj(x2, ln_emb_g, ln_emb_b, w_in[layer].astype(BF16))
    y_hg = hgrn2_mixer(proj, lb[0, layer], lb[1, layer], hg_norm_w[layer], batch, seq)
    y_rg = rglru_mixer(proj, conv_w[layer], conv_b[layer], rg_wa[layer], rg_ba[layer], rg_wx[layer],
                       rg_bx[layer], rg_lam[layer], batch, seq)
    h1, h1p_tiles, logits_t = outproj_ln_router(
        y_hg, y_rg, x2, ln_emb_g, ln_emb_b, w_out[layer].astype(BF16), ln1_g[layer], ln1_b[layer],
        w_router[layer].T.astype(BF16))

    idx_t, gate_t = route_topk(logits_t, router_bias[layer])
    rank_t, counts = expert_ranks(idx_t)
    counts = counts[:, 0]

    n_assign = n_tok * TOP_K
    n_blocks = -(-(n_assign + N_EXPERTS * (MOE_BLOCK - 1)) // MOE_BLOCK)
    n_rows = n_blocks * MOE_BLOCK
    padded = (counts + MOE_BLOCK - 1) // MOE_BLOCK * MOE_BLOCK
    padded_end = jnp.cumsum(padded)
    padded_start = padded_end - padded
    n_used_blocks = (padded_end[-1:] // MOE_BLOCK).astype(I32)
    blk_row = jnp.minimum(jnp.arange(n_blocks, dtype=I32), n_used_blocks[0] - 1) * MOE_BLOCK
    block_expert = jnp.minimum(
        jnp.sum((padded_end[None, :] <= blk_row[:, None]).astype(I32), axis=1), N_EXPERTS - 1)
    e_ids = jnp.arange(N_EXPERTS, dtype=I32)
    active = counts > 0
    later = lax.cummin(jnp.where(active, e_ids, N_EXPERTS)[::-1])[::-1]
    nxt = jnp.concatenate([later[1:], jnp.full((1,), N_EXPERTS, I32)])
    next1 = jnp.where(nxt < N_EXPERTS, nxt, -1).astype(I32)
    hot1 = (next1[:, None] == e_ids[None, :]).astype(I32)
    next2 = jnp.sum(hot1 * (next1 + 1)[None, :], axis=1) - 1
    ordinal = jnp.cumsum(active.astype(I32)) - 1
    onehot = (block_expert[:, None] == e_ids[None, :]).astype(I32)
    block_next = jnp.sum(onehot * next1[None, :], axis=1)
    block_next2 = jnp.sum(onehot * next2[None, :], axis=1)
    block_slot = jnp.sum(onehot * (ordinal % WEIGHT_SLOTS)[None, :], axis=1)
    block_count = jnp.clip(jnp.sum(onehot * (counts + padded_start)[None, :], axis=1) - blk_row, 1, MOE_BLOCK)

    dest_flat = assignment_rows(idx_t, rank_t, padded_start).reshape(-1)
    row_tok = row_tokens(dest_flat, n_rows)

    ys_tiles = grouped_experts(h1p_tiles, row_tok, block_expert, block_next, block_next2, block_slot,
                               block_count, n_used_blocks, w_gate[layer], w_up[layer], w_down[layer])
    out = combine_shared_ln(dest_flat, h1, h1p_tiles, gate_t.T, ys_tiles,
                            ws_gate[layer].astype(BF16), ws_up[layer].astype(BF16),
                            ws_down[layer].astype(BF16), ln2_g[layer], ln2_b[layer])
    return out.reshape(batch, seq, d)
```

```python
import functools

import numpy as np
import jax
import jax.numpy as jnp
from jax import lax
from jax.experimental import pallas as pl
from jax.experimental.pallas import tpu as pltpu

F32 = jnp.float32
BF16 = jnp.bfloat16
U32 = jnp.uint32
I32 = jnp.int32

D_MODEL = 2048
DEPTH = 1
HG_KEY_DIM = 128
HG_VAL_DIM = 128
HG_WIDTH = D_MODEL // 2
HG_HEADS = HG_WIDTH // HG_VAL_DIM
HG_KEY_WIDTH = HG_HEADS * HG_KEY_DIM
RG_WIDTH = D_MODEL - HG_WIDTH
RG_BLOCKS = 8
RG_BLOCK_DIM = RG_WIDTH // RG_BLOCKS
RG_CONV = 4
RG_C = 8.0
MIX_WIDTH = HG_WIDTH + RG_WIDTH
IN_WIDTHS = (HG_KEY_WIDTH, HG_KEY_WIDTH, HG_KEY_WIDTH, HG_WIDTH, HG_WIDTH, RG_WIDTH, RG_WIDTH)
IN_WIDTH = sum(IN_WIDTHS)
N_EXPERTS = 256
TOP_K = 8
N_GROUPS = 8
TOPK_GROUPS = 4
GROUP_SIZE = N_EXPERTS // N_GROUPS
EXPERT_FF = 512
SHARED_FF = 512
ROUTE_SCALE = 2.5
MOE_BLOCK = 128
DN_ALPHA = (2 * DEPTH) ** 0.25
LN_EPS = 1e-5
RMS_EPS = 1e-6

LANES = 128
SUBLANES = 8
VMEM_LIMIT_BYTES = 56 * 1024 * 1024

HG_CHUNK = 128
HG_SUB = 8
HG_LEVELS = (16, 32, 64, 128)


def _cparams(sem, vmem=VMEM_LIMIT_BYTES):
    return pltpu.CompilerParams(dimension_semantics=sem, vmem_limit_bytes=vmem)


def _layer_norm(x, g, b):
    mu = jnp.mean(x, axis=-1, keepdims=True)
    xc = x - mu
    var = jnp.mean(xc * xc, axis=-1, keepdims=True)
    return xc * lax.rsqrt(var + LN_EPS) * g + b


def _pack_bf16_pair(x):
    n = x.shape[1] // 2
    bits = lax.bitcast_convert_type(x.astype(BF16).astype(F32), U32)
    return (bits[:, :n] >> 16) | (bits[:, n:] & jnp.uint32(0xFFFF0000))


def _unpack_pair_f32(w):
    lo = lax.bitcast_convert_type(w << 16, F32)
    hi = lax.bitcast_convert_type(w & jnp.uint32(0xFFFF0000), F32)
    return jnp.concatenate([lo, hi], axis=1)


def _unpack_bf16_pair(w):
    return _unpack_pair_f32(w).astype(BF16)


def _store_row_tiles(ref, packed):
    m = packed.shape[0]
    for s in range(SUBLANES):
        ref[pl.ds(s, m, stride=SUBLANES), :] = packed[:, s * LANES:(s + 1) * LANES]


def _load_row_tiles(ref, m):
    return jnp.concatenate([ref[pl.ds(s, m, stride=SUBLANES), :] for s in range(SUBLANES)], axis=1)


def _ln_inproj_kernel(x_ref, g_ref, b_ref, w_ref, o_ref, hb_ref):
    @pl.when(pl.program_id(1) == 0)
    def _():
        hb_ref[...] = _layer_norm(x_ref[...], g_ref[...], b_ref[...]).astype(BF16)

    o_ref[...] = jnp.dot(hb_ref[...], w_ref[...].astype(BF16), preferred_element_type=F32).astype(o_ref.dtype)


def ln_inproj(x2, g, b, w_in, tm=1024, tn=1024):
    n, d = x2.shape
    width = w_in.shape[1]
    return pl.pallas_call(
        _ln_inproj_kernel,
        grid=(n // tm, width // tn),
        in_specs=[
            pl.BlockSpec((tm, d), lambda i, j: (i, 0)),
            pl.BlockSpec((1, d), lambda i, j: (0, 0)),
            pl.BlockSpec((1, d), lambda i, j: (0, 0)),
            pl.BlockSpec((d, tn), lambda i, j: (0, j)),
        ],
        out_specs=pl.BlockSpec((tm, tn), lambda i, j: (i, j)),
        out_shape=jax.ShapeDtypeStruct((n, width), BF16),
        scratch_shapes=[pltpu.VMEM((tm, d), BF16)],
        compiler_params=_cparams(("parallel", "arbitrary")),
        name="ln_inproj",
    )(x2, g.reshape(1, d), b.reshape(1, d), w_in)


def _hgrn2_constants():
    c = HG_CHUNK
    t = np.arange(c)[:, None]
    u = np.arange(c)[None, :]
    fwd = [(u <= t), (u > t)]
    bwd = [(u >= t), (u < t)]
    for m in HG_LEVELS:
        start = (t // m) * m
        mid = start + m // 2 - 1
        second = t > mid
        fwd.append(np.where(second, (u > mid) & (u <= t), (u > t) & (u <= mid)))
        bwd.append(np.where(second, (u > mid) & (u < t), (u >= t) & (u <= mid)))
    fwd = np.concatenate(fwd, axis=0).astype(np.float32)
    bwd = np.concatenate(bwd, axis=0).astype(np.float32)
    fwd = np.concatenate([fwd, fwd], axis=1)
    bwd = np.concatenate([bwd, bwd], axis=1)
    lvl = np.full((c, c), -1, np.int32)
    tt = np.arange(c)[:, None]
    ss = np.arange(c)[None, :]
    for li in reversed(range(len(HG_LEVELS))):
        m = HG_LEVELS[li]
        lvl = np.where((tt // m) == (ss // m), li, lvl)
    lvl = np.where((tt // HG_SUB) == (ss // HG_SUB), -1, lvl).astype(np.int32)
    return fwd, bwd, lvl


def _hgrn2_kernel(q_ref, zf_ref, zb_ref, v_ref, g_ref, lbf_ref, lbb_ref, nw_ref,
                  mf_ref, mb_ref, lvl_ref, o_ref,
                  qs_s, v_s, lf_s, kf_s, lb_s, kb_s, cf_s, cb_s,
                  qif_s, kif_s, qib_s, kib_s, df_s, db_s, acc_s, stf_s, stb_s):
    seq = q_ref.shape[0]
    c = HG_CHUNK
    n_chunks = seq // c
    n_lvl = len(HG_LEVELS)

    qs_s[...] = jax.nn.silu(q_ref[...].astype(F32))
    v_s[...] = v_ref[...].astype(F32)
    for z_ref, lb_ref, lf_out, k_out in ((zf_ref, lbf_ref, lf_s, kf_s), (zb_ref, lbb_ref, lb_s, kb_s)):
        z = z_ref[...].astype(F32)
        lb = lb_ref[...]
        sig = jax.nn.sigmoid(z)
        lf_out[...] = jnp.log(lb + (1.0 - lb) * sig)
        k_out[...] = (1.0 - lb) * (1.0 - sig)

    ones_mat = jnp.ones((LANES, LANES), BF16)
    lvl = lvl_ref[...]
    row_c = lax.broadcasted_iota(I32, (c, LANES), 0)
    lane8 = lax.broadcasted_iota(I32, (HG_SUB, LANES), 1)
    sub8 = lax.broadcasted_iota(I32, (HG_SUB, LANES), 0)
    n_stack = (2 + n_lvl) * c

    def intra(n, carry):
        r0 = pl.multiple_of(n * c, c)
        rows = pl.ds(r0, c)
        qs = qs_s[rows, :]
        kf = kf_s[rows, :]
        kb = kb_s[rows, :]
        hl = []
        for lfs in (lf_s, lb_s):
            lf = lfs[rows, :]
            hi = lf.astype(BF16)
            hl.append(jnp.concatenate([hi, (lf - hi.astype(F32)).astype(BF16)], axis=0))
        ef = jnp.dot(mf_ref[...], hl[0], preferred_element_type=F32)
        eb = jnp.dot(mb_ref[...], hl[1], preferred_element_type=F32)
        for (e, kk, cs, qi_s, ki_s, d_s, tot_row) in ((ef, kf, cf_s, qif_s, kif_s, df_s, c - 1),
                                                    (eb, kb, cb_s, qib_s, kib_s, db_s, 0)):
            cum = e[0:c]
            cs[rows, :] = cum
            qi_s[rows, :] = (qs * jnp.exp(cum)).astype(BF16)
            ki_s[rows, :] = (kk * jnp.exp(e[c:2 * c])).astype(BF16)
            d_s[pl.ds(n, 1), :] = jnp.exp(cum[tot_row:tot_row + 1, :])

        a_mat = jnp.zeros((c, c), F32)
        for li, m in enumerate(HG_LEVELS):
            second = (row_c & (m // 2)) != 0
            wf = jnp.exp(ef[(2 + li) * c:(3 + li) * c])
            wb = jnp.exp(eb[(2 + li) * c:(3 + li) * c])
            qcat = jnp.concatenate([qs * jnp.where(second, wf, 0.0), qs * jnp.where(second, 0.0, wb)], axis=1)
            kcat = jnp.concatenate([kf * jnp.where(second, 0.0, wf), kb * jnp.where(second, wb, 0.0)], axis=1)
            sc = lax.dot_general(qcat.astype(BF16), kcat.astype(BF16),
                                 (((1,), (1,)), ((), ())), preferred_element_type=F32)
            a_mat = jnp.where(lvl == li, sc, a_mat)

        p_list = []
        for j in range(c // HG_SUB):
            rj = r0 + j * HG_SUB
            sl = pl.ds(rj, HG_SUB)
            cf = cf_s[sl, :]
            cb = cb_s[sl, :]
            qj = qs_s[sl, :]
            for s in range(HG_SUB):
                src = pl.ds(rj + s, 1)
                df = jnp.where(sub8 >= s, cf - cf_s[src, :], -jnp.inf)
                db = jnp.where(sub8 <= s, cb - cb_s[src, :], -jnp.inf)
                p_list.append(qj * (kf_s[src, :] * jnp.exp(df) + kb_s[src, :] * jnp.exp(db)))
        p_all = jnp.concatenate(p_list, axis=0).astype(BF16)
        red = jnp.dot(p_all, ones_mat, preferred_element_type=F32)
        diag_rows = []
        for j in range(c // HG_SUB):
            blk = jnp.zeros((HG_SUB, LANES), F32)
            for s in range(HG_SUB):
                idx = (j * HG_SUB + s) * HG_SUB
                blk = jnp.where(lane8 == j * HG_SUB + s, red[idx:idx + HG_SUB], blk)
            diag_rows.append(blk)
        a_mat = a_mat + jnp.concatenate(diag_rows, axis=0)
        acc_s[rows, :] = jnp.dot(a_mat.astype(BF16), v_s[rows, :].astype(BF16), preferred_element_type=F32)
        return carry

    lax.fori_loop(0, n_chunks, intra, 0, unroll=4)

    stf_s[...] = jnp.zeros_like(stf_s)
    stb_s[...] = jnp.zeros_like(stb_s)

    def inter(n, carry):
        for (st_s, qi_s, ki_s, d_s, idx) in ((stf_s, qif_s, kif_s, df_s, n),
                                             (stb_s, qib_s, kib_s, db_s, n_chunks - 1 - n)):
            rows = pl.ds(pl.multiple_of(idx * c, c), c)
            st = st_s[...]
            acc_s[rows, :] += lax.dot_general(qi_s[rows, :], st.astype(BF16),
                                              (((1,), (1,)), ((), ())), preferred_element_type=F32)
            upd = lax.dot_general(v_s[rows, :].astype(BF16), ki_s[rows, :],
                                  (((0,), (0,)), ((), ())), preferred_element_type=F32)
            st_s[...] = st * d_s[pl.ds(idx, 1), :] + upd
        return carry

    lax.fori_loop(0, n_chunks, inter, 0, unroll=4)

    o = acc_s[...]
    o = o * lax.rsqrt(jnp.mean(o * o, axis=-1, keepdims=True) + RMS_EPS) * nw_ref[...]
    o_ref[...] = (o * jax.nn.silu(g_ref[...].astype(F32))).astype(o_ref.dtype)


def hgrn2_mixer(proj, lb_f, lb_b, norm_w, batch, seq):
    n = proj.shape[0]
    heads = HG_HEADS
    c = HG_CHUNK
    n_chunks = seq // c
    mf, mb, lvl = _hgrn2_constants()
    n_stack = mf.shape[0]

    def col(off):
        return pl.BlockSpec((seq, LANES), lambda b, h, off=off: (b, off + h))

    def per_head(arr):
        return pl.BlockSpec((1, LANES), lambda b, h: (0, h))

    const2 = lambda b, h: (0, 0)
    seq_f32 = pltpu.VMEM((seq, LANES), F32)
    seq_bf16 = pltpu.VMEM((seq, LANES), BF16)
    dec = pltpu.VMEM((n_chunks, LANES), F32)
    state = pltpu.VMEM((LANES, LANES), F32)
    return pl.pallas_call(
        _hgrn2_kernel,
        grid=(batch, heads),
        in_specs=[col(0), col(heads), col(2 * heads), col(3 * heads), col(4 * heads),
                  per_head(lb_f), per_head(lb_b),
                  pl.BlockSpec((1, LANES), const2),
                  pl.BlockSpec((n_stack, 2 * c), const2),
                  pl.BlockSpec((n_stack, 2 * c), const2),
                  pl.BlockSpec((c, c), const2)],
        out_specs=pl.BlockSpec((seq, LANES), lambda b, h: (b, h)),
        out_shape=jax.ShapeDtypeStruct((n, HG_WIDTH), BF16),
        scratch_shapes=[seq_f32] * 8 + [seq_bf16] * 4 + [dec, dec, seq_f32, state, state],
        compiler_params=_cparams(("parallel", "parallel")),
        name="hgrn2_mixer",
    )(proj, proj, proj, proj, proj, lb_f.reshape(1, -1), lb_b.reshape(1, -1), norm_w.reshape(1, -1),
      jnp.asarray(mf, BF16), jnp.asarray(mb, BF16), jnp.asarray(lvl))


def _shift_rows(x, d, fill, reverse):
    pad = jnp.full((d, x.shape[1]), fill, x.dtype)
    if reverse:
        return jnp.concatenate([x[d:], pad], axis=0)
    return jnp.concatenate([pad, x[:-d]], axis=0)


def _linear_scan(a, u, reverse):
    seq = a.shape[0]
    d = 1
    while d < seq:
        u = u + a * _shift_rows(u, d, 0.0, reverse)
        if d * 2 < seq:
            a = a * _shift_rows(a, d, 1.0, reverse)
        d *= 2
    return u


def _rglru_kernel(xr_ref, gr_ref, cw_ref, cb_ref, wa_ref, ba_ref, wx_ref, bx_ref, lam_ref, o_ref, xp_s):
    seq = xr_ref.shape[0]
    pad = SUBLANES
    pad_l = RG_CONV // 2
    xp_s[0:pad, :] = jnp.zeros((pad, LANES), F32)
    xp_s[pad + seq:, :] = jnp.zeros((pad, LANES), F32)
    xp_s[pad:pad + seq, :] = xr_ref[...].astype(F32)
    xc = cb_ref[...]
    for j in range(RG_CONV):
        off = pad + j - pad_l
        xc = xc + xp_s[off:off + seq, :] * cw_ref[j:j + 1, :]
    xcb = xc.astype(BF16)
    h = None
    for d in range(2):
        r = jax.nn.sigmoid(jnp.dot(xcb, wa_ref[d, 0], preferred_element_type=F32) + ba_ref[d:d + 1, :])
        i = jax.nn.sigmoid(jnp.dot(xcb, wx_ref[d, 0], preferred_element_type=F32) + bx_ref[d:d + 1, :])
        log_a = (-RG_C) * r * jax.nn.softplus(-lam_ref[d:d + 1, :])
        a = jnp.exp(log_a)
        u = jnp.sqrt(-jnp.tanh(log_a) * (a * a + 1.0)) * (i * xc)
        hd = _linear_scan(a, u, reverse=(d == 1))
        h = hd if h is None else h + hd
    o_ref[...] = (h * jax.nn.gelu(gr_ref[...].astype(F32))).astype(o_ref.dtype)


def rglru_mixer(proj, conv_w, conv_b, wa, ba, wx, bx, lam, batch, seq):
    n = proj.shape[0]
    x_off = (3 * HG_KEY_WIDTH + 2 * HG_WIDTH) // LANES
    g_off = x_off + RG_WIDTH // LANES
    blk = lambda b, j: (0, j)
    return pl.pallas_call(
        _rglru_kernel,
        grid=(batch, RG_BLOCKS),
        in_specs=[
            pl.BlockSpec((seq, LANES), lambda b, j: (b, x_off + j)),
            pl.BlockSpec((seq, LANES), lambda b, j: (b, g_off + j)),
            pl.BlockSpec((RG_CONV, LANES), blk),
            pl.BlockSpec((1, LANES), blk),
            pl.BlockSpec((2, 1, RG_BLOCK_DIM, RG_BLOCK_DIM), lambda b, j: (0, j, 0, 0)),
            pl.BlockSpec((2, LANES), blk),
            pl.BlockSpec((2, 1, RG_BLOCK_DIM, RG_BLOCK_DIM), lambda b, j: (0, j, 0, 0)),
            pl.BlockSpec((2, LANES), blk),
            pl.BlockSpec((2, LANES), blk),
        ],
        out_specs=pl.BlockSpec((seq, LANES), lambda b, j: (b, j)),
        out_shape=jax.ShapeDtypeStruct((n, RG_WIDTH), BF16),
        scratch_shapes=[pltpu.VMEM((seq + 2 * SUBLANES, LANES), F32)],
        compiler_params=_cparams(("parallel", "parallel")),
        name="rglru_mixer",
    )(proj, proj, conv_w, conv_b.reshape(1, -1), wa.astype(BF16), ba, wx.astype(BF16), bx, lam)


def _outproj_kernel(yh_ref, yr_ref, x_ref, g0_ref, b0_ref, wo_ref, g1_ref, b1_ref, wr_ref,
                    h1_ref, h1p_ref, lg_ref):
    half = yh_ref.shape[1]
    h0 = _layer_norm(x_ref[...], g0_ref[...], b0_ref[...])
    mix = jnp.dot(yh_ref[...], wo_ref[0:half, :].astype(BF16), preferred_element_type=F32)
    mix = mix + jnp.dot(yr_ref[...], wo_ref[half:, :].astype(BF16), preferred_element_type=F32)
    h1 = _layer_norm(DN_ALPHA * h0 + mix, g1_ref[...], b1_ref[...])
    h1_ref[...] = h1
    _store_row_tiles(h1p_ref, _pack_bf16_pair(h1))
    lg_ref[...] = lax.dot_general(wr_ref[...], h1.astype(BF16), (((1,), (1,)), ((), ())),
                                  preferred_element_type=F32)


def outproj_ln_router(y_hg, y_rg, x2, g0, b0, w_out, g1, b1, w_router_t_bf16, tm=256):
    n, d = x2.shape
    half = y_hg.shape[1]
    n_e = w_router_t_bf16.shape[0]
    row = lambda i: (i, 0)
    const = lambda i: (0, 0)
    vec = pl.BlockSpec((1, d), const)
    return pl.pallas_call(
        _outproj_kernel,
        grid=(n // tm,),
        in_specs=[pl.BlockSpec((tm, half), row), pl.BlockSpec((tm, half), row), pl.BlockSpec((tm, d), row),
                  vec, vec, pl.BlockSpec((2 * half, d), const), vec, vec,
                  pl.BlockSpec((n_e, d), const)],
        out_specs=[pl.BlockSpec((tm, d), row), pl.BlockSpec((tm * SUBLANES, LANES), row),
                   pl.BlockSpec((n_e, tm), lambda i: (0, i))],
        out_shape=[jax.ShapeDtypeStruct((n, d), F32), jax.ShapeDtypeStruct((n * SUBLANES, LANES), U32),
                   jax.ShapeDtypeStruct((n_e, n), F32)],
        compiler_params=_cparams(("parallel",)),
        name="outproj_ln_router",
    )(y_hg, y_rg, x2, g0.reshape(1, d), b0.reshape(1, d), w_out, g1.reshape(1, d), b1.reshape(1, d),
      w_router_t_bf16)


def _first_argmax(vals, iota, n):
    m = jnp.max(vals, axis=0, keepdims=True)
    first = jnp.min(jnp.where(vals == m, iota, n), axis=0, keepdims=True)
    return m, first


def _route_kernel(lg_ref, bias_ref, idx_ref, gate_ref):
    n_e, tn = lg_ref.shape
    scores = jax.nn.sigmoid(lg_ref[...])
    choice = scores + bias_ref[...]
    neg = -jnp.inf
    gi = lax.broadcasted_iota(I32, (GROUP_SIZE, tn), 0)
    gs_rows = []
    for g in range(N_GROUPS):
        blk = choice[g * GROUP_SIZE:(g + 1) * GROUP_SIZE]
        m1, f1 = _first_argmax(blk, gi, GROUP_SIZE)
        m2 = jnp.max(jnp.where(gi == f1, neg, blk), axis=0, keepdims=True)
        gs_rows.append(m1 + m2)
    gs = jnp.concatenate(gs_rows, axis=0)
    ri = lax.broadcasted_iota(I32, (N_GROUPS, tn), 0)
    gmask = jnp.zeros((N_GROUPS, tn), jnp.bool_)
    for _ in range(TOPK_GROUPS):
        _, f = _first_argmax(gs, ri, N_GROUPS)
        hit = ri == f
        gmask = gmask | hit
        gs = jnp.where(hit, neg, gs)
    gm = jnp.where(gmask, 1.0, 0.0)
    emask = jnp.concatenate(
        [jnp.broadcast_to(gm[g:g + 1], (GROUP_SIZE, tn)) for g in range(N_GROUPS)], axis=0)
    masked = jnp.where(emask > 0.5, choice, neg)
    ei = lax.broadcasted_iota(I32, (n_e, tn), 0)
    idx_rows, gate_rows = [], []
    for _ in range(TOP_K):
        _, f = _first_argmax(masked, ei, n_e)
        hit = ei == f
        idx_rows.append(f)
        gate_rows.append(jnp.sum(jnp.where(hit, scores, 0.0), axis=0, keepdims=True))
        masked = jnp.where(hit, neg, masked)
    gate = jnp.concatenate(gate_rows, axis=0)
    gate = gate / jnp.sum(gate, axis=0, keepdims=True) * ROUTE_SCALE
    idx_ref[...] = jnp.concatenate(idx_rows, axis=0)
    gate_ref[...] = gate


def route_topk(logits_t, bias, tn=512):
    n_e, n = logits_t.shape
    return pl.pallas_call(
        _route_kernel,
        grid=(n // tn,),
        in_specs=[pl.BlockSpec((n_e, tn), lambda i: (0, i)), pl.BlockSpec((n_e, 1), lambda i: (0, 0))],
        out_specs=[pl.BlockSpec((TOP_K, tn), lambda i: (0, i)), pl.BlockSpec((TOP_K, tn), lambda i: (0, i))],
        out_shape=[jax.ShapeDtypeStruct((TOP_K, n), I32), jax.ShapeDtypeStruct((TOP_K, n), F32)],
        compiler_params=_cparams(("parallel",)),
        name="route_topk",
    )(logits_t, bias.reshape(n_e, 1))


def _rank_kernel(idx_ref, su_ref, rank_ref, cnt_ref, carry_s):
    n_e = cnt_ref.shape[0]
    tn = idx_ref.shape[1]

    @pl.when(pl.program_id(0) == 0)
    def _():
        carry_s[...] = jnp.zeros_like(carry_s)

    idx = idx_ref[...]
    ei = lax.broadcasted_iota(I32, (n_e, tn), 0)
    hits = [ei == idx[k:k + 1] for k in range(TOP_K)]
    member = jnp.zeros((n_e, tn), F32)
    for h in hits:
        member = member + jnp.where(h, 1.0, 0.0)
    prefix = jnp.dot(member.astype(BF16), su_ref[...], preferred_element_type=F32) + carry_s[:, 0:1]
    rank_ref[...] = jnp.concatenate(
        [jnp.sum(jnp.where(h, prefix, 0.0), axis=0, keepdims=True) for h in hits], axis=0).astype(I32)
    carry_s[...] = carry_s[...] + jnp.sum(member, axis=1, keepdims=True)
    cnt_ref[...] = carry_s[...].astype(I32)


def expert_ranks(idx_t, tn=512):
    k, n = idx_t.shape
    su = jnp.asarray(np.triu(np.ones((tn, tn), np.float32), 1), BF16)
    return pl.pallas_call(
        _rank_kernel,
        grid=(n // tn,),
        in_specs=[pl.BlockSpec((k, tn), lambda i: (0, i)), pl.BlockSpec((tn, tn), lambda i: (0, 0))],
        out_specs=[pl.BlockSpec((k, tn), lambda i: (0, i)), pl.BlockSpec((N_EXPERTS, LANES), lambda i: (0, 0))],
        out_shape=[jax.ShapeDtypeStruct((k, n), I32), jax.ShapeDtypeStruct((N_EXPERTS, LANES), I32)],
        scratch_shapes=[pltpu.VMEM((N_EXPERTS, LANES), F32)],
        compiler_params=_cparams(("arbitrary",)),
        name="expert_ranks",
    )(idx_t, su)


def _dest_kernel(idx_ref, rank_ref, start_ref, dest_ref):
    n_e = start_ref.shape[0]
    tn = idx_ref.shape[1]
    idx = idx_ref[...]
    ei = lax.broadcasted_iota(I32, (n_e, tn), 0)
    start = start_ref[...]
    base = jnp.concatenate(
        [jnp.sum(jnp.where(ei == idx[k:k + 1], start, 0.0), axis=0, keepdims=True) for k in range(TOP_K)],
        axis=0)
    dest_ref[...] = base.astype(I32) + rank_ref[...]


def assignment_rows(idx_t, rank_t, padded_start, tn=512):
    k, n = idx_t.shape
    blk = pl.BlockSpec((k, tn), lambda i: (0, i))
    return pl.pallas_call(
        _dest_kernel,
        grid=(n // tn,),
        in_specs=[blk, blk, pl.BlockSpec((N_EXPERTS, 1), lambda i: (0, 0))],
        out_specs=blk,
        out_shape=jax.ShapeDtypeStruct((k, n), I32),
        compiler_params=_cparams(("parallel",)),
        name="assignment_rows",
    )(idx_t, rank_t, padded_start.astype(F32).reshape(N_EXPERTS, 1))


WEIGHT_DMA_PRIORITY = 1
WEIGHT_SLOTS = 3
ROW_GROUP = 8
ROW_CHUNK = 8
BLOCKS_PER_STEP = 2


def _rowmap_kernel(dest_ref, tok_ref):
    n_tok = dest_ref.shape[0] // TOP_K

    def body(j, c):
        for k in range(TOP_K):
            tok_ref[dest_ref[k * n_tok + j]] = j
        return c

    lax.fori_loop(0, n_tok, body, 0, unroll=4)


def row_tokens(dest_flat, n_rows):
    return pl.pallas_call(
        _rowmap_kernel,
        grid_spec=pltpu.PrefetchScalarGridSpec(
            num_scalar_prefetch=1,
            grid=(1,),
            in_specs=[],
            out_specs=pl.BlockSpec(memory_space=pltpu.SMEM),
        ),
        out_shape=jax.ShapeDtypeStruct((n_rows,), I32),
        compiler_params=_cparams(("arbitrary",)),
        name="row_tokens",
    )(dest_flat)


def _experts_kernel(be_ref, nxt_ref, nxt2_ref, slot_ref, cnt_ref, tok_ref, nb_ref,
                    h1p_ref, wg_ref, wu_ref, wd_ref, ys_ref,
                    xbuf, wgb, wub, wdb, gsem, wsem):
    tm = ys_ref.shape[0] // (SUBLANES * BLOCKS_PER_STEP)
    step = pl.program_id(0)
    nb = nb_ref[0]

    def weight_copies(e, slot):
        return (pltpu.make_async_copy(wg_ref.at[e], wgb.at[slot], wsem.at[slot]),
                pltpu.make_async_copy(wu_ref.at[e], wub.at[slot], wsem.at[slot]),
                pltpu.make_async_copy(wd_ref.at[e], wdb.at[slot], wsem.at[slot]))

    def start_weights(e, slot):
        for cp in weight_copies(e, slot):
            cp.start(priority=WEIGHT_DMA_PRIORITY)

    def start_group(grp, slot):
        def one_block(g, carry):
            blk = jnp.minimum(grp * ROW_GROUP + g, nb - 1)
            cnt = cnt_ref[blk]
            last = cnt - 1
            base = blk * tm
            n_chunks = lax.div(cnt + (ROW_CHUNK - 1), ROW_CHUNK)

            def issue(ch, c):
                for j in range(ROW_CHUNK):
                    r = ch * ROW_CHUNK + j
                    t = tok_ref[base + jnp.minimum(r, last)]
                    pltpu.make_async_copy(h1p_ref.at[pl.ds(t * SUBLANES, SUBLANES)],
                                          xbuf.at[slot, pl.ds((g * tm + r) * SUBLANES, SUBLANES)],
                                          gsem.at[slot]).start()
                return c

            lax.fori_loop(0, n_chunks, issue, 0)
            done = n_chunks * ROW_CHUNK
            pad = tm - done
            size = tm // 2
            while size >= ROW_CHUNK:
                @pl.when((pad & size) != 0)
                def _(done=done, size=size):
                    pltpu.make_async_copy(
                        h1p_ref.at[pl.ds(0, size * SUBLANES)],
                        xbuf.at[slot, pl.ds((g * tm + done) * SUBLANES, size * SUBLANES)],
                        gsem.at[slot]).start()
                done = done + jnp.where((pad & size) != 0, size, 0)
                size //= 2
            return carry

        lax.fori_loop(0, ROW_GROUP, one_block, 0)

    def process(i, out_ref):
        e = be_ref[i]
        wslot = slot_ref[i]
        grp = lax.div(i, ROW_GROUP)
        sub = lax.rem(i, ROW_GROUP)
        gslot = lax.rem(grp, 2)

        @pl.when(i == 0)
        def _():
            start_weights(e, wslot)
            nxt = nxt_ref[0]

            @pl.when(nxt >= 0)
            def _():
                start_weights(nxt, lax.rem(wslot + 1, WEIGHT_SLOTS))

            start_group(0, 0)

        @pl.when((i == 0) | (e != be_ref[jnp.maximum(i - 1, 0)]))
        def _():
            for cp in weight_copies(e, wslot):
                cp.wait()
            nxt2 = nxt2_ref[i]

            @pl.when(nxt2 >= 0)
            def _():
                start_weights(nxt2, lax.rem(wslot + 2, WEIGHT_SLOTS))

        @pl.when(sub == 0)
        def _():
            pltpu.make_async_copy(h1p_ref.at[pl.ds(0, ROW_GROUP * tm * SUBLANES)], xbuf.at[gslot],
                                  gsem.at[gslot]).wait()

            @pl.when((grp + 1) * ROW_GROUP < nb)
            def _():
                start_group(grp + 1, 1 - gslot)

        row0 = pl.multiple_of(sub * (tm * SUBLANES), tm * SUBLANES)
        x = _unpack_bf16_pair(_load_row_tiles(xbuf.at[gslot, pl.ds(row0, tm * SUBLANES)], tm))
        a = jnp.dot(x, wgb[wslot].astype(BF16), preferred_element_type=F32)
        u = jnp.dot(x, wub[wslot].astype(BF16), preferred_element_type=F32)
        hid = (jax.nn.silu(a) * u).astype(BF16)
        y = jnp.dot(hid, wdb[wslot].astype(BF16), preferred_element_type=F32)
        _store_row_tiles(out_ref, _pack_bf16_pair(y))

    for h in range(BLOCKS_PER_STEP):
        blk = step * BLOCKS_PER_STEP + h

        @pl.when(blk < nb)
        def _(blk=blk, h=h):
            process(blk, ys_ref.at[pl.ds(h * tm * SUBLANES, tm * SUBLANES)])


def grouped_experts(h1p_tiles, row_tok, block_expert, block_next, block_next2, block_slot, block_count,
                    n_blocks_used, w_gate, w_up, w_down, tm=MOE_BLOCK):
    n_rows = row_tok.shape[0]
    n_e, d, ff = w_gate.shape
    n_blocks = n_rows // tm
    assert n_blocks % BLOCKS_PER_STEP == 0
    bps = BLOCKS_PER_STEP
    rowmap = lambda i, be, nx, nx2, sl, cn, tok, nb: (jnp.minimum(i, jnp.maximum(nb[0] - 1, 0) // bps), 0)
    hbm = pl.BlockSpec(memory_space=pl.ANY)
    return pl.pallas_call(
        _experts_kernel,
        grid_spec=pltpu.PrefetchScalarGridSpec(
            num_scalar_prefetch=7,
            grid=(n_blocks // bps,),
            in_specs=[hbm, hbm, hbm, hbm],
            out_specs=pl.BlockSpec((bps * tm * SUBLANES, LANES), rowmap),
            scratch_shapes=[pltpu.VMEM((2, ROW_GROUP * tm * SUBLANES, LANES), U32),
                            pltpu.VMEM((WEIGHT_SLOTS, d, ff), F32), pltpu.VMEM((WEIGHT_SLOTS, d, ff), F32),
                            pltpu.VMEM((WEIGHT_SLOTS, ff, d), F32),
                            pltpu.SemaphoreType.DMA((2,)), pltpu.SemaphoreType.DMA((WEIGHT_SLOTS,))],
        ),
        out_shape=jax.ShapeDtypeStruct((n_rows * SUBLANES, LANES), U32),
        compiler_params=_cparams(("arbitrary",)),
        name="grouped_experts",
    )(block_expert, block_next, block_next2, block_slot, block_count, row_tok, n_blocks_used,
      h1p_tiles, w_gate, w_up, w_down)


def _combine_kernel(dest_ref, h1_ref, h1p_ref, gate_ref, wsg_ref, wsu_ref, wsd_ref, g2_ref, b2_ref, ys_ref,
                    o_ref, buf_s, sem):
    tm = h1_ref.shape[0]
    i = pl.program_id(0)
    n_steps = pl.num_programs(0)
    n_tok = n_steps * tm
    slot = lax.rem(i, 2)

    def start_rows(step, sl):
        def issue(r, c):
            for k in range(TOP_K):
                row = dest_ref[k * n_tok + step * tm + r]
                pltpu.make_async_copy(ys_ref.at[pl.ds(row * SUBLANES, SUBLANES)],
                                      buf_s.at[sl, k, pl.ds(r * SUBLANES, SUBLANES)], sem.at[sl]).start()
            return c

        lax.fori_loop(0, tm, issue, 0, unroll=2)

    @pl.when(i == 0)
    def _():
        start_rows(0, 0)

    @pl.when(i + 1 < n_steps)
    def _():
        start_rows(i + 1, 1 - slot)

    xb = _unpack_bf16_pair(_load_row_tiles(h1p_ref, tm))
    a = jnp.dot(xb, wsg_ref[...].astype(BF16), preferred_element_type=F32)
    u = jnp.dot(xb, wsu_ref[...].astype(BF16), preferred_element_type=F32)
    shared = jnp.dot((jax.nn.silu(a) * u).astype(BF16), wsd_ref[...].astype(BF16),
                     preferred_element_type=F32)

    for k in range(TOP_K):
        pltpu.make_async_copy(ys_ref.at[pl.ds(0, tm * SUBLANES)], buf_s.at[slot, k], sem.at[slot]).wait()

    gate = gate_ref[...]
    routed = jnp.zeros_like(shared)
    for k in range(TOP_K):
        routed = routed + _unpack_pair_f32(_load_row_tiles(buf_s.at[slot, k], tm)) * gate[:, k:k + 1]
    o_ref[...] = _layer_norm(DN_ALPHA * h1_ref[...] + routed + shared, g2_ref[...], b2_ref[...])


def combine_shared_ln(dest_flat, h1, h1p_tiles, gate_nk, ys_tiles, wsg, wsu, wsd, g2, b2, tm=128):
    n, d = h1.shape
    ff = wsg.shape[1]
    row = lambda i, dest: (i, 0)
    const = lambda i, dest: (0, 0)
    return pl.pallas_call(
        _combine_kernel,
        grid_spec=pltpu.PrefetchScalarGridSpec(
            num_scalar_prefetch=1,
            grid=(n // tm,),
            in_specs=[pl.BlockSpec((tm, d), row), pl.BlockSpec((tm * SUBLANES, LANES), row),
                      pl.BlockSpec((tm, TOP_K), row),
                      pl.BlockSpec((d, ff), const), pl.BlockSpec((d, ff), const), pl.BlockSpec((ff, d), const),
                      pl.BlockSpec((1, d), const), pl.BlockSpec((1, d), const),
                      pl.BlockSpec(memory_space=pl.ANY)],
            out_specs=pl.BlockSpec((tm, d), row),
            scratch_shapes=[pltpu.VMEM((2, TOP_K, tm * SUBLANES, LANES), U32), pltpu.SemaphoreType.DMA((2,))],
        ),
        out_shape=jax.ShapeDtypeStruct((n, d), F32),
        compiler_params=_cparams(("arbitrary",)),
        name="combine_shared_ln",
    )(dest_flat, h1, h1p_tiles, gate_nk, wsg, wsu, wsd, g2.reshape(1, d), b2.reshape(1, d), ys_tiles)


def kernel(x, ln_emb_g, ln_emb_b, w_in, lb_logits, hg_norm_w, conv_w, conv_b, rg_wa, rg_ba, rg_wx, rg_bx,
           rg_lam, w_out, ln1_g, ln1_b, w_router, router_bias, w_gate, w_up, w_down, ws_gate, ws_up, ws_down,
           ln2_g, ln2_b):
    batch, seq, d = x.shape
    assert d == 2 * SUBLANES * LANES, "packed rows must fill one 8x128 tile"
    n_tok = batch * seq
    layer = 0
    x2 = x.reshape(n_tok, d)
    lb = jnp.cumsum(jax.nn.softmax(lb_logits.astype(F32), axis=1), axis=1)

    proj = ln_inproj(x2, ln_emb_g, ln_emb_b, w_in[layer])
    y_hg = hgrn2_mixer(proj, lb[0, layer], lb[1, layer], hg_norm_w[layer], batch, seq)
    y_rg = rglru_mixer(proj, conv_w[layer], conv_b[layer], rg_wa[layer], rg_ba[layer], rg_wx[layer],
                       rg_bx[layer], rg_lam[layer], batch, seq)
    h1, h1p_tiles, logits_t = outproj_ln_router(
        y_hg, y_rg, x2, ln_emb_g, ln_emb_b, w_out[layer], ln1_g[layer], ln1_b[layer],
        w_router[layer].T.astype(BF16))

    idx_t, gate_t = route_topk(logits_t, router_bias[layer])
    rank_t, counts = expert_ranks(idx_t)
    counts = counts[:, 0]

    n_assign = n_tok * TOP_K
    n_blocks = -(-(n_assign + N_EXPERTS * (MOE_BLOCK - 1)) // MOE_BLOCK)
    n_rows = n_blocks * MOE_BLOCK
    padded = (counts + MOE_BLOCK - 1) // MOE_BLOCK * MOE_BLOCK
    padded_end = jnp.cumsum(padded)
    padded_start = padded_end - padded
    n_used_blocks = (padded_end[-1:] // MOE_BLOCK).astype(I32)
    blk_row = jnp.minimum(jnp.arange(n_blocks, dtype=I32), n_used_blocks[0] - 1) * MOE_BLOCK
    block_expert = jnp.minimum(
        jnp.sum((padded_end[None, :] <= blk_row[:, None]).astype(I32), axis=1), N_EXPERTS - 1)
    e_ids = jnp.arange(N_EXPERTS, dtype=I32)
    active = counts > 0
    later = lax.cummin(jnp.where(active, e_ids, N_EXPERTS)[::-1])[::-1]
    nxt = jnp.concatenate([later[1:], jnp.full((1,), N_EXPERTS, I32)])
    next1 = jnp.where(nxt < N_EXPERTS, nxt, -1).astype(I32)
    hot1 = (next1[:, None] == e_ids[None, :]).astype(I32)
    next2 = jnp.sum(hot1 * (next1 + 1)[None, :], axis=1) - 1
    ordinal = jnp.cumsum(active.astype(I32)) - 1
    onehot = (block_expert[:, None] == e_ids[None, :]).astype(I32)
    block_next = jnp.sum(onehot * next1[None, :], axis=1)
    block_next2 = jnp.sum(onehot * next2[None, :], axis=1)
    block_slot = jnp.sum(onehot * (ordinal % WEIGHT_SLOTS)[None, :], axis=1)
    block_count = jnp.clip(jnp.sum(onehot * (counts + padded_start)[None, :], axis=1) - blk_row, 1, MOE_BLOCK)

    dest_flat = assignment_rows(idx_t, rank_t, padded_start).reshape(-1)
    row_tok = row_tokens(dest_flat, n_rows)

    ys_tiles = grouped_experts(h1p_tiles, row_tok, block_expert, block_next, block_next2, block_slot,
                               block_count, n_used_blocks, w_gate[layer], w_up[layer], w_down[layer])
    out = combine_shared_ln(dest_flat, h1, h1p_tiles, gate_t.T, ys_tiles,
                            ws_gate[layer], ws_up[layer], ws_down[layer], ln2_g[layer], ln2_b[layer])
    return out.reshape(batch, seq, d)
```

```python
import functools

import numpy as np
import jax
import jax.numpy as jnp
from jax import lax
from jax.experimental import pallas as pl
from jax.experimental.pallas import tpu as pltpu

F32 = jnp.float32
BF16 = jnp.bfloat16
U32 = jnp.uint32
I32 = jnp.int32

D_MODEL = 2048
DEPTH = 1
HG_KEY_DIM = 128
HG_VAL_DIM = 128
HG_WIDTH = D_MODEL // 2
HG_HEADS = HG_WIDTH // HG_VAL_DIM
HG_KEY_WIDTH = HG_HEADS * HG_KEY_DIM
RG_WIDTH = D_MODEL - HG_WIDTH
RG_BLOCKS = 8
RG_BLOCK_DIM = RG_WIDTH // RG_BLOCKS
RG_CONV = 4
RG_C = 8.0
MIX_WIDTH = HG_WIDTH + RG_WIDTH
IN_WIDTHS = (HG_KEY_WIDTH, HG_KEY_WIDTH, HG_KEY_WIDTH, HG_WIDTH, HG_WIDTH, RG_WIDTH, RG_WIDTH)
IN_WIDTH = sum(IN_WIDTHS)
N_EXPERTS = 256
TOP_K = 8
N_GROUPS = 8
TOPK_GROUPS = 4
GROUP_SIZE = N_EXPERTS // N_GROUPS
EXPERT_FF = 512
SHARED_FF = 512
ROUTE_SCALE = 2.5
MOE_BLOCK = 128
DN_ALPHA = (2 * DEPTH) ** 0.25
LN_EPS = 1e-5
RMS_EPS = 1e-6

LANES = 128
SUBLANES = 8
VMEM_LIMIT_BYTES = 56 * 1024 * 1024

HG_CHUNK = 128
HG_SUB = 8
HG_LEVELS = (16, 32, 64, 128)


def _cparams(sem, vmem=VMEM_LIMIT_BYTES):
    return pltpu.CompilerParams(dimension_semantics=sem, vmem_limit_bytes=vmem)


def _layer_norm(x, g, b):
    mu = jnp.mean(x, axis=-1, keepdims=True)
    xc = x - mu
    var = jnp.mean(xc * xc, axis=-1, keepdims=True)
    return xc * lax.rsqrt(var + LN_EPS) * g + b


def _pack_bf16_pair(x):
    n = x.shape[1] // 2
    bits = lax.bitcast_convert_type(x.astype(BF16).astype(F32), U32)
    return (bits[:, :n] >> 16) | (bits[:, n:] & jnp.uint32(0xFFFF0000))


def _unpack_pair_f32(w):
    lo = lax.bitcast_convert_type(w << 16, F32)
    hi = lax.bitcast_convert_type(w & jnp.uint32(0xFFFF0000), F32)
    return jnp.concatenate([lo, hi], axis=1)


def _unpack_bf16_pair(w):
    return _unpack_pair_f32(w).astype(BF16)


def _store_row_tiles(ref, packed):
    m = packed.shape[0]
    for s in range(SUBLANES):
        ref[pl.ds(s, m, stride=SUBLANES), :] = packed[:, s * LANES:(s + 1) * LANES]


def _load_row_tiles(ref, m):
    return jnp.concatenate([ref[pl.ds(s, m, stride=SUBLANES), :] for s in range(SUBLANES)], axis=1)


def _ln_inproj_kernel(x_ref, g_ref, b_ref, w_ref, o_ref, hb_ref):
    @pl.when(pl.program_id(1) == 0)
    def _():
        hb_ref[...] = _layer_norm(x_ref[...], g_ref[...], b_ref[...]).astype(BF16)

    o_ref[...] = jnp.dot(hb_ref[...], w_ref[...].astype(BF16), preferred_element_type=F32).astype(o_ref.dtype)


def ln_inproj(x2, g, b, w_in, tm=1024, tn=1024):
    n, d = x2.shape
    width = w_in.shape[1]
    return pl.pallas_call(
        _ln_inproj_kernel,
        grid=(n // tm, width // tn),
        in_specs=[
            pl.BlockSpec((tm, d), lambda i, j: (i, 0)),
            pl.BlockSpec((1, d), lambda i, j: (0, 0)),
            pl.BlockSpec((1, d), lambda i, j: (0, 0)),
            pl.BlockSpec((d, tn), lambda i, j: (0, j)),
        ],
        out_specs=pl.BlockSpec((tm, tn), lambda i, j: (i, j)),
        out_shape=jax.ShapeDtypeStruct((n, width), BF16),
        scratch_shapes=[pltpu.VMEM((tm, d), BF16)],
        compiler_params=_cparams(("parallel", "arbitrary")),
        name="ln_inproj",
    )(x2, g.reshape(1, d), b.reshape(1, d), w_in)


def _hgrn2_constants():
    c = HG_CHUNK
    t = np.arange(c)[:, None]
    u = np.arange(c)[None, :]
    fwd = [(u <= t), (u > t)]
    bwd = [(u >= t), (u < t)]
    for m in HG_LEVELS:
        start = (t // m) * m
        mid = start + m // 2 - 1
        second = t > mid
        fwd.append(np.where(second, (u > mid) & (u <= t), (u > t) & (u <= mid)))
        bwd.append(np.where(second, (u > mid) & (u < t), (u >= t) & (u <= mid)))
    fwd = np.concatenate(fwd, axis=0).astype(np.float32)
    bwd = np.concatenate(bwd, axis=0).astype(np.float32)
    fwd = np.concatenate([fwd, fwd], axis=1)
    bwd = np.concatenate([bwd, bwd], axis=1)
    lvl = np.full((c, c), -1, np.int32)
    tt = np.arange(c)[:, None]
    ss = np.arange(c)[None, :]
    for li in reversed(range(len(HG_LEVELS))):
        m = HG_LEVELS[li]
        lvl = np.where((tt // m) == (ss // m), li, lvl)
    lvl = np.where((tt // HG_SUB) == (ss // HG_SUB), -1, lvl).astype(np.int32)
    return fwd, bwd, lvl


def _hgrn2_kernel(q_ref, zf_ref, zb_ref, v_ref, g_ref, lbf_ref, lbb_ref, nw_ref,
                  mf_ref, mb_ref, lvl_ref, o_ref,
                  qs_s, v_s, lf_s, kf_s, lb_s, kb_s, cf_s, cb_s,
                  qif_s, kif_s, qib_s, kib_s, df_s, db_s, acc_s, stf_s, stb_s):
    seq = q_ref.shape[0]
    c = HG_CHUNK
    n_chunks = seq // c
    n_lvl = len(HG_LEVELS)

    qs_s[...] = jax.nn.silu(q_ref[...].astype(F32))
    v_s[...] = v_ref[...].astype(F32)
    for z_ref, lb_ref, lf_out, k_out in ((zf_ref, lbf_ref, lf_s, kf_s), (zb_ref, lbb_ref, lb_s, kb_s)):
        z = z_ref[...].astype(F32)
        lb = lb_ref[...]
        sig = jax.nn.sigmoid(z)
        lf_out[...] = jnp.log(lb + (1.0 - lb) * sig)
        k_out[...] = (1.0 - lb) * (1.0 - sig)

    ones_mat = jnp.ones((LANES, LANES), BF16)
    lvl = lvl_ref[...]
    row_c = lax.broadcasted_iota(I32, (c, LANES), 0)
    lane8 = lax.broadcasted_iota(I32, (HG_SUB, LANES), 1)
    sub8 = lax.broadcasted_iota(I32, (HG_SUB, LANES), 0)
    n_stack = (2 + n_lvl) * c

    def intra(n, ef, eb):
        r0 = pl.multiple_of(n * c, c)
        rows = pl.ds(r0, c)
        qs = qs_s[rows, :]
        kf = kf_s[rows, :]
        kb = kb_s[rows, :]
        for (e, kk, cs, qi_s, ki_s, d_s, tot_row) in ((ef, kf, cf_s, qif_s, kif_s, df_s, c - 1),
                                                    (eb, kb, cb_s, qib_s, kib_s, db_s, 0)):
            cum = e[0:c]
            cs[rows, :] = cum
            qi_s[rows, :] = (qs * jnp.exp(cum)).astype(BF16)
            ki_s[rows, :] = (kk * jnp.exp(e[c:2 * c])).astype(BF16)
            d_s[pl.ds(n, 1), :] = jnp.exp(cum[tot_row:tot_row + 1, :])

        a_mat = jnp.zeros((c, c), F32)
        for li, m in enumerate(HG_LEVELS):
            second = (row_c & (m // 2)) != 0
            wf = jnp.exp(ef[(2 + li) * c:(3 + li) * c])
            wb = jnp.exp(eb[(2 + li) * c:(3 + li) * c])
            qcat = jnp.concatenate([qs * jnp.where(second, wf, 0.0), qs * jnp.where(second, 0.0, wb)], axis=1)
            kcat = jnp.concatenate([kf * jnp.where(second, 0.0, wf), kb * jnp.where(second, wb, 0.0)], axis=1)
            sc = lax.dot_general(qcat.astype(BF16), kcat.astype(BF16),
                                 (((1,), (1,)), ((), ())), preferred_element_type=F32)
            a_mat = jnp.where(lvl == li, sc, a_mat)

        p_list = []
        for j in range(c // HG_SUB):
            rj = r0 + j * HG_SUB
            sl = pl.ds(rj, HG_SUB)
            cf = cf_s[sl, :]
            cb = cb_s[sl, :]
            qj = qs_s[sl, :]
            for s in range(HG_SUB):
                src = pl.ds(rj + s, 1)
                df = jnp.where(sub8 >= s, cf - cf_s[src, :], -jnp.inf)
                db = jnp.where(sub8 <= s, cb - cb_s[src, :], -jnp.inf)
                p_list.append(qj * (kf_s[src, :] * jnp.exp(df) + kb_s[src, :] * jnp.exp(db)))
        p_all = jnp.concatenate(p_list, axis=0).astype(BF16)
        red = jnp.dot(p_all, ones_mat, preferred_element_type=F32)
        diag_rows = []
        for j in range(c // HG_SUB):
            blk = jnp.zeros((HG_SUB, LANES), F32)
            for s in range(HG_SUB):
                idx = (j * HG_SUB + s) * HG_SUB
                blk = jnp.where(lane8 == j * HG_SUB + s, red[idx:idx + HG_SUB], blk)
            diag_rows.append(blk)
        a_mat = a_mat + jnp.concatenate(diag_rows, axis=0)
        acc_s[rows, :] = jnp.dot(a_mat.astype(BF16), v_s[rows, :].astype(BF16), preferred_element_type=F32)

    def intra_pair(n2, carry):
        r0 = pl.multiple_of(n2 * (2 * c), 2 * c)
        hls = []
        for lfs in (lf_s, lb_s):
            cols = []
            for j in range(2):
                lf = lfs[pl.ds(r0 + j * c, c), :]
                hi = lf.astype(BF16)
                cols.append(jnp.concatenate([hi, (lf - hi.astype(F32)).astype(BF16)], axis=0))
            hls.append(jnp.concatenate(cols, axis=1))
        ef2 = jnp.dot(mf_ref[...], hls[0], preferred_element_type=F32)
        eb2 = jnp.dot(mb_ref[...], hls[1], preferred_element_type=F32)
        for j in range(2):
            intra(2 * n2 + j, ef2[:, j * LANES:(j + 1) * LANES], eb2[:, j * LANES:(j + 1) * LANES])
        return carry

    lax.fori_loop(0, n_chunks // 2, intra_pair, 0, unroll=2)

    stf_s[...] = jnp.zeros_like(stf_s)
    stb_s[...] = jnp.zeros_like(stb_s)

    def inter(n, carry):
        for (st_s, qi_s, ki_s, d_s, idx) in ((stf_s, qif_s, kif_s, df_s, n),
                                             (stb_s, qib_s, kib_s, db_s, n_chunks - 1 - n)):
            rows = pl.ds(pl.multiple_of(idx * c, c), c)
            st = st_s[...]
            acc_s[rows, :] += lax.dot_general(qi_s[rows, :], st.astype(BF16),
                                              (((1,), (1,)), ((), ())), preferred_element_type=F32)
            upd = lax.dot_general(v_s[rows, :].astype(BF16), ki_s[rows, :],
                                  (((0,), (0,)), ((), ())), preferred_element_type=F32)
            st_s[...] = st * d_s[pl.ds(idx, 1), :] + upd
        return carry

    lax.fori_loop(0, n_chunks, inter, 0, unroll=4)

    o = acc_s[...]
    o = o * lax.rsqrt(jnp.mean(o * o, axis=-1, keepdims=True) + RMS_EPS) * nw_ref[...]
    o_ref[...] = (o * jax.nn.silu(g_ref[...].astype(F32))).astype(o_ref.dtype)


def hgrn2_mixer(proj, lb_f, lb_b, norm_w, batch, seq):
    n = proj.shape[0]
    heads = HG_HEADS
    c = HG_CHUNK
    n_chunks = seq // c
    mf, mb, lvl = _hgrn2_constants()
    n_stack = mf.shape[0]

    def col(off):
        return pl.BlockSpec((seq, LANES), lambda b, h, off=off: (b, off + h))

    def per_head(arr):
        return pl.BlockSpec((1, LANES), lambda b, h: (0, h))

    const2 = lambda b, h: (0, 0)
    seq_f32 = pltpu.VMEM((seq, LANES), F32)
    seq_bf16 = pltpu.VMEM((seq, LANES), BF16)
    dec = pltpu.VMEM((n_chunks, LANES), F32)
    state = pltpu.VMEM((LANES, LANES), F32)
    return pl.pallas_call(
        _hgrn2_kernel,
        grid=(batch, heads),
        in_specs=[col(0), col(heads), col(2 * heads), col(3 * heads), col(4 * heads),
                  per_head(lb_f), per_head(lb_b),
                  pl.BlockSpec((1, LANES), const2),
                  pl.BlockSpec((n_stack, 2 * c), const2),
                  pl.BlockSpec((n_stack, 2 * c), const2),
                  pl.BlockSpec((c, c), const2)],
        out_specs=pl.BlockSpec((seq, LANES), lambda b, h: (b, h)),
        out_shape=jax.ShapeDtypeStruct((n, HG_WIDTH), BF16),
        scratch_shapes=[seq_f32] * 8 + [seq_bf16] * 4 + [dec, dec, seq_f32, state, state],
        compiler_params=_cparams(("parallel", "parallel")),
        name="hgrn2_mixer",
    )(proj, proj, proj, proj, proj, lb_f.reshape(1, -1), lb_b.reshape(1, -1), norm_w.reshape(1, -1),
      jnp.asarray(mf, BF16), jnp.asarray(mb, BF16), jnp.asarray(lvl))


def _shift_rows(x, d, fill, reverse):
    pad = jnp.full((d, x.shape[1]), fill, x.dtype)
    if reverse:
        return jnp.concatenate([x[d:], pad], axis=0)
    return jnp.concatenate([pad, x[:-d]], axis=0)


def _linear_scan(a, u, reverse):
    seq = a.shape[0]
    d = 1
    while d < seq:
        u = u + a * _shift_rows(u, d, 0.0, reverse)
        if d * 2 < seq:
            a = a * _shift_rows(a, d, 1.0, reverse)
        d *= 2
    return u


def _rglru_kernel(xr_ref, gr_ref, cw_ref, cb_ref, wa_ref, ba_ref, wx_ref, bx_ref, lam_ref, o_ref, xp_s):
    seq = xr_ref.shape[0]
    pad = SUBLANES
    pad_l = RG_CONV // 2
    xp_s[0:pad, :] = jnp.zeros((pad, LANES), F32)
    xp_s[pad + seq:, :] = jnp.zeros((pad, LANES), F32)
    xp_s[pad:pad + seq, :] = xr_ref[...].astype(F32)
    xc = cb_ref[...]
    for j in range(RG_CONV):
        off = pad + j - pad_l
        xc = xc + xp_s[off:off + seq, :] * cw_ref[j:j + 1, :]
    xcb = xc.astype(BF16)
    h = None
    for d in range(2):
        r = jax.nn.sigmoid(jnp.dot(xcb, wa_ref[d, 0], preferred_element_type=F32) + ba_ref[d:d + 1, :])
        i = jax.nn.sigmoid(jnp.dot(xcb, wx_ref[d, 0], preferred_element_type=F32) + bx_ref[d:d + 1, :])
        log_a = (-RG_C) * r * jax.nn.softplus(-lam_ref[d:d + 1, :])
        a = jnp.exp(log_a)
        u = jnp.sqrt(-jnp.tanh(log_a) * (a * a + 1.0)) * (i * xc)
        hd = _linear_scan(a, u, reverse=(d == 1))
        h = hd if h is None else h + hd
    o_ref[...] = (h * jax.nn.gelu(gr_ref[...].astype(F32))).astype(o_ref.dtype)


def rglru_mixer(proj, conv_w, conv_b, wa, ba, wx, bx, lam, batch, seq):
    n = proj.shape[0]
    x_off = (3 * HG_KEY_WIDTH + 2 * HG_WIDTH) // LANES
    g_off = x_off + RG_WIDTH // LANES
    blk = lambda b, j: (0, j)
    return pl.pallas_call(
        _rglru_kernel,
        grid=(batch, RG_BLOCKS),
        in_specs=[
            pl.BlockSpec((seq, LANES), lambda b, j: (b, x_off + j)),
            pl.BlockSpec((seq, LANES), lambda b, j: (b, g_off + j)),
            pl.BlockSpec((RG_CONV, LANES), blk),
            pl.BlockSpec((1, LANES), blk),
            pl.BlockSpec((2, 1, RG_BLOCK_DIM, RG_BLOCK_DIM), lambda b, j: (0, j, 0, 0)),
            pl.BlockSpec((2, LANES), blk),
            pl.BlockSpec((2, 1, RG_BLOCK_DIM, RG_BLOCK_DIM), lambda b, j: (0, j, 0, 0)),
            pl.BlockSpec((2, LANES), blk),
            pl.BlockSpec((2, LANES), blk),
        ],
        out_specs=pl.BlockSpec((seq, LANES), lambda b, j: (b, j)),
        out_shape=jax.ShapeDtypeStruct((n, RG_WIDTH), BF16),
        scratch_shapes=[pltpu.VMEM((seq + 2 * SUBLANES, LANES), F32)],
        compiler_params=_cparams(("parallel", "parallel")),
        name="rglru_mixer",
    )(proj, proj, conv_w, conv_b.reshape(1, -1), wa.astype(BF16), ba, wx.astype(BF16), bx, lam)


def _outproj_kernel(yh_ref, yr_ref, x_ref, g0_ref, b0_ref, wo_ref, g1_ref, b1_ref, wr_ref,
                    h1_ref, h1p_ref, lg_ref):
    half = yh_ref.shape[1]
    h0 = _layer_norm(x_ref[...], g0_ref[...], b0_ref[...])
    mix = jnp.dot(yh_ref[...], wo_ref[0:half, :].astype(BF16), preferred_element_type=F32)
    mix = mix + jnp.dot(yr_ref[...], wo_ref[half:, :].astype(BF16), preferred_element_type=F32)
    h1 = _layer_norm(DN_ALPHA * h0 + mix, g1_ref[...], b1_ref[...])
    h1_ref[...] = h1
    _store_row_tiles(h1p_ref, _pack_bf16_pair(h1))
    lg_ref[...] = lax.dot_general(wr_ref[...], h1.astype(BF16), (((1,), (1,)), ((), ())),
                                  preferred_element_type=F32)


def outproj_ln_router(y_hg, y_rg, x2, g0, b0, w_out, g1, b1, w_router_t_bf16, tm=256):
    n, d = x2.shape
    half = y_hg.shape[1]
    n_e = w_router_t_bf16.shape[0]
    row = lambda i: (i, 0)
    const = lambda i: (0, 0)
    vec = pl.BlockSpec((1, d), const)
    return pl.pallas_call(
        _outproj_kernel,
        grid=(n // tm,),
        in_specs=[pl.BlockSpec((tm, half), row), pl.BlockSpec((tm, half), row), pl.BlockSpec((tm, d), row),
                  vec, vec, pl.BlockSpec((2 * half, d), const), vec, vec,
                  pl.BlockSpec((n_e, d), const)],
        out_specs=[pl.BlockSpec((tm, d), row), pl.BlockSpec((tm * SUBLANES, LANES), row),
                   pl.BlockSpec((n_e, tm), lambda i: (0, i))],
        out_shape=[jax.ShapeDtypeStruct((n, d), F32), jax.ShapeDtypeStruct((n * SUBLANES, LANES), U32),
                   jax.ShapeDtypeStruct((n_e, n), F32)],
        compiler_params=_cparams(("parallel",)),
        name="outproj_ln_router",
    )(y_hg, y_rg, x2, g0.reshape(1, d), b0.reshape(1, d), w_out, g1.reshape(1, d), b1.reshape(1, d),
      w_router_t_bf16)


def _first_argmax(vals, iota, n):
    m = jnp.max(vals, axis=0, keepdims=True)
    first = jnp.min(jnp.where(vals == m, iota, n), axis=0, keepdims=True)
    return m, first


def _route_kernel(lg_ref, bias_ref, idx_ref, gate_ref):
    n_e, tn = lg_ref.shape
    scores = jax.nn.sigmoid(lg_ref[...])
    choice = scores + bias_ref[...]
    neg = -jnp.inf
    gi = lax.broadcasted_iota(I32, (GROUP_SIZE, tn), 0)
    gs_rows = []
    for g in range(N_GROUPS):
        blk = choice[g * GROUP_SIZE:(g + 1) * GROUP_SIZE]
        m1, f1 = _first_argmax(blk, gi, GROUP_SIZE)
        m2 = jnp.max(jnp.where(gi == f1, neg, blk), axis=0, keepdims=True)
        gs_rows.append(m1 + m2)
    gs = jnp.concatenate(gs_rows, axis=0)
    ri = lax.broadcasted_iota(I32, (N_GROUPS, tn), 0)
    gmask = jnp.zeros((N_GROUPS, tn), jnp.bool_)
    for _ in range(TOPK_GROUPS):
        _, f = _first_argmax(gs, ri, N_GROUPS)
        hit = ri == f
        gmask = gmask | hit
        gs = jnp.where(hit, neg, gs)
    gm = jnp.where(gmask, 1.0, 0.0)
    emask = jnp.concatenate(
        [jnp.broadcast_to(gm[g:g + 1], (GROUP_SIZE, tn)) for g in range(N_GROUPS)], axis=0)
    masked = jnp.where(emask > 0.5, choice, neg)
    ei = lax.broadcasted_iota(I32, (n_e, tn), 0)
    idx_rows, gate_rows = [], []
    for _ in range(TOP_K):
        _, f = _first_argmax(masked, ei, n_e)
        hit = ei == f
        idx_rows.append(f)
        gate_rows.append(jnp.sum(jnp.where(hit, scores, 0.0), axis=0, keepdims=True))
        masked = jnp.where(hit, neg, masked)
    gate = jnp.concatenate(gate_rows, axis=0)
    gate = gate / jnp.sum(gate, axis=0, keepdims=True) * ROUTE_SCALE
    idx_ref[...] = jnp.concatenate(idx_rows, axis=0)
    gate_ref[...] = gate


def route_topk(logits_t, bias, tn=512):
    n_e, n = logits_t.shape
    return pl.pallas_call(
        _route_kernel,
        grid=(n // tn,),
        in_specs=[pl.BlockSpec((n_e, tn), lambda i: (0, i)), pl.BlockSpec((n_e, 1), lambda i: (0, 0))],
        out_specs=[pl.BlockSpec((TOP_K, tn), lambda i: (0, i)), pl.BlockSpec((TOP_K, tn), lambda i: (0, i))],
        out_shape=[jax.ShapeDtypeStruct((TOP_K, n), I32), jax.ShapeDtypeStruct((TOP_K, n), F32)],
        compiler_params=_cparams(("parallel",)),
        name="route_topk",
    )(logits_t, bias.reshape(n_e, 1))


def _rank_kernel(idx_ref, su_ref, rank_ref, cnt_ref, carry_s):
    n_e = cnt_ref.shape[0]
    tn = idx_ref.shape[1]

    @pl.when(pl.program_id(0) == 0)
    def _():
        carry_s[...] = jnp.zeros_like(carry_s)

    idx = idx_ref[...]
    ei = lax.broadcasted_iota(I32, (n_e, tn), 0)
    hits = [ei == idx[k:k + 1] for k in range(TOP_K)]
    member = jnp.zeros((n_e, tn), F32)
    for h in hits:
        member = member + jnp.where(h, 1.0, 0.0)
    prefix = jnp.dot(member.astype(BF16), su_ref[...], preferred_element_type=F32) + carry_s[:, 0:1]
    rank_ref[...] = jnp.concatenate(
        [jnp.sum(jnp.where(h, prefix, 0.0), axis=0, keepdims=True) for h in hits], axis=0).astype(I32)
    carry_s[...] = carry_s[...] + jnp.sum(member, axis=1, keepdims=True)
    cnt_ref[...] = carry_s[...].astype(I32)


def expert_ranks(idx_t, tn=512):
    k, n = idx_t.shape
    su = jnp.asarray(np.triu(np.ones((tn, tn), np.float32), 1), BF16)
    return pl.pallas_call(
        _rank_kernel,
        grid=(n // tn,),
        in_specs=[pl.BlockSpec((k, tn), lambda i: (0, i)), pl.BlockSpec((tn, tn), lambda i: (0, 0))],
        out_specs=[pl.BlockSpec((k, tn), lambda i: (0, i)), pl.BlockSpec((N_EXPERTS, LANES), lambda i: (0, 0))],
        out_shape=[jax.ShapeDtypeStruct((k, n), I32), jax.ShapeDtypeStruct((N_EXPERTS, LANES), I32)],
        scratch_shapes=[pltpu.VMEM((N_EXPERTS, LANES), F32)],
        compiler_params=_cparams(("arbitrary",)),
        name="expert_ranks",
    )(idx_t, su)


def _dest_kernel(idx_ref, rank_ref, start_ref, dest_ref):
    n_e = start_ref.shape[0]
    tn = idx_ref.shape[1]
    idx = idx_ref[...]
    ei = lax.broadcasted_iota(I32, (n_e, tn), 0)
    start = start_ref[...]
    base = jnp.concatenate(
        [jnp.sum(jnp.where(ei == idx[k:k + 1], start, 0.0), axis=0, keepdims=True) for k in range(TOP_K)],
        axis=0)
    dest_ref[...] = base.astype(I32) + rank_ref[...]


def assignment_rows(idx_t, rank_t, padded_start, tn=512):
    k, n = idx_t.shape
    blk = pl.BlockSpec((k, tn), lambda i: (0, i))
    return pl.pallas_call(
        _dest_kernel,
        grid=(n // tn,),
        in_specs=[blk, blk, pl.BlockSpec((N_EXPERTS, 1), lambda i: (0, 0))],
        out_specs=blk,
        out_shape=jax.ShapeDtypeStruct((k, n), I32),
        compiler_params=_cparams(("parallel",)),
        name="assignment_rows",
    )(idx_t, rank_t, padded_start.astype(F32).reshape(N_EXPERTS, 1))


WEIGHT_DMA_PRIORITY = 1
WEIGHT_SLOTS = 3
ROW_GROUP = 8
ROW_CHUNK = 8
BLOCKS_PER_STEP = 2


def _rowmap_kernel(dest_ref, tok_ref):
    n_tok = dest_ref.shape[0] // TOP_K

    def body(j, c):
        for k in range(TOP_K):
            tok_ref[dest_ref[k * n_tok + j]] = j
        return c

    lax.fori_loop(0, n_tok, body, 0, unroll=4)


def row_tokens(dest_flat, n_rows):
    return pl.pallas_call(
        _rowmap_kernel,
        grid_spec=pltpu.PrefetchScalarGridSpec(
            num_scalar_prefetch=1,
            grid=(1,),
            in_specs=[],
            out_specs=pl.BlockSpec(memory_space=pltpu.SMEM),
        ),
        out_shape=jax.ShapeDtypeStruct((n_rows,), I32),
        compiler_params=_cparams(("arbitrary",)),
        name="row_tokens",
    )(dest_flat)


def _experts_kernel(be_ref, nxt_ref, nxt2_ref, slot_ref, cnt_ref, tok_ref, nb_ref,
                    h1p_ref, wg_ref, wu_ref, wd_ref, ys_ref,
                    xbuf, wgb, wub, wdb, gsem, wsem):
    tm = ys_ref.shape[0] // (SUBLANES * BLOCKS_PER_STEP)
    step = pl.program_id(0)
    nb = nb_ref[0]

    def weight_copies(e, slot):
        return (pltpu.make_async_copy(wg_ref.at[e], wgb.at[slot], wsem.at[slot]),
                pltpu.make_async_copy(wu_ref.at[e], wub.at[slot], wsem.at[slot]),
                pltpu.make_async_copy(wd_ref.at[e], wdb.at[slot], wsem.at[slot]))

    def start_weights(e, slot):
        for cp in weight_copies(e, slot):
            cp.start(priority=WEIGHT_DMA_PRIORITY)

    def start_group(grp, slot):
        def one_block(g, carry):
            blk = jnp.minimum(grp * ROW_GROUP + g, nb - 1)
            cnt = cnt_ref[blk]
            last = cnt - 1
            base = blk * tm
            n_chunks = lax.div(cnt + (ROW_CHUNK - 1), ROW_CHUNK)

            def issue(ch, c):
                for j in range(ROW_CHUNK):
                    r = ch * ROW_CHUNK + j
                    t = tok_ref[base + jnp.minimum(r, last)]
                    pltpu.make_async_copy(h1p_ref.at[pl.ds(t * SUBLANES, SUBLANES)],
                                          xbuf.at[slot, pl.ds((g * tm + r) * SUBLANES, SUBLANES)],
                                          gsem.at[slot]).start()
                return c

            lax.fori_loop(0, n_chunks, issue, 0)
            done = n_chunks * ROW_CHUNK
            pad = tm - done
            size = tm // 2
            while size >= ROW_CHUNK:
                @pl.when((pad & size) != 0)
                def _(done=done, size=size):
                    pltpu.make_async_copy(
                        h1p_ref.at[pl.ds(0, size * SUBLANES)],
                        xbuf.at[slot, pl.ds((g * tm + done) * SUBLANES, size * SUBLANES)],
                        gsem.at[slot]).start()
                done = done + jnp.where((pad & size) != 0, size, 0)
                size //= 2
            return carry

        lax.fori_loop(0, ROW_GROUP, one_block, 0)

    def process(i, out_ref):
        e = be_ref[i]
        wslot = slot_ref[i]
        grp = lax.div(i, ROW_GROUP)
        sub = lax.rem(i, ROW_GROUP)
        gslot = lax.rem(grp, 2)

        @pl.when(i == 0)
        def _():
            start_weights(e, wslot)
            nxt = nxt_ref[0]

            @pl.when(nxt >= 0)
            def _():
                start_weights(nxt, lax.rem(wslot + 1, WEIGHT_SLOTS))

            start_group(0, 0)

        @pl.when((i == 0) | (e != be_ref[jnp.maximum(i - 1, 0)]))
        def _():
            for cp in weight_copies(e, wslot):
                cp.wait()
            nxt2 = nxt2_ref[i]

            @pl.when(nxt2 >= 0)
            def _():
                start_weights(nxt2, lax.rem(wslot + 2, WEIGHT_SLOTS))

        @pl.when(sub == 0)
        def _():
            pltpu.make_async_copy(h1p_ref.at[pl.ds(0, ROW_GROUP * tm * SUBLANES)], xbuf.at[gslot],
                                  gsem.at[gslot]).wait()

            @pl.when((grp + 1) * ROW_GROUP < nb)
            def _():
                start_group(grp + 1, 1 - gslot)

        row0 = pl.multiple_of(sub * (tm * SUBLANES), tm * SUBLANES)
        x = _unpack_bf16_pair(_load_row_tiles(xbuf.at[gslot, pl.ds(row0, tm * SUBLANES)], tm))
        a = jnp.dot(x, wgb[wslot].astype(BF16), preferred_element_type=F32)
        u = jnp.dot(x, wub[wslot].astype(BF16), preferred_element_type=F32)
        hid = (jax.nn.silu(a) * u).astype(BF16)
        y = jnp.dot(hid, wdb[wslot].astype(BF16), preferred_element_type=F32)
        _store_row_tiles(out_ref, _pack_bf16_pair(y))

    for h in range(BLOCKS_PER_STEP):
        blk = step * BLOCKS_PER_STEP + h

        @pl.when(blk < nb)
        def _(blk=blk, h=h):
            process(blk, ys_ref.at[pl.ds(h * tm * SUBLANES, tm * SUBLANES)])


def grouped_experts(h1p_tiles, row_tok, block_expert, block_next, block_next2, block_slot, block_count,
                    n_blocks_used, w_gate, w_up, w_down, tm=MOE_BLOCK):
    n_rows = row_tok.shape[0]
    n_e, d, ff = w_gate.shape
    n_blocks = n_rows // tm
    assert n_blocks % BLOCKS_PER_STEP == 0
    bps = BLOCKS_PER_STEP
    rowmap = lambda i, be, nx, nx2, sl, cn, tok, nb: (jnp.minimum(i, jnp.maximum(nb[0] - 1, 0) // bps), 0)
    hbm = pl.BlockSpec(memory_space=pl.ANY)
    return pl.pallas_call(
        _experts_kernel,
        grid_spec=pltpu.PrefetchScalarGridSpec(
            num_scalar_prefetch=7,
            grid=(n_blocks // bps,),
            in_specs=[hbm, hbm, hbm, hbm],
            out_specs=pl.BlockSpec((bps * tm * SUBLANES, LANES), rowmap),
            scratch_shapes=[pltpu.VMEM((2, ROW_GROUP * tm * SUBLANES, LANES), U32),
                            pltpu.VMEM((WEIGHT_SLOTS, d, ff), F32), pltpu.VMEM((WEIGHT_SLOTS, d, ff), F32),
                            pltpu.VMEM((WEIGHT_SLOTS, ff, d), F32),
                            pltpu.SemaphoreType.DMA((2,)), pltpu.SemaphoreType.DMA((WEIGHT_SLOTS,))],
        ),
        out_shape=jax.ShapeDtypeStruct((n_rows * SUBLANES, LANES), U32),
        compiler_params=_cparams(("arbitrary",)),
        name="grouped_experts",
    )(block_expert, block_next, block_next2, block_slot, block_count, row_tok, n_blocks_used,
      h1p_tiles, w_gate, w_up, w_down)


def _combine_kernel(dest_ref, h1_ref, h1p_ref, gate_ref, wsg_ref, wsu_ref, wsd_ref, g2_ref, b2_ref, ys_ref,
                    o_ref, buf_s, sem):
    tm = h1_ref.shape[0]
    i = pl.program_id(0)
    n_steps = pl.num_programs(0)
    n_tok = n_steps * tm
    slot = lax.rem(i, 2)

    def start_rows(step, sl):
        def issue(r, c):
            for k in range(TOP_K):
                row = dest_ref[k * n_tok + step * tm + r]
                pltpu.make_async_copy(ys_ref.at[pl.ds(row * SUBLANES, SUBLANES)],
                                      buf_s.at[sl, k, pl.ds(r * SUBLANES, SUBLANES)], sem.at[sl]).start()
            return c

        lax.fori_loop(0, tm, issue, 0, unroll=2)

    @pl.when(i == 0)
    def _():
        start_rows(0, 0)

    @pl.when(i + 1 < n_steps)
    def _():
        start_rows(i + 1, 1 - slot)

    xb = _unpack_bf16_pair(_load_row_tiles(h1p_ref, tm))
    a = jnp.dot(xb, wsg_ref[...].astype(BF16), preferred_element_type=F32)
    u = jnp.dot(xb, wsu_ref[...].astype(BF16), preferred_element_type=F32)
    shared = jnp.dot((jax.nn.silu(a) * u).astype(BF16), wsd_ref[...].astype(BF16),
                     preferred_element_type=F32)

    for k in range(TOP_K):
        pltpu.make_async_copy(ys_ref.at[pl.ds(0, tm * SUBLANES)], buf_s.at[slot, k], sem.at[slot]).wait()

    gate = gate_ref[...]
    routed = jnp.zeros_like(shared)
    for k in range(TOP_K):
        routed = routed + _unpack_pair_f32(_load_row_tiles(buf_s.at[slot, k], tm)) * gate[:, k:k + 1]
    o_ref[...] = _layer_norm(DN_ALPHA * h1_ref[...] + routed + shared, g2_ref[...], b2_ref[...])


def combine_shared_ln(dest_flat, h1, h1p_tiles, gate_nk, ys_tiles, wsg, wsu, wsd, g2, b2, tm=128):
    n, d = h1.shape
    ff = wsg.shape[1]
    row = lambda i, dest: (i, 0)
    const = lambda i, dest: (0, 0)
    return pl.pallas_call(
        _combine_kernel,
        grid_spec=pltpu.PrefetchScalarGridSpec(
            num_scalar_prefetch=1,
            grid=(n // tm,),
            in_specs=[pl.BlockSpec((tm, d), row), pl.BlockSpec((tm * SUBLANES, LANES), row),
                      pl.BlockSpec((tm, TOP_K), row),
                      pl.BlockSpec((d, ff), const), pl.BlockSpec((d, ff), const), pl.BlockSpec((ff, d), const),
                      pl.BlockSpec((1, d), const), pl.BlockSpec((1, d), const),
                      pl.BlockSpec(memory_space=pl.ANY)],
            out_specs=pl.BlockSpec((tm, d), row),
            scratch_shapes=[pltpu.VMEM((2, TOP_K, tm * SUBLANES, LANES), U32), pltpu.SemaphoreType.DMA((2,))],
        ),
        out_shape=jax.ShapeDtypeStruct((n, d), F32),
        compiler_params=_cparams(("arbitrary",)),
        name="combine_shared_ln",
    )(dest_flat, h1, h1p_tiles, gate_nk, wsg, wsu, wsd, g2.reshape(1, d), b2.reshape(1, d), ys_tiles)


def kernel(x, ln_emb_g, ln_emb_b, w_in, lb_logits, hg_norm_w, conv_w, conv_b, rg_wa, rg_ba, rg_wx, rg_bx,
           rg_lam, w_out, ln1_g, ln1_b, w_router, router_bias, w_gate, w_up, w_down, ws_gate, ws_up, ws_down,
           ln2_g, ln2_b):
    batch, seq, d = x.shape
    assert d == 2 * SUBLANES * LANES, "packed rows must fill one 8x128 tile"
    n_tok = batch * seq
    layer = 0
    x2 = x.reshape(n_tok, d)
    lb = jnp.cumsum(jax.nn.softmax(lb_logits.astype(F32), axis=1), axis=1)

    proj = ln_inproj(x2, ln_emb_g, ln_emb_b, w_in[layer])
    y_hg = hgrn2_mixer(proj, lb[0, layer], lb[1, layer], hg_norm_w[layer], batch, seq)
    y_rg = rglru_mixer(proj, conv_w[layer], conv_b[layer], rg_wa[layer], rg_ba[layer], rg_wx[layer],
                       rg_bx[layer], rg_lam[layer], batch, seq)
    h1, h1p_tiles, logits_t = outproj_ln_router(
        y_hg, y_rg, x2, ln_emb_g, ln_emb_b, w_out[layer], ln1_g[layer], ln1_b[layer],
        w_router[layer].T.astype(BF16))

    idx_t, gate_t = route_topk(logits_t, router_bias[layer])
    rank_t, counts = expert_ranks(idx_t)
    counts = counts[:, 0]

    n_assign = n_tok * TOP_K
    n_blocks = -(-(n_assign + N_EXPERTS * (MOE_BLOCK - 1)) // MOE_BLOCK)
    n_rows = n_blocks * MOE_BLOCK
    padded = (counts + MOE_BLOCK - 1) // MOE_BLOCK * MOE_BLOCK
    padded_end = jnp.cumsum(padded)
    padded_start = padded_end - padded
    n_used_blocks = (padded_end[-1:] // MOE_BLOCK).astype(I32)
    blk_row = jnp.minimum(jnp.arange(n_blocks, dtype=I32), n_used_blocks[0] - 1) * MOE_BLOCK
    block_expert = jnp.minimum(
        jnp.sum((padded_end[None, :] <= blk_row[:, None]).astype(I32), axis=1), N_EXPERTS - 1)
    e_ids = jnp.arange(N_EXPERTS, dtype=I32)
    active = counts > 0
    later = lax.cummin(jnp.where(active, e_ids, N_EXPERTS)[::-1])[::-1]
    nxt = jnp.concatenate([later[1:], jnp.full((1,), N_EXPERTS, I32)])
    next1 = jnp.where(nxt < N_EXPERTS, nxt, -1).astype(I32)
    hot1 = (next1[:, None] == e_ids[None, :]).astype(I32)
    next2 = jnp.sum(hot1 * (next1 + 1)[None, :], axis=1) - 1
    ordinal = jnp.cumsum(active.astype(I32)) - 1
    onehot = (block_expert[:, None] == e_ids[None, :]).astype(I32)
    block_next = jnp.sum(onehot * next1[None, :], axis=1)
    block_next2 = jnp.sum(onehot * next2[None, :], axis=1)
    block_slot = jnp.sum(onehot * (ordinal % WEIGHT_SLOTS)[None, :], axis=1)
    block_count = jnp.clip(jnp.sum(onehot * (counts + padded_start)[None, :], axis=1) - blk_row, 1, MOE_BLOCK)

    dest_flat = assignment_rows(idx_t, rank_t, padded_start).reshape(-1)
    row_tok = row_tokens(dest_flat, n_rows)

    ys_tiles = grouped_experts(h1p_tiles, row_tok, block_expert, block_next, block_next2, block_slot,
                               block_count, n_used_blocks, w_gate[layer], w_up[layer], w_down[layer])
    out = combine_shared_ln(dest_flat, h1, h1p_tiles, gate_t.T, ys_tiles,
                            ws_gate[layer], ws_up[layer], ws_down[layer], ln2_g[layer], ln2_b[layer])
    return out.reshape(batch, seq, d)
```

```python
import functools

import numpy as np
import jax
import jax.numpy as jnp
from jax import lax
from jax.experimental import pallas as pl
from jax.experimental.pallas import tpu as pltpu

F32 = jnp.float32
BF16 = jnp.bfloat16
U32 = jnp.uint32
I32 = jnp.int32

D_MODEL = 2048
DEPTH = 1
HG_KEY_DIM = 128
HG_VAL_DIM = 128
HG_WIDTH = D_MODEL // 2
HG_HEADS = HG_WIDTH // HG_VAL_DIM
HG_KEY_WIDTH = HG_HEADS * HG_KEY_DIM
RG_WIDTH = D_MODEL - HG_WIDTH
RG_BLOCKS = 8
RG_BLOCK_DIM = RG_WIDTH // RG_BLOCKS
RG_CONV = 4
RG_C = 8.0
MIX_WIDTH = HG_WIDTH + RG_WIDTH
IN_WIDTHS = (HG_KEY_WIDTH, HG_KEY_WIDTH, HG_KEY_WIDTH, HG_WIDTH, HG_WIDTH, RG_WIDTH, RG_WIDTH)
IN_WIDTH = sum(IN_WIDTHS)
N_EXPERTS = 256
TOP_K = 8
N_GROUPS = 8
TOPK_GROUPS = 4
GROUP_SIZE = N_EXPERTS // N_GROUPS
EXPERT_FF = 512
SHARED_FF = 512
ROUTE_SCALE = 2.5
MOE_BLOCK = 128
DN_ALPHA = (2 * DEPTH) ** 0.25
LN_EPS = 1e-5
RMS_EPS = 1e-6

LANES = 128
SUBLANES = 8
VMEM_LIMIT_BYTES = 56 * 1024 * 1024

HG_CHUNK = 128
HG_SUB = 8
HG_LEVELS = (16, 32, 64, 128)


def _cparams(sem, vmem=VMEM_LIMIT_BYTES):
    return pltpu.CompilerParams(dimension_semantics=sem, vmem_limit_bytes=vmem)


def _layer_norm(x, g, b):
    mu = jnp.mean(x, axis=-1, keepdims=True)
    xc = x - mu
    var = jnp.mean(xc * xc, axis=-1, keepdims=True)
    return xc * lax.rsqrt(var + LN_EPS) * g + b


def _pack_bf16_pair(x):
    n = x.shape[1] // 2
    bits = lax.bitcast_convert_type(x.astype(BF16).astype(F32), U32)
    return (bits[:, :n] >> 16) | (bits[:, n:] & jnp.uint32(0xFFFF0000))


def _unpack_pair_f32(w):
    lo = lax.bitcast_convert_type(w << 16, F32)
    hi = lax.bitcast_convert_type(w & jnp.uint32(0xFFFF0000), F32)
    return jnp.concatenate([lo, hi], axis=1)


def _unpack_bf16_pair(w):
    return _unpack_pair_f32(w).astype(BF16)


def _store_row_tiles(ref, packed):
    m = packed.shape[0]
    for s in range(SUBLANES):
        ref[pl.ds(s, m, stride=SUBLANES), :] = packed[:, s * LANES:(s + 1) * LANES]


def _load_row_tiles(ref, m):
    return jnp.concatenate([ref[pl.ds(s, m, stride=SUBLANES), :] for s in range(SUBLANES)], axis=1)


def _ln_inproj_kernel(x_ref, g_ref, b_ref, w_ref, o_ref, hb_ref):
    @pl.when(pl.program_id(1) == 0)
    def _():
        hb_ref[...] = _layer_norm(x_ref[...], g_ref[...], b_ref[...]).astype(BF16)

    o_ref[...] = jnp.dot(hb_ref[...], w_ref[...].astype(BF16), preferred_element_type=F32).astype(o_ref.dtype)


def ln_inproj(x2, g, b, w_in, tm=1024, tn=1024):
    n, d = x2.shape
    width = w_in.shape[1]
    return pl.pallas_call(
        _ln_inproj_kernel,
        grid=(n // tm, width // tn),
        in_specs=[
            pl.BlockSpec((tm, d), lambda i, j: (i, 0)),
            pl.BlockSpec((1, d), lambda i, j: (0, 0)),
            pl.BlockSpec((1, d), lambda i, j: (0, 0)),
            pl.BlockSpec((d, tn), lambda i, j: (0, j)),
        ],
        out_specs=pl.BlockSpec((tm, tn), lambda i, j: (i, j)),
        out_shape=jax.ShapeDtypeStruct((n, width), BF16),
        scratch_shapes=[pltpu.VMEM((tm, d), BF16)],
        compiler_params=_cparams(("parallel", "arbitrary")),
        name="ln_inproj",
    )(x2, g.reshape(1, d), b.reshape(1, d), w_in)


def _hgrn2_constants():
    c = HG_CHUNK
    t = np.arange(c)[:, None]
    u = np.arange(c)[None, :]
    fwd = [(u <= t), (u > t)]
    bwd = [(u >= t), (u < t)]
    for m in HG_LEVELS:
        start = (t // m) * m
        mid = start + m // 2 - 1
        second = t > mid
        fwd.append(np.where(second, (u > mid) & (u <= t), (u > t) & (u <= mid)))
        bwd.append(np.where(second, (u > mid) & (u < t), (u >= t) & (u <= mid)))
    fwd = np.concatenate(fwd, axis=0).astype(np.float32)
    bwd = np.concatenate(bwd, axis=0).astype(np.float32)
    fwd = np.concatenate([fwd, fwd], axis=1)
    bwd = np.concatenate([bwd, bwd], axis=1)
    lvl = np.full((c, c), -1, np.int32)
    tt = np.arange(c)[:, None]
    ss = np.arange(c)[None, :]
    for li in reversed(range(len(HG_LEVELS))):
        m = HG_LEVELS[li]
        lvl = np.where((tt // m) == (ss // m), li, lvl)
    lvl = np.where((tt // HG_SUB) == (ss // HG_SUB), -1, lvl).astype(np.int32)
    return fwd, bwd, lvl


def _hgrn2_kernel(q_ref, zf_ref, zb_ref, v_ref, g_ref, lbf_ref, lbb_ref, nw_ref,
                  mf_ref, mb_ref, lvl_ref, o_ref,
                  qs_s, v_s, lf_s, kf_s, lb_s, kb_s, cf_s, cb_s,
                  qif_s, kif_s, qib_s, kib_s, df_s, db_s, acc_s, stf_s, stb_s):
    seq = q_ref.shape[0]
    c = HG_CHUNK
    n_chunks = seq // c
    n_lvl = len(HG_LEVELS)

    qs_s[...] = jax.nn.silu(q_ref[...].astype(F32))
    v_s[...] = v_ref[...].astype(F32)
    for z_ref, lb_ref, lf_out, k_out in ((zf_ref, lbf_ref, lf_s, kf_s), (zb_ref, lbb_ref, lb_s, kb_s)):
        z = z_ref[...].astype(F32)
        lb = lb_ref[...]
        sig = jax.nn.sigmoid(z)
        lf_out[...] = jnp.log(lb + (1.0 - lb) * sig)
        k_out[...] = (1.0 - lb) * (1.0 - sig)

    ones_mat = jnp.ones((LANES, LANES), BF16)
    lvl = lvl_ref[...]
    row_c = lax.broadcasted_iota(I32, (c, LANES), 0)
    lane8 = lax.broadcasted_iota(I32, (HG_SUB, LANES), 1)
    sub8 = lax.broadcasted_iota(I32, (HG_SUB, LANES), 0)
    n_stack = (2 + n_lvl) * c

    def intra(n, ef, eb):
        r0 = pl.multiple_of(n * c, c)
        rows = pl.ds(r0, c)
        qs = qs_s[rows, :]
        kf = kf_s[rows, :]
        kb = kb_s[rows, :]
        for (e, kk, cs, qi_s, ki_s, d_s, tot_row) in ((ef, kf, cf_s, qif_s, kif_s, df_s, c - 1),
                                                    (eb, kb, cb_s, qib_s, kib_s, db_s, 0)):
            cum = e[0:c]
            cs[rows, :] = cum
            qi_s[rows, :] = (qs * jnp.exp(cum)).astype(BF16)
            ki_s[rows, :] = (kk * jnp.exp(e[c:2 * c])).astype(BF16)
            d_s[pl.ds(n, 1), :] = jnp.exp(cum[tot_row:tot_row + 1, :])

        a_mat = jnp.zeros((c, c), F32)
        for li, m in enumerate(HG_LEVELS):
            second = (row_c & (m // 2)) != 0
            wf = jnp.exp(ef[(2 + li) * c:(3 + li) * c])
            wb = jnp.exp(eb[(2 + li) * c:(3 + li) * c])
            qcat = jnp.concatenate([qs * jnp.where(second, wf, 0.0), qs * jnp.where(second, 0.0, wb)], axis=1)
            kcat = jnp.concatenate([kf * jnp.where(second, 0.0, wf), kb * jnp.where(second, wb, 0.0)], axis=1)
            sc = lax.dot_general(qcat.astype(BF16), kcat.astype(BF16),
                                 (((1,), (1,)), ((), ())), preferred_element_type=F32)
            a_mat = jnp.where(lvl == li, sc, a_mat)

        p_list = []
        for j in range(c // HG_SUB):
            rj = r0 + j * HG_SUB
            sl = pl.ds(rj, HG_SUB)
            cf = cf_s[sl, :]
            cb = cb_s[sl, :]
            qj = qs_s[sl, :]
            for s in range(HG_SUB):
                src = pl.ds(rj + s, 1)
                df = jnp.where(sub8 >= s, cf - cf_s[src, :], -jnp.inf)
                db = jnp.where(sub8 <= s, cb - cb_s[src, :], -jnp.inf)
                p_list.append(qj * (kf_s[src, :] * jnp.exp(df) + kb_s[src, :] * jnp.exp(db)))
        p_all = jnp.concatenate(p_list, axis=0).astype(BF16)
        red = jnp.dot(p_all, ones_mat, preferred_element_type=F32)
        diag_rows = []
        for j in range(c // HG_SUB):
            blk = jnp.zeros((HG_SUB, LANES), F32)
            for s in range(HG_SUB):
                idx = (j * HG_SUB + s) * HG_SUB
                blk = jnp.where(lane8 == j * HG_SUB + s, red[idx:idx + HG_SUB], blk)
            diag_rows.append(blk)
        a_mat = a_mat + jnp.concatenate(diag_rows, axis=0)
        acc_s[rows, :] = jnp.dot(a_mat.astype(BF16), v_s[rows, :].astype(BF16), preferred_element_type=F32)

    def intra_pair(n2, carry):
        r0 = pl.multiple_of(n2 * (2 * c), 2 * c)
        hls = []
        for lfs in (lf_s, lb_s):
            cols = []
            for j in range(2):
                lf = lfs[pl.ds(r0 + j * c, c), :]
                hi = lf.astype(BF16)
                cols.append(jnp.concatenate([hi, (lf - hi.astype(F32)).astype(BF16)], axis=0))
            hls.append(jnp.concatenate(cols, axis=1))
        ef2 = jnp.dot(mf_ref[...], hls[0], preferred_element_type=F32)
        eb2 = jnp.dot(mb_ref[...], hls[1], preferred_element_type=F32)
        for j in range(2):
            intra(2 * n2 + j, ef2[:, j * LANES:(j + 1) * LANES], eb2[:, j * LANES:(j + 1) * LANES])
        return carry

    lax.fori_loop(0, n_chunks // 2, intra_pair, 0, unroll=4)

    stf_s[...] = jnp.zeros_like(stf_s)
    stb_s[...] = jnp.zeros_like(stb_s)

    def inter(n, carry):
        for (st_s, qi_s, ki_s, d_s, idx) in ((stf_s, qif_s, kif_s, df_s, n),
                                             (stb_s, qib_s, kib_s, db_s, n_chunks - 1 - n)):
            rows = pl.ds(pl.multiple_of(idx * c, c), c)
            st = st_s[...]
            acc_s[rows, :] += lax.dot_general(qi_s[rows, :], st.astype(BF16),
                                              (((1,), (1,)), ((), ())), preferred_element_type=F32)
            upd = lax.dot_general(v_s[rows, :].astype(BF16), ki_s[rows, :],
                                  (((0,), (0,)), ((), ())), preferred_element_type=F32)
            st_s[...] = st * d_s[pl.ds(idx, 1), :] + upd
        return carry

    lax.fori_loop(0, n_chunks, inter, 0, unroll=4)

    o = acc_s[...]
    o = o * lax.rsqrt(jnp.mean(o * o, axis=-1, keepdims=True) + RMS_EPS) * nw_ref[...]
    o_ref[...] = (o * jax.nn.silu(g_ref[...].astype(F32))).astype(o_ref.dtype)


def hgrn2_mixer(proj, lb_f, lb_b, norm_w, batch, seq):
    n = proj.shape[0]
    heads = HG_HEADS
    c = HG_CHUNK
    n_chunks = seq // c
    mf, mb, lvl = _hgrn2_constants()
    n_stack = mf.shape[0]

    def col(off):
        return pl.BlockSpec((seq, LANES), lambda b, h, off=off: (b, off + h))

    def per_head(arr):
        return pl.BlockSpec((1, LANES), lambda b, h: (0, h))

    const2 = lambda b, h: (0, 0)
    seq_f32 = pltpu.VMEM((seq, LANES), F32)
    seq_bf16 = pltpu.VMEM((seq, LANES), BF16)
    dec = pltpu.VMEM((n_chunks, LANES), F32)
    state = pltpu.VMEM((LANES, LANES), F32)
    return pl.pallas_call(
        _hgrn2_kernel,
        grid=(batch, heads),
        in_specs=[col(0), col(heads), col(2 * heads), col(3 * heads), col(4 * heads),
                  per_head(lb_f), per_head(lb_b),
                  pl.BlockSpec((1, LANES), const2),
                  pl.BlockSpec((n_stack, 2 * c), const2),
                  pl.BlockSpec((n_stack, 2 * c), const2),
                  pl.BlockSpec((c, c), const2)],
        out_specs=pl.BlockSpec((seq, LANES), lambda b, h: (b, h)),
        out_shape=jax.ShapeDtypeStruct((n, HG_WIDTH), BF16),
        scratch_shapes=[seq_f32] * 8 + [seq_bf16] * 4 + [dec, dec, seq_f32, state, state],
        compiler_params=_cparams(("parallel", "parallel")),
        name="hgrn2_mixer",
    )(proj, proj, proj, proj, proj, lb_f.reshape(1, -1), lb_b.reshape(1, -1), norm_w.reshape(1, -1),
      jnp.asarray(mf, BF16), jnp.asarray(mb, BF16), jnp.asarray(lvl))


def _shift_rows(x, d, fill, reverse):
    pad = jnp.full((d, x.shape[1]), fill, x.dtype)
    if reverse:
        return jnp.concatenate([x[d:], pad], axis=0)
    return jnp.concatenate([pad, x[:-d]], axis=0)


def _linear_scan(a, u, reverse):
    seq = a.shape[0]
    d = 1
    while d < seq:
        u = u + a * _shift_rows(u, d, 0.0, reverse)
        if d * 2 < seq:
            a = a * _shift_rows(a, d, 1.0, reverse)
        d *= 2
    return u


def _rglru_kernel(xr_ref, gr_ref, cw_ref, cb_ref, wa_ref, ba_ref, wx_ref, bx_ref, lam_ref, o_ref, xp_s):
    seq = xr_ref.shape[0]
    pad = SUBLANES
    pad_l = RG_CONV // 2
    xp_s[0:pad, :] = jnp.zeros((pad, LANES), F32)
    xp_s[pad + seq:, :] = jnp.zeros((pad, LANES), F32)
    xp_s[pad:pad + seq, :] = xr_ref[...].astype(F32)
    xc = cb_ref[...]
    for j in range(RG_CONV):
        off = pad + j - pad_l
        xc = xc + xp_s[off:off + seq, :] * cw_ref[j:j + 1, :]
    xcb = xc.astype(BF16)
    h = None
    for d in range(2):
        r = jax.nn.sigmoid(jnp.dot(xcb, wa_ref[d, 0], preferred_element_type=F32) + ba_ref[d:d + 1, :])
        i = jax.nn.sigmoid(jnp.dot(xcb, wx_ref[d, 0], preferred_element_type=F32) + bx_ref[d:d + 1, :])
        log_a = (-RG_C) * r * jax.nn.softplus(-lam_ref[d:d + 1, :])
        a = jnp.exp(log_a)
        u = jnp.sqrt(-jnp.tanh(log_a) * (a * a + 1.0)) * (i * xc)
        hd = _linear_scan(a, u, reverse=(d == 1))
        h = hd if h is None else h + hd
    o_ref[...] = (h * jax.nn.gelu(gr_ref[...].astype(F32))).astype(o_ref.dtype)


def rglru_mixer(proj, conv_w, conv_b, wa, ba, wx, bx, lam, batch, seq):
    n = proj.shape[0]
    x_off = (3 * HG_KEY_WIDTH + 2 * HG_WIDTH) // LANES
    g_off = x_off + RG_WIDTH // LANES
    blk = lambda b, j: (0, j)
    return pl.pallas_call(
        _rglru_kernel,
        grid=(batch, RG_BLOCKS),
        in_specs=[
            pl.BlockSpec((seq, LANES), lambda b, j: (b, x_off + j)),
            pl.BlockSpec((seq, LANES), lambda b, j: (b, g_off + j)),
            pl.BlockSpec((RG_CONV, LANES), blk),
            pl.BlockSpec((1, LANES), blk),
            pl.BlockSpec((2, 1, RG_BLOCK_DIM, RG_BLOCK_DIM), lambda b, j: (0, j, 0, 0)),
            pl.BlockSpec((2, LANES), blk),
            pl.BlockSpec((2, 1, RG_BLOCK_DIM, RG_BLOCK_DIM), lambda b, j: (0, j, 0, 0)),
            pl.BlockSpec((2, LANES), blk),
            pl.BlockSpec((2, LANES), blk),
        ],
        out_specs=pl.BlockSpec((seq, LANES), lambda b, j: (b, j)),
        out_shape=jax.ShapeDtypeStruct((n, RG_WIDTH), BF16),
        scratch_shapes=[pltpu.VMEM((seq + 2 * SUBLANES, LANES), F32)],
        compiler_params=_cparams(("parallel", "parallel")),
        name="rglru_mixer",
    )(proj, proj, conv_w, conv_b.reshape(1, -1), wa.astype(BF16), ba, wx.astype(BF16), bx, lam)


def _outproj_kernel(yh_ref, yr_ref, x_ref, g0_ref, b0_ref, wo_ref, g1_ref, b1_ref, wr_ref,
                    h1_ref, h1p_ref, lg_ref):
    half = yh_ref.shape[1]
    h0 = _layer_norm(x_ref[...], g0_ref[...], b0_ref[...])
    mix = jnp.dot(yh_ref[...], wo_ref[0:half, :].astype(BF16), preferred_element_type=F32)
    mix = mix + jnp.dot(yr_ref[...], wo_ref[half:, :].astype(BF16), preferred_element_type=F32)
    h1 = _layer_norm(DN_ALPHA * h0 + mix, g1_ref[...], b1_ref[...])
    h1_ref[...] = h1
    _store_row_tiles(h1p_ref, _pack_bf16_pair(h1))
    lg_ref[...] = lax.dot_general(wr_ref[...], h1.astype(BF16), (((1,), (1,)), ((), ())),
                                  preferred_element_type=F32)


def outproj_ln_router(y_hg, y_rg, x2, g0, b0, w_out, g1, b1, w_router_t_bf16, tm=256):
    n, d = x2.shape
    half = y_hg.shape[1]
    n_e = w_router_t_bf16.shape[0]
    row = lambda i: (i, 0)
    const = lambda i: (0, 0)
    vec = pl.BlockSpec((1, d), const)
    return pl.pallas_call(
        _outproj_kernel,
        grid=(n // tm,),
        in_specs=[pl.BlockSpec((tm, half), row), pl.BlockSpec((tm, half), row), pl.BlockSpec((tm, d), row),
                  vec, vec, pl.BlockSpec((2 * half, d), const), vec, vec,
                  pl.BlockSpec((n_e, d), const)],
        out_specs=[pl.BlockSpec((tm, d), row), pl.BlockSpec((tm * SUBLANES, LANES), row),
                   pl.BlockSpec((n_e, tm), lambda i: (0, i))],
        out_shape=[jax.ShapeDtypeStruct((n, d), F32), jax.ShapeDtypeStruct((n * SUBLANES, LANES), U32),
                   jax.ShapeDtypeStruct((n_e, n), F32)],
        compiler_params=_cparams(("parallel",)),
        name="outproj_ln_router",
    )(y_hg, y_rg, x2, g0.reshape(1, d), b0.reshape(1, d), w_out, g1.reshape(1, d), b1.reshape(1, d),
      w_router_t_bf16)


def _first_argmax(vals, iota, n):
    m = jnp.max(vals, axis=0, keepdims=True)
    first = jnp.min(jnp.where(vals == m, iota, n), axis=0, keepdims=True)
    return m, first


def _route_kernel(lg_ref, bias_ref, idx_ref, gate_ref):
    n_e, tn = lg_ref.shape
    scores = jax.nn.sigmoid(lg_ref[...])
    choice = scores + bias_ref[...]
    neg = -jnp.inf
    gi = lax.broadcasted_iota(I32, (GROUP_SIZE, tn), 0)
    gs_rows = []
    for g in range(N_GROUPS):
        blk = choice[g * GROUP_SIZE:(g + 1) * GROUP_SIZE]
        m1, f1 = _first_argmax(blk, gi, GROUP_SIZE)
        m2 = jnp.max(jnp.where(gi == f1, neg, blk), axis=0, keepdims=True)
        gs_rows.append(m1 + m2)
    gs = jnp.concatenate(gs_rows, axis=0)
    ri = lax.broadcasted_iota(I32, (N_GROUPS, tn), 0)
    gmask = jnp.zeros((N_GROUPS, tn), jnp.bool_)
    for _ in range(TOPK_GROUPS):
        _, f = _first_argmax(gs, ri, N_GROUPS)
        hit = ri == f
        gmask = gmask | hit
        gs = jnp.where(hit, neg, gs)
    gm = jnp.where(gmask, 1.0, 0.0)
    emask = jnp.concatenate(
        [jnp.broadcast_to(gm[g:g + 1], (GROUP_SIZE, tn)) for g in range(N_GROUPS)], axis=0)
    masked = jnp.where(emask > 0.5, choice, neg)
    ei = lax.broadcasted_iota(I32, (n_e, tn), 0)
    idx_rows, gate_rows = [], []
    for _ in range(TOP_K):
        _, f = _first_argmax(masked, ei, n_e)
        hit = ei == f
        idx_rows.append(f)
        gate_rows.append(jnp.sum(jnp.where(hit, scores, 0.0), axis=0, keepdims=True))
        masked = jnp.where(hit, neg, masked)
    gate = jnp.concatenate(gate_rows, axis=0)
    gate = gate / jnp.sum(gate, axis=0, keepdims=True) * ROUTE_SCALE
    idx_ref[...] = jnp.concatenate(idx_rows, axis=0)
    gate_ref[...] = gate


def route_topk(logits_t, bias, tn=512):
    n_e, n = logits_t.shape
    return pl.pallas_call(
        _route_kernel,
        grid=(n // tn,),
        in_specs=[pl.BlockSpec((n_e, tn), lambda i: (0, i)), pl.BlockSpec((n_e, 1), lambda i: (0, 0))],
        out_specs=[pl.BlockSpec((TOP_K, tn), lambda i: (0, i)), pl.BlockSpec((TOP_K, tn), lambda i: (0, i))],
        out_shape=[jax.ShapeDtypeStruct((TOP_K, n), I32), jax.ShapeDtypeStruct((TOP_K, n), F32)],
        compiler_params=_cparams(("parallel",)),
        name="route_topk",
    )(logits_t, bias.reshape(n_e, 1))


def _rank_kernel(idx_ref, su_ref, rank_ref, cnt_ref, carry_s):
    n_e = cnt_ref.shape[0]
    tn = idx_ref.shape[1]

    @pl.when(pl.program_id(0) == 0)
    def _():
        carry_s[...] = jnp.zeros_like(carry_s)

    idx = idx_ref[...]
    ei = lax.broadcasted_iota(I32, (n_e, tn), 0)
    hits = [ei == idx[k:k + 1] for k in range(TOP_K)]
    member = jnp.zeros((n_e, tn), F32)
    for h in hits:
        member = member + jnp.where(h, 1.0, 0.0)
    prefix = jnp.dot(member.astype(BF16), su_ref[...], preferred_element_type=F32) + carry_s[:, 0:1]
    rank_ref[...] = jnp.concatenate(
        [jnp.sum(jnp.where(h, prefix, 0.0), axis=0, keepdims=True) for h in hits], axis=0).astype(I32)
    carry_s[...] = carry_s[...] + jnp.sum(member, axis=1, keepdims=True)
    cnt_ref[...] = carry_s[...].astype(I32)


def expert_ranks(idx_t, tn=512):
    k, n = idx_t.shape
    su = jnp.asarray(np.triu(np.ones((tn, tn), np.float32), 1), BF16)
    return pl.pallas_call(
        _rank_kernel,
        grid=(n // tn,),
        in_specs=[pl.BlockSpec((k, tn), lambda i: (0, i)), pl.BlockSpec((tn, tn), lambda i: (0, 0))],
        out_specs=[pl.BlockSpec((k, tn), lambda i: (0, i)), pl.BlockSpec((N_EXPERTS, LANES), lambda i: (0, 0))],
        out_shape=[jax.ShapeDtypeStruct((k, n), I32), jax.ShapeDtypeStruct((N_EXPERTS, LANES), I32)],
        scratch_shapes=[pltpu.VMEM((N_EXPERTS, LANES), F32)],
        compiler_params=_cparams(("arbitrary",)),
        name="expert_ranks",
    )(idx_t, su)


def _dest_kernel(idx_ref, rank_ref, start_ref, dest_ref):
    n_e = start_ref.shape[0]
    tn = idx_ref.shape[1]
    idx = idx_ref[...]
    ei = lax.broadcasted_iota(I32, (n_e, tn), 0)
    start = start_ref[...]
    base = jnp.concatenate(
        [jnp.sum(jnp.where(ei == idx[k:k + 1], start, 0.0), axis=0, keepdims=True) for k in range(TOP_K)],
        axis=0)
    dest_ref[...] = base.astype(I32) + rank_ref[...]


def assignment_rows(idx_t, rank_t, padded_start, tn=512):
    k, n = idx_t.shape
    blk = pl.BlockSpec((k, tn), lambda i: (0, i))
    return pl.pallas_call(
        _dest_kernel,
        grid=(n // tn,),
        in_specs=[blk, blk, pl.BlockSpec((N_EXPERTS, 1), lambda i: (0, 0))],
        out_specs=blk,
        out_shape=jax.ShapeDtypeStruct((k, n), I32),
        compiler_params=_cparams(("parallel",)),
        name="assignment_rows",
    )(idx_t, rank_t, padded_start.astype(F32).reshape(N_EXPERTS, 1))


WEIGHT_DMA_PRIORITY = 1
WEIGHT_SLOTS = 3
ROW_GROUP = 8
ROW_CHUNK = 8
BLOCKS_PER_STEP = 2


def _rowmap_kernel(dest_ref, tok_ref):
    n_tok = dest_ref.shape[0] // TOP_K

    def body(j, c):
        for k in range(TOP_K):
            tok_ref[dest_ref[k * n_tok + j]] = j
        return c

    lax.fori_loop(0, n_tok, body, 0, unroll=4)


def row_tokens(dest_flat, n_rows):
    return pl.pallas_call(
        _rowmap_kernel,
        grid_spec=pltpu.PrefetchScalarGridSpec(
            num_scalar_prefetch=1,
            grid=(1,),
            in_specs=[],
            out_specs=pl.BlockSpec(memory_space=pltpu.SMEM),
        ),
        out_shape=jax.ShapeDtypeStruct((n_rows,), I32),
        compiler_params=_cparams(("arbitrary",)),
        name="row_tokens",
    )(dest_flat)


def _experts_kernel(be_ref, nxt_ref, nxt2_ref, slot_ref, cnt_ref, tok_ref, nb_ref,
                    h1p_ref, wg_ref, wu_ref, wd_ref, ys_ref,
                    xbuf, zrows, wgb, wub, wdb, gsem, wsem):
    tm = ys_ref.shape[0] // (SUBLANES * BLOCKS_PER_STEP)
    step = pl.program_id(0)
    nb = nb_ref[0]

    def weight_copies(e, slot):
        return (pltpu.make_async_copy(wg_ref.at[e], wgb.at[slot], wsem.at[slot]),
                pltpu.make_async_copy(wu_ref.at[e], wub.at[slot], wsem.at[slot]),
                pltpu.make_async_copy(wd_ref.at[e], wdb.at[slot], wsem.at[slot]))

    def start_weights(e, slot):
        for cp in weight_copies(e, slot):
            cp.start(priority=WEIGHT_DMA_PRIORITY)

    def start_group(grp, slot):
        def one_block(g, carry):
            blk = jnp.minimum(grp * ROW_GROUP + g, nb - 1)
            cnt = cnt_ref[blk]
            last = cnt - 1
            base = blk * tm
            n_chunks = lax.div(cnt + (ROW_CHUNK - 1), ROW_CHUNK)

            def issue(ch, c):
                for j in range(ROW_CHUNK):
                    r = ch * ROW_CHUNK + j
                    t = tok_ref[base + jnp.minimum(r, last)]
                    pltpu.make_async_copy(h1p_ref.at[pl.ds(t * SUBLANES, SUBLANES)],
                                          xbuf.at[slot, pl.ds((g * tm + r) * SUBLANES, SUBLANES)],
                                          gsem.at[slot]).start()
                return c

            lax.fori_loop(0, n_chunks, issue, 0)
            done = n_chunks * ROW_CHUNK
            pad = tm - done
            size = tm // 2
            while size >= ROW_CHUNK:
                @pl.when((pad & size) != 0)
                def _(done=done, size=size):
                    pltpu.make_async_copy(
                        zrows.at[pl.ds(0, size * SUBLANES)],
                        xbuf.at[slot, pl.ds((g * tm + done) * SUBLANES, size * SUBLANES)],
                        gsem.at[slot]).start()
                done = done + jnp.where((pad & size) != 0, size, 0)
                size //= 2
            return carry

        lax.fori_loop(0, ROW_GROUP, one_block, 0)

    def process(i, out_ref):
        e = be_ref[i]
        wslot = slot_ref[i]
        grp = lax.div(i, ROW_GROUP)
        sub = lax.rem(i, ROW_GROUP)
        gslot = lax.rem(grp, 2)

        @pl.when(i == 0)
        def _():
            start_weights(e, wslot)
            zrows[...] = jnp.zeros_like(zrows)
            nxt = nxt_ref[0]

            @pl.when(nxt >= 0)
            def _():
                start_weights(nxt, lax.rem(wslot + 1, WEIGHT_SLOTS))

            start_group(0, 0)

        @pl.when((i == 0) | (e != be_ref[jnp.maximum(i - 1, 0)]))
        def _():
            for cp in weight_copies(e, wslot):
                cp.wait()
            nxt2 = nxt2_ref[i]

            @pl.when(nxt2 >= 0)
            def _():
                start_weights(nxt2, lax.rem(wslot + 2, WEIGHT_SLOTS))

        @pl.when(sub == 0)
        def _():
            pltpu.make_async_copy(h1p_ref.at[pl.ds(0, ROW_GROUP * tm * SUBLANES)], xbuf.at[gslot],
                                  gsem.at[gslot]).wait()

            @pl.when((grp + 1) * ROW_GROUP < nb)
            def _():
                start_group(grp + 1, 1 - gslot)

        row0 = pl.multiple_of(sub * (tm * SUBLANES), tm * SUBLANES)
        x = _unpack_bf16_pair(_load_row_tiles(xbuf.at[gslot, pl.ds(row0, tm * SUBLANES)], tm))
        a = jnp.dot(x, wgb[wslot].astype(BF16), preferred_element_type=F32)
        u = jnp.dot(x, wub[wslot].astype(BF16), preferred_element_type=F32)
        hid = (jax.nn.silu(a) * u).astype(BF16)
        y = jnp.dot(hid, wdb[wslot].astype(BF16), preferred_element_type=F32)
        _store_row_tiles(out_ref, _pack_bf16_pair(y))

    for h in range(BLOCKS_PER_STEP):
        blk = step * BLOCKS_PER_STEP + h

        @pl.when(blk < nb)
        def _(blk=blk, h=h):
            process(blk, ys_ref.at[pl.ds(h * tm * SUBLANES, tm * SUBLANES)])


def grouped_experts(h1p_tiles, row_tok, block_expert, block_next, block_next2, block_slot, block_count,
                    n_blocks_used, w_gate, w_up, w_down, tm=MOE_BLOCK):
    n_rows = row_tok.shape[0]
    n_e, d, ff = w_gate.shape
    n_blocks = n_rows // tm
    assert n_blocks % BLOCKS_PER_STEP == 0
    bps = BLOCKS_PER_STEP
    rowmap = lambda i, be, nx, nx2, sl, cn, tok, nb: (jnp.minimum(i, jnp.maximum(nb[0] - 1, 0) // bps), 0)
    hbm = pl.BlockSpec(memory_space=pl.ANY)
    return pl.pallas_call(
        _experts_kernel,
        grid_spec=pltpu.PrefetchScalarGridSpec(
            num_scalar_prefetch=7,
            grid=(n_blocks // bps,),
            in_specs=[hbm, hbm, hbm, hbm],
            out_specs=pl.BlockSpec((bps * tm * SUBLANES, LANES), rowmap),
            scratch_shapes=[pltpu.VMEM((2, ROW_GROUP * tm * SUBLANES, LANES), U32),
                            pltpu.VMEM((tm // 2 * SUBLANES, LANES), U32),
                            pltpu.VMEM((WEIGHT_SLOTS, d, ff), F32), pltpu.VMEM((WEIGHT_SLOTS, d, ff), F32),
                            pltpu.VMEM((WEIGHT_SLOTS, ff, d), F32),
                            pltpu.SemaphoreType.DMA((2,)), pltpu.SemaphoreType.DMA((WEIGHT_SLOTS,))],
        ),
        out_shape=jax.ShapeDtypeStruct((n_rows * SUBLANES, LANES), U32),
        compiler_params=_cparams(("arbitrary",)),
        name="grouped_experts",
    )(block_expert, block_next, block_next2, block_slot, block_count, row_tok, n_blocks_used,
      h1p_tiles, w_gate, w_up, w_down)


def _combine_kernel(dest_ref, h1_ref, h1p_ref, gate_ref, wsg_ref, wsu_ref, wsd_ref, g2_ref, b2_ref, ys_ref,
                    o_ref, buf_s, sem):
    tm = h1_ref.shape[0]
    i = pl.program_id(0)
    n_steps = pl.num_programs(0)
    n_tok = n_steps * tm
    slot = lax.rem(i, 2)

    def start_rows(step, sl):
        def issue(r, c):
            for k in range(TOP_K):
                row = dest_ref[k * n_tok + step * tm + r]
                pltpu.make_async_copy(ys_ref.at[pl.ds(row * SUBLANES, SUBLANES)],
                                      buf_s.at[sl, k, pl.ds(r * SUBLANES, SUBLANES)], sem.at[sl]).start()
            return c

        lax.fori_loop(0, tm, issue, 0, unroll=2)

    @pl.when(i == 0)
    def _():
        start_rows(0, 0)

    @pl.when(i + 1 < n_steps)
    def _():
        start_rows(i + 1, 1 - slot)

    xb = _unpack_bf16_pair(_load_row_tiles(h1p_ref, tm))
    a = jnp.dot(xb, wsg_ref[...].astype(BF16), preferred_element_type=F32)
    u = jnp.dot(xb, wsu_ref[...].astype(BF16), preferred_element_type=F32)
    shared = jnp.dot((jax.nn.silu(a) * u).astype(BF16), wsd_ref[...].astype(BF16),
                     preferred_element_type=F32)

    for k in range(TOP_K):
        pltpu.make_async_copy(ys_ref.at[pl.ds(0, tm * SUBLANES)], buf_s.at[slot, k], sem.at[slot]).wait()

    gate = gate_ref[...]
    routed = jnp.zeros_like(shared)
    for k in range(TOP_K):
        routed = routed + _unpack_pair_f32(_load_row_tiles(buf_s.at[slot, k], tm)) * gate[:, k:k + 1]
    o_ref[...] = _layer_norm(DN_ALPHA * h1_ref[...] + routed + shared, g2_ref[...], b2_ref[...])


def combine_shared_ln(dest_flat, h1, h1p_tiles, gate_nk, ys_tiles, wsg, wsu, wsd, g2, b2, tm=128):
    n, d = h1.shape
    ff = wsg.shape[1]
    row = lambda i, dest: (i, 0)
    const = lambda i, dest: (0, 0)
    return pl.pallas_call(
        _combine_kernel,
        grid_spec=pltpu.PrefetchScalarGridSpec(
            num_scalar_prefetch=1,
            grid=(n // tm,),
            in_specs=[pl.BlockSpec((tm, d), row), pl.BlockSpec((tm * SUBLANES, LANES), row),
                      pl.BlockSpec((tm, TOP_K), row),
                      pl.BlockSpec((d, ff), const), pl.BlockSpec((d, ff), const), pl.BlockSpec((ff, d), const),
                      pl.BlockSpec((1, d), const), pl.BlockSpec((1, d), const),
                      pl.BlockSpec(memory_space=pl.ANY)],
            out_specs=pl.BlockSpec((tm, d), row),
            scratch_shapes=[pltpu.VMEM((2, TOP_K, tm * SUBLANES, LANES), U32), pltpu.SemaphoreType.DMA((2,))],
        ),
        out_shape=jax.ShapeDtypeStruct((n, d), F32),
        compiler_params=_cparams(("arbitrary",)),
        name="combine_shared_ln",
    )(dest_flat, h1, h1p_tiles, gate_nk, wsg, wsu, wsd, g2.reshape(1, d), b2.reshape(1, d), ys_tiles)


def kernel(x, ln_emb_g, ln_emb_b, w_in, lb_logits, hg_norm_w, conv_w, conv_b, rg_wa, rg_ba, rg_wx, rg_bx,
           rg_lam, w_out, ln1_g, ln1_b, w_router, router_bias, w_gate, w_up, w_down, ws_gate, ws_up, ws_down,
           ln2_g, ln2_b):
    batch, seq, d = x.shape
    assert d == 2 * SUBLANES * LANES, "packed rows must fill one 8x128 tile"
    n_tok = batch * seq
    layer = 0
    x2 = x.reshape(n_tok, d)
    lb = jnp.cumsum(jax.nn.softmax(lb_logits.astype(F32), axis=1), axis=1)

    proj = ln_inproj(x2, ln_emb_g, ln_emb_b, w_in[layer])
    y_hg = hgrn2_mixer(proj, lb[0, layer], lb[1, layer], hg_norm_w[layer], batch, seq)
    y_rg = rglru_mixer(proj, conv_w[layer], conv_b[layer], rg_wa[layer], rg_ba[layer], rg_wx[layer],
                       rg_bx[layer], rg_lam[layer], batch, seq)
    h1, h1p_tiles, logits_t = outproj_ln_router(
        y_hg, y_rg, x2, ln_emb_g, ln_emb_b, w_out[layer], ln1_g[layer], ln1_b[layer],
        w_router[layer].T.astype(BF16))

    idx_t, gate_t = route_topk(logits_t, router_bias[layer])
    rank_t, counts = expert_ranks(idx_t)
    counts = counts[:, 0]

    n_assign = n_tok * TOP_K
    n_blocks = -(-(n_assign + N_EXPERTS * (MOE_BLOCK - 1)) // MOE_BLOCK)
    n_rows = n_blocks * MOE_BLOCK
    padded = (counts + MOE_BLOCK - 1) // MOE_BLOCK * MOE_BLOCK
    padded_end = jnp.cumsum(padded)
    padded_start = padded_end - padded
    n_used_blocks = (padded_end[-1:] // MOE_BLOCK).astype(I32)
    blk_row = jnp.minimum(jnp.arange(n_blocks, dtype=I32), n_used_blocks[0] - 1) * MOE_BLOCK
    block_expert = jnp.minimum(
        jnp.sum((padded_end[None, :] <= blk_row[:, None]).astype(I32), axis=1), N_EXPERTS - 1)
    e_ids = jnp.arange(N_EXPERTS, dtype=I32)
    active = counts > 0
    later = lax.cummin(jnp.where(active, e_ids, N_EXPERTS)[::-1])[::-1]
    nxt = jnp.concatenate([later[1:], jnp.full((1,), N_EXPERTS, I32)])
    next1 = jnp.where(nxt < N_EXPERTS, nxt, -1).astype(I32)
    hot1 = (next1[:, None] == e_ids[None, :]).astype(I32)
    next2 = jnp.sum(hot1 * (next1 + 1)[None, :], axis=1) - 1
    ordinal = jnp.cumsum(active.astype(I32)) - 1
    onehot = (block_expert[:, None] == e_ids[None, :]).astype(I32)
    block_next = jnp.sum(onehot * next1[None, :], axis=1)
    block_next2 = jnp.sum(onehot * next2[None, :], axis=1)
    block_slot = jnp.sum(onehot * (ordinal % WEIGHT_SLOTS)[None, :], axis=1)
    block_count = jnp.clip(jnp.sum(onehot * (counts + padded_start)[None, :], axis=1) - blk_row, 1, MOE_BLOCK)

    dest_flat = assignment_rows(idx_t, rank_t, padded_start).reshape(-1)
    row_tok = row_tokens(dest_flat, n_rows)

    ys_tiles = grouped_experts(h1p_tiles, row_tok, block_expert, block_next, block_next2, block_slot,
                               block_count, n_used_blocks, w_gate[layer], w_up[layer], w_down[layer])
    out = combine_shared_ln(dest_flat, h1, h1p_tiles, gate_t.T, ys_tiles,
                            ws_gate[layer], ws_up[layer], ws_down[layer], ln2_g[layer], ln2_b[layer])
    return out.reshape(batch, seq, d)
```

```python
import functools

import numpy as np
import jax
import jax.numpy as jnp
from jax import lax
from jax.experimental import pallas as pl
from jax.experimental.pallas import tpu as pltpu

F32 = jnp.float32
BF16 = jnp.bfloat16
U32 = jnp.uint32
I32 = jnp.int32

D_MODEL = 2048
DEPTH = 1
HG_KEY_DIM = 128
HG_VAL_DIM = 128
HG_WIDTH = D_MODEL // 2
HG_HEADS = HG_WIDTH // HG_VAL_DIM
HG_KEY_WIDTH = HG_HEADS * HG_KEY_DIM
RG_WIDTH = D_MODEL - HG_WIDTH
RG_BLOCKS = 8
RG_BLOCK_DIM = RG_WIDTH // RG_BLOCKS
RG_CONV = 4
RG_C = 8.0
MIX_WIDTH = HG_WIDTH + RG_WIDTH
IN_WIDTHS = (HG_KEY_WIDTH, HG_KEY_WIDTH, HG_KEY_WIDTH, HG_WIDTH, HG_WIDTH, RG_WIDTH, RG_WIDTH)
IN_WIDTH = sum(IN_WIDTHS)
N_EXPERTS = 256
TOP_K = 8
N_GROUPS = 8
TOPK_GROUPS = 4
GROUP_SIZE = N_EXPERTS // N_GROUPS
EXPERT_FF = 512
SHARED_FF = 512
ROUTE_SCALE = 2.5
MOE_BLOCK = 128
DN_ALPHA = (2 * DEPTH) ** 0.25
LN_EPS = 1e-5
RMS_EPS = 1e-6

LANES = 128
SUBLANES = 8
VMEM_LIMIT_BYTES = 56 * 1024 * 1024

HG_CHUNK = 128
HG_SUB = 8
HG_LEVELS = (16, 32, 64, 128)


def _cparams(sem, vmem=VMEM_LIMIT_BYTES):
    return pltpu.CompilerParams(dimension_semantics=sem, vmem_limit_bytes=vmem)


def _layer_norm(x, g, b):
    mu = jnp.mean(x, axis=-1, keepdims=True)
    xc = x - mu
    var = jnp.mean(xc * xc, axis=-1, keepdims=True)
    return xc * lax.rsqrt(var + LN_EPS) * g + b


def _pack_bf16_pair(x):
    n = x.shape[1] // 2
    bits = lax.bitcast_convert_type(x.astype(BF16).astype(F32), U32)
    return (bits[:, :n] >> 16) | (bits[:, n:] & jnp.uint32(0xFFFF0000))


def _unpack_pair_f32(w):
    lo = lax.bitcast_convert_type(w << 16, F32)
    hi = lax.bitcast_convert_type(w & jnp.uint32(0xFFFF0000), F32)
    return jnp.concatenate([lo, hi], axis=1)


def _unpack_bf16_pair(w):
    return _unpack_pair_f32(w).astype(BF16)


def _store_row_tiles(ref, packed):
    m = packed.shape[0]
    for s in range(SUBLANES):
        ref[pl.ds(s, m, stride=SUBLANES), :] = packed[:, s * LANES:(s + 1) * LANES]


def _load_row_tiles(ref, m):
    return jnp.concatenate([ref[pl.ds(s, m, stride=SUBLANES), :] for s in range(SUBLANES)], axis=1)


def _ln_inproj_kernel(x_ref, g_ref, b_ref, w_ref, o_ref, hb_ref):
    @pl.when(pl.program_id(1) == 0)
    def _():
        hb_ref[...] = _layer_norm(x_ref[...], g_ref[...], b_ref[...]).astype(BF16)

    o_ref[...] = jnp.dot(hb_ref[...], w_ref[...].astype(BF16), preferred_element_type=F32).astype(o_ref.dtype)


def ln_inproj(x2, g, b, w_in, tm=1024, tn=1024):
    n, d = x2.shape
    width = w_in.shape[1]
    return pl.pallas_call(
        _ln_inproj_kernel,
        grid=(n // tm, width // tn),
        in_specs=[
            pl.BlockSpec((tm, d), lambda i, j: (i, 0)),
            pl.BlockSpec((1, d), lambda i, j: (0, 0)),
            pl.BlockSpec((1, d), lambda i, j: (0, 0)),
            pl.BlockSpec((d, tn), lambda i, j: (0, j)),
        ],
        out_specs=pl.BlockSpec((tm, tn), lambda i, j: (i, j)),
        out_shape=jax.ShapeDtypeStruct((n, width), BF16),
        scratch_shapes=[pltpu.VMEM((tm, d), BF16)],
        compiler_params=_cparams(("parallel", "arbitrary")),
        name="ln_inproj",
    )(x2, g.reshape(1, d), b.reshape(1, d), w_in)


def _hgrn2_constants():
    c = HG_CHUNK
    t = np.arange(c)[:, None]
    u = np.arange(c)[None, :]
    fwd = [(u <= t), (u > t)]
    bwd = [(u >= t), (u < t)]
    for m in HG_LEVELS:
        start = (t // m) * m
        mid = start + m // 2 - 1
        second = t > mid
        fwd.append(np.where(second, (u > mid) & (u <= t), (u > t) & (u <= mid)))
        bwd.append(np.where(second, (u > mid) & (u < t), (u >= t) & (u <= mid)))
    fwd = np.concatenate(fwd, axis=0).astype(np.float32)
    bwd = np.concatenate(bwd, axis=0).astype(np.float32)
    fwd = np.concatenate([fwd, fwd], axis=1)
    bwd = np.concatenate([bwd, bwd], axis=1)
    lvl = np.full((c, c), -1, np.int32)
    tt = np.arange(c)[:, None]
    ss = np.arange(c)[None, :]
    for li in reversed(range(len(HG_LEVELS))):
        m = HG_LEVELS[li]
        lvl = np.where((tt // m) == (ss // m), li, lvl)
    lvl = np.where((tt // HG_SUB) == (ss // HG_SUB), -1, lvl).astype(np.int32)
    return fwd, bwd, lvl


def _hgrn2_kernel(q_ref, zf_ref, zb_ref, v_ref, g_ref, lbf_ref, lbb_ref, nw_ref,
                  mf_ref, mb_ref, lvl_ref, o_ref,
                  qs_s, v_s, lf_s, kf_s, lb_s, kb_s, cf_s, cb_s,
                  qif_s, kif_s, qib_s, kib_s, df_s, db_s, acc_s, stf_s, stb_s):
    seq = q_ref.shape[0]
    c = HG_CHUNK
    n_chunks = seq // c
    n_lvl = len(HG_LEVELS)

    qs_s[...] = jax.nn.silu(q_ref[...].astype(F32))
    v_s[...] = v_ref[...].astype(F32)
    for z_ref, lb_ref, lf_out, k_out in ((zf_ref, lbf_ref, lf_s, kf_s), (zb_ref, lbb_ref, lb_s, kb_s)):
        z = z_ref[...].astype(F32)
        lb = lb_ref[...]
        sig = jax.nn.sigmoid(z)
        lf_out[...] = jnp.log(lb + (1.0 - lb) * sig)
        k_out[...] = (1.0 - lb) * (1.0 - sig)

    ones_mat = jnp.ones((LANES, LANES), BF16)
    lvl = lvl_ref[...]
    row_c = lax.broadcasted_iota(I32, (c, LANES), 0)
    lane8 = lax.broadcasted_iota(I32, (HG_SUB, LANES), 1)
    sub8 = lax.broadcasted_iota(I32, (HG_SUB, LANES), 0)
    n_stack = (2 + n_lvl) * c

    def intra(n, ef, eb):
        r0 = pl.multiple_of(n * c, c)
        rows = pl.ds(r0, c)
        qs = qs_s[rows, :]
        kf = kf_s[rows, :]
        kb = kb_s[rows, :]
        for (e, kk, cs, qi_s, ki_s, d_s, tot_row) in ((ef, kf, cf_s, qif_s, kif_s, df_s, c - 1),
                                                    (eb, kb, cb_s, qib_s, kib_s, db_s, 0)):
            cum = e[0:c]
            cs[rows, :] = cum
            qi_s[rows, :] = (qs * jnp.exp(cum)).astype(BF16)
            ki_s[rows, :] = (kk * jnp.exp(e[c:2 * c])).astype(BF16)
            d_s[pl.ds(n, 1), :] = jnp.exp(cum[tot_row:tot_row + 1, :])

        a_mat = jnp.zeros((c, c), F32)
        for li, m in enumerate(HG_LEVELS):
            second = (row_c & (m // 2)) != 0
            wf = jnp.exp(ef[(2 + li) * c:(3 + li) * c])
            wb = jnp.exp(eb[(2 + li) * c:(3 + li) * c])
            qcat = jnp.concatenate([qs * jnp.where(second, wf, 0.0), qs * jnp.where(second, 0.0, wb)], axis=1)
            kcat = jnp.concatenate([kf * jnp.where(second, 0.0, wf), kb * jnp.where(second, wb, 0.0)], axis=1)
            sc = lax.dot_general(qcat.astype(BF16), kcat.astype(BF16),
                                 (((1,), (1,)), ((), ())), preferred_element_type=F32)
            a_mat = jnp.where(lvl == li, sc, a_mat)

        p_list = []
        for j in range(c // HG_SUB):
            rj = r0 + j * HG_SUB
            sl = pl.ds(rj, HG_SUB)
            cf = cf_s[sl, :]
            cb = cb_s[sl, :]
            qj = qs_s[sl, :]
            for s in range(HG_SUB):
                src = pl.ds(rj + s, 1)
                df = jnp.where(sub8 >= s, cf - cf_s[src, :], -jnp.inf)
                db = jnp.where(sub8 <= s, cb - cb_s[src, :], -jnp.inf)
                p_list.append(qj * (kf_s[src, :] * jnp.exp(df) + kb_s[src, :] * jnp.exp(db)))
        p_all = jnp.concatenate(p_list, axis=0).astype(BF16)
        red = jnp.dot(p_all, ones_mat, preferred_element_type=F32)
        diag_rows = []
        for j in range(c // HG_SUB):
            blk = jnp.zeros((HG_SUB, LANES), F32)
            for s in range(HG_SUB):
                idx = (j * HG_SUB + s) * HG_SUB
                blk = jnp.where(lane8 == j * HG_SUB + s, red[idx:idx + HG_SUB], blk)
            diag_rows.append(blk)
        a_mat = a_mat + jnp.concatenate(diag_rows, axis=0)
        acc_s[rows, :] = jnp.dot(a_mat.astype(BF16), v_s[rows, :].astype(BF16), preferred_element_type=F32)

    def intra_pair(n2, carry):
        r0 = pl.multiple_of(n2 * (2 * c), 2 * c)
        hls = []
        for lfs in (lf_s, lb_s):
            cols = []
            for j in range(2):
                lf = lfs[pl.ds(r0 + j * c, c), :]
                hi = lf.astype(BF16)
                cols.append(jnp.concatenate([hi, (lf - hi.astype(F32)).astype(BF16)], axis=0))
            hls.append(jnp.concatenate(cols, axis=1))
        ef2 = jnp.dot(mf_ref[...], hls[0], preferred_element_type=F32)
        eb2 = jnp.dot(mb_ref[...], hls[1], preferred_element_type=F32)
        for j in range(2):
            intra(2 * n2 + j, ef2[:, j * LANES:(j + 1) * LANES], eb2[:, j * LANES:(j + 1) * LANES])
        return carry

    lax.fori_loop(0, n_chunks // 2, intra_pair, 0, unroll=4)

    stf_s[...] = jnp.zeros_like(stf_s)
    stb_s[...] = jnp.zeros_like(stb_s)

    def inter(n, carry):
        for (st_s, qi_s, ki_s, d_s, idx) in ((stf_s, qif_s, kif_s, df_s, n),
                                             (stb_s, qib_s, kib_s, db_s, n_chunks - 1 - n)):
            rows = pl.ds(pl.multiple_of(idx * c, c), c)
            st = st_s[...]
            acc_s[rows, :] += lax.dot_general(qi_s[rows, :], st.astype(BF16),
                                              (((1,), (1,)), ((), ())), preferred_element_type=F32)
            upd = lax.dot_general(v_s[rows, :].astype(BF16), ki_s[rows, :],
                                  (((0,), (0,)), ((), ())), preferred_element_type=F32)
            st_s[...] = st * d_s[pl.ds(idx, 1), :] + upd
        return carry

    lax.fori_loop(0, n_chunks, inter, 0, unroll=4)

    o = acc_s[...]
    o = o * lax.rsqrt(jnp.mean(o * o, axis=-1, keepdims=True) + RMS_EPS) * nw_ref[...]
    o_ref[...] = (o * jax.nn.silu(g_ref[...].astype(F32))).astype(o_ref.dtype)


def hgrn2_mixer(proj, lb_f, lb_b, norm_w, batch, seq):
    n = proj.shape[0]
    heads = HG_HEADS
    c = HG_CHUNK
    n_chunks = seq // c
    mf, mb, lvl = _hgrn2_constants()
    n_stack = mf.shape[0]

    def col(off):
        return pl.BlockSpec((seq, LANES), lambda b, h, off=off: (b, off + h))

    def per_head(arr):
        return pl.BlockSpec((1, LANES), lambda b, h: (0, h))

    const2 = lambda b, h: (0, 0)
    seq_f32 = pltpu.VMEM((seq, LANES), F32)
    seq_bf16 = pltpu.VMEM((seq, LANES), BF16)
    dec = pltpu.VMEM((n_chunks, LANES), F32)
    state = pltpu.VMEM((LANES, LANES), F32)
    return pl.pallas_call(
        _hgrn2_kernel,
        grid=(batch, heads),
        in_specs=[col(0), col(heads), col(2 * heads), col(3 * heads), col(4 * heads),
                  per_head(lb_f), per_head(lb_b),
                  pl.BlockSpec((1, LANES), const2),
                  pl.BlockSpec((n_stack, 2 * c), const2),
                  pl.BlockSpec((n_stack, 2 * c), const2),
                  pl.BlockSpec((c, c), const2)],
        out_specs=pl.BlockSpec((seq, LANES), lambda b, h: (b, h)),
        out_shape=jax.ShapeDtypeStruct((n, HG_WIDTH), BF16),
        scratch_shapes=[seq_f32] * 8 + [seq_bf16] * 4 + [dec, dec, seq_f32, state, state],
        compiler_params=_cparams(("parallel", "parallel")),
        name="hgrn2_mixer",
    )(proj, proj, proj, proj, proj, lb_f.reshape(1, -1), lb_b.reshape(1, -1), norm_w.reshape(1, -1),
      jnp.asarray(mf, BF16), jnp.asarray(mb, BF16), jnp.asarray(lvl))


def _shift_rows(x, d, fill, reverse):
    pad = jnp.full((d, x.shape[1]), fill, x.dtype)
    if reverse:
        return jnp.concatenate([x[d:], pad], axis=0)
    return jnp.concatenate([pad, x[:-d]], axis=0)


def _linear_scan(a, u, reverse):
    seq = a.shape[0]
    d = 1
    while d < seq:
        u = u + a * _shift_rows(u, d, 0.0, reverse)
        if d * 2 < seq:
            a = a * _shift_rows(a, d, 1.0, reverse)
        d *= 2
    return u


def _rglru_kernel(xr_ref, gr_ref, cw_ref, cb_ref, wa_ref, ba_ref, wx_ref, bx_ref, lam_ref, o_ref, xp_s):
    seq = xr_ref.shape[0]
    pad = SUBLANES
    pad_l = RG_CONV // 2
    xp_s[0:pad, :] = jnp.zeros((pad, LANES), F32)
    xp_s[pad + seq:, :] = jnp.zeros((pad, LANES), F32)
    xp_s[pad:pad + seq, :] = xr_ref[...].astype(F32)
    xc = cb_ref[...]
    for j in range(RG_CONV):
        off = pad + j - pad_l
        xc = xc + xp_s[off:off + seq, :] * cw_ref[j:j + 1, :]
    xcb = xc.astype(BF16)
    h = None
    for d in range(2):
        r = jax.nn.sigmoid(jnp.dot(xcb, wa_ref[d, 0], preferred_element_type=F32) + ba_ref[d:d + 1, :])
        i = jax.nn.sigmoid(jnp.dot(xcb, wx_ref[d, 0], preferred_element_type=F32) + bx_ref[d:d + 1, :])
        log_a = (-RG_C) * r * jax.nn.softplus(-lam_ref[d:d + 1, :])
        a = jnp.exp(log_a)
        u = jnp.sqrt(-jnp.tanh(log_a) * (a * a + 1.0)) * (i * xc)
        hd = _linear_scan(a, u, reverse=(d == 1))
        h = hd if h is None else h + hd
    o_ref[...] = (h * jax.nn.gelu(gr_ref[...].astype(F32))).astype(o_ref.dtype)


def rglru_mixer(proj, conv_w, conv_b, wa, ba, wx, bx, lam, batch, seq):
    n = proj.shape[0]
    x_off = (3 * HG_KEY_WIDTH + 2 * HG_WIDTH) // LANES
    g_off = x_off + RG_WIDTH // LANES
    blk = lambda b, j: (0, j)
    return pl.pallas_call(
        _rglru_kernel,
        grid=(batch, RG_BLOCKS),
        in_specs=[
            pl.BlockSpec((seq, LANES), lambda b, j: (b, x_off + j)),
            pl.BlockSpec((seq, LANES), lambda b, j: (b, g_off + j)),
            pl.BlockSpec((RG_CONV, LANES), blk),
            pl.BlockSpec((1, LANES), blk),
            pl.BlockSpec((2, 1, RG_BLOCK_DIM, RG_BLOCK_DIM), lambda b, j: (0, j, 0, 0)),
            pl.BlockSpec((2, LANES), blk),
            pl.BlockSpec((2, 1, RG_BLOCK_DIM, RG_BLOCK_DIM), lambda b, j: (0, j, 0, 0)),
            pl.BlockSpec((2, LANES), blk),
            pl.BlockSpec((2, LANES), blk),
        ],
        out_specs=pl.BlockSpec((seq, LANES), lambda b, j: (b, j)),
        out_shape=jax.ShapeDtypeStruct((n, RG_WIDTH), BF16),
        scratch_shapes=[pltpu.VMEM((seq + 2 * SUBLANES, LANES), F32)],
        compiler_params=_cparams(("parallel", "parallel")),
        name="rglru_mixer",
    )(proj, proj, conv_w, conv_b.reshape(1, -1), wa.astype(BF16), ba, wx.astype(BF16), bx, lam)


def _outproj_kernel(yh_ref, yr_ref, x_ref, g0_ref, b0_ref, wo_ref, g1_ref, b1_ref, wr_ref,
                    h1_ref, h1p_ref, lg_ref):
    half = yh_ref.shape[1]
    h0 = _layer_norm(x_ref[...], g0_ref[...], b0_ref[...])
    mix = jnp.dot(yh_ref[...], wo_ref[0:half, :].astype(BF16), preferred_element_type=F32)
    mix = mix + jnp.dot(yr_ref[...], wo_ref[half:, :].astype(BF16), preferred_element_type=F32)
    h1 = _layer_norm(DN_ALPHA * h0 + mix, g1_ref[...], b1_ref[...])
    h1_ref[...] = h1
    _store_row_tiles(h1p_ref, _pack_bf16_pair(h1))
    lg_ref[...] = lax.dot_general(wr_ref[...], h1.astype(BF16), (((1,), (1,)), ((), ())),
                                  preferred_element_type=F32)


def outproj_ln_router(y_hg, y_rg, x2, g0, b0, w_out, g1, b1, w_router_t_bf16, tm=256):
    n, d = x2.shape
    half = y_hg.shape[1]
    n_e = w_router_t_bf16.shape[0]
    row = lambda i: (i, 0)
    const = lambda i: (0, 0)
    vec = pl.BlockSpec((1, d), const)
    return pl.pallas_call(
        _outproj_kernel,
        grid=(n // tm,),
        in_specs=[pl.BlockSpec((tm, half), row), pl.BlockSpec((tm, half), row), pl.BlockSpec((tm, d), row),
                  vec, vec, pl.BlockSpec((2 * half, d), const), vec, vec,
                  pl.BlockSpec((n_e, d), const)],
        out_specs=[pl.BlockSpec((tm, d), row), pl.BlockSpec((tm * SUBLANES, LANES), row),
                   pl.BlockSpec((n_e, tm), lambda i: (0, i))],
        out_shape=[jax.ShapeDtypeStruct((n, d), F32), jax.ShapeDtypeStruct((n * SUBLANES, LANES), U32),
                   jax.ShapeDtypeStruct((n_e, n), F32)],
        compiler_params=_cparams(("parallel",)),
        name="outproj_ln_router",
    )(y_hg, y_rg, x2, g0.reshape(1, d), b0.reshape(1, d), w_out, g1.reshape(1, d), b1.reshape(1, d),
      w_router_t_bf16)


def _first_argmax(vals, iota, n):
    m = jnp.max(vals, axis=0, keepdims=True)
    first = jnp.min(jnp.where(vals == m, iota, n), axis=0, keepdims=True)
    return m, first


def _route_kernel(lg_ref, bias_ref, idx_ref, gate_ref):
    n_e, tn = lg_ref.shape
    scores = jax.nn.sigmoid(lg_ref[...])
    choice = scores + bias_ref[...]
    neg = -jnp.inf
    gi = lax.broadcasted_iota(I32, (GROUP_SIZE, tn), 0)
    gs_rows = []
    for g in range(N_GROUPS):
        blk = choice[g * GROUP_SIZE:(g + 1) * GROUP_SIZE]
        m1, f1 = _first_argmax(blk, gi, GROUP_SIZE)
        m2 = jnp.max(jnp.where(gi == f1, neg, blk), axis=0, keepdims=True)
        gs_rows.append(m1 + m2)
    gs = jnp.concatenate(gs_rows, axis=0)
    ri = lax.broadcasted_iota(I32, (N_GROUPS, tn), 0)
    gmask = jnp.zeros((N_GROUPS, tn), jnp.bool_)
    for _ in range(TOPK_GROUPS):
        _, f = _first_argmax(gs, ri, N_GROUPS)
        hit = ri == f
        gmask = gmask | hit
        gs = jnp.where(hit, neg, gs)
    gm = jnp.where(gmask, 1.0, 0.0)
    emask = jnp.concatenate(
        [jnp.broadcast_to(gm[g:g + 1], (GROUP_SIZE, tn)) for g in range(N_GROUPS)], axis=0)
    masked = jnp.where(emask > 0.5, choice, neg)
    ei = lax.broadcasted_iota(I32, (n_e, tn), 0)
    idx_rows, gate_rows = [], []
    for _ in range(TOP_K):
        _, f = _first_argmax(masked, ei, n_e)
        hit = ei == f
        idx_rows.append(f)
        gate_rows.append(jnp.sum(jnp.where(hit, scores, 0.0), axis=0, keepdims=True))
        masked = jnp.where(hit, neg, masked)
    gate = jnp.concatenate(gate_rows, axis=0)
    gate = gate / jnp.sum(gate, axis=0, keepdims=True) * ROUTE_SCALE
    idx_ref[...] = jnp.concatenate(idx_rows, axis=0)
    gate_ref[...] = gate


def route_topk(logits_t, bias, tn=512):
    n_e, n = logits_t.shape
    return pl.pallas_call(
        _route_kernel,
        grid=(n // tn,),
        in_specs=[pl.BlockSpec((n_e, tn), lambda i: (0, i)), pl.BlockSpec((n_e, 1), lambda i: (0, 0))],
        out_specs=[pl.BlockSpec((TOP_K, tn), lambda i: (0, i)), pl.BlockSpec((TOP_K, tn), lambda i: (0, i))],
        out_shape=[jax.ShapeDtypeStruct((TOP_K, n), I32), jax.ShapeDtypeStruct((TOP_K, n), F32)],
        compiler_params=_cparams(("parallel",)),
        name="route_topk",
    )(logits_t, bias.reshape(n_e, 1))


def _rank_kernel(idx_ref, su_ref, rank_ref, cnt_ref, carry_s):
    n_e = cnt_ref.shape[0]
    tn = idx_ref.shape[1]

    @pl.when(pl.program_id(0) == 0)
    def _():
        carry_s[...] = jnp.zeros_like(carry_s)

    idx = idx_ref[...]
    ei = lax.broadcasted_iota(I32, (n_e, tn), 0)
    hits = [ei == idx[k:k + 1] for k in range(TOP_K)]
    member = jnp.zeros((n_e, tn), F32)
    for h in hits:
        member = member + jnp.where(h, 1.0, 0.0)
    prefix = jnp.dot(member.astype(BF16), su_ref[...], preferred_element_type=F32) + carry_s[:, 0:1]
    rank_ref[...] = jnp.concatenate(
        [jnp.sum(jnp.where(h, prefix, 0.0), axis=0, keepdims=True) for h in hits], axis=0).astype(I32)
    carry_s[...] = carry_s[...] + jnp.sum(member, axis=1, keepdims=True)
    cnt_ref[...] = carry_s[...].astype(I32)


def expert_ranks(idx_t, tn=512):
    k, n = idx_t.shape
    su = jnp.asarray(np.triu(np.ones((tn, tn), np.float32), 1), BF16)
    return pl.pallas_call(
        _rank_kernel,
        grid=(n // tn,),
        in_specs=[pl.BlockSpec((k, tn), lambda i: (0, i)), pl.BlockSpec((tn, tn), lambda i: (0, 0))],
        out_specs=[pl.BlockSpec((k, tn), lambda i: (0, i)), pl.BlockSpec((N_EXPERTS, LANES), lambda i: (0, 0))],
        out_shape=[jax.ShapeDtypeStruct((k, n), I32), jax.ShapeDtypeStruct((N_EXPERTS, LANES), I32)],
        scratch_shapes=[pltpu.VMEM((N_EXPERTS, LANES), F32)],
        compiler_params=_cparams(("arbitrary",)),
        name="expert_ranks",
    )(idx_t, su)


def _dest_kernel(idx_ref, rank_ref, start_ref, dest_ref):
    n_e = start_ref.shape[0]
    tn = idx_ref.shape[1]
    idx = idx_ref[...]
    ei = lax.broadcasted_iota(I32, (n_e, tn), 0)
    start = start_ref[...]
    base = jnp.concatenate(
        [jnp.sum(jnp.where(ei == idx[k:k + 1], start, 0.0), axis=0, keepdims=True) for k in range(TOP_K)],
        axis=0)
    dest_ref[...] = base.astype(I32) + rank_ref[...]


def assignment_rows(idx_t, rank_t, padded_start, tn=512):
    k, n = idx_t.shape
    blk = pl.BlockSpec((k, tn), lambda i: (0, i))
    return pl.pallas_call(
        _dest_kernel,
        grid=(n // tn,),
        in_specs=[blk, blk, pl.BlockSpec((N_EXPERTS, 1), lambda i: (0, 0))],
        out_specs=blk,
        out_shape=jax.ShapeDtypeStruct((k, n), I32),
        compiler_params=_cparams(("parallel",)),
        name="assignment_rows",
    )(idx_t, rank_t, padded_start.astype(F32).reshape(N_EXPERTS, 1))


WEIGHT_DMA_PRIORITY = 1
WEIGHT_SLOTS = 3
ROW_GROUP = 8
ROW_CHUNK = 8
BLOCKS_PER_STEP = 2


def _rowmap_kernel(dest_ref, tok_ref):
    n_tok = dest_ref.shape[0] // TOP_K

    def body(j, c):
        for k in range(TOP_K):
            tok_ref[dest_ref[k * n_tok + j]] = j
        return c

    lax.fori_loop(0, n_tok, body, 0, unroll=4)


def row_tokens(dest_flat, n_rows):
    return pl.pallas_call(
        _rowmap_kernel,
        grid_spec=pltpu.PrefetchScalarGridSpec(
            num_scalar_prefetch=1,
            grid=(1,),
            in_specs=[],
            out_specs=pl.BlockSpec(memory_space=pltpu.SMEM),
        ),
        out_shape=jax.ShapeDtypeStruct((n_rows,), I32),
        compiler_params=_cparams(("arbitrary",)),
        name="row_tokens",
    )(dest_flat)


def _experts_kernel(be_ref, nxt_ref, nxt2_ref, slot_ref, cnt_ref, tok_ref, nb_ref,
                    h1p_ref, wg_ref, wu_ref, wd_ref, ys_ref,
                    xbuf, wgb, wub, wdb, gsem, wsem):
    tm = ys_ref.shape[0] // (SUBLANES * BLOCKS_PER_STEP)
    step = pl.program_id(0)
    nb = nb_ref[0]

    def weight_copies(e, slot):
        return (pltpu.make_async_copy(wg_ref.at[e], wgb.at[slot], wsem.at[slot]),
                pltpu.make_async_copy(wu_ref.at[e], wub.at[slot], wsem.at[slot]),
                pltpu.make_async_copy(wd_ref.at[e], wdb.at[slot], wsem.at[slot]))

    def start_weights(e, slot):
        for cp in weight_copies(e, slot):
            cp.start(priority=WEIGHT_DMA_PRIORITY)

    def start_group(grp, slot):
        def one_block(g, carry):
            blk = jnp.minimum(grp * ROW_GROUP + g, nb - 1)
            cnt = cnt_ref[blk]
            last = cnt - 1
            base = blk * tm
            n_chunks = lax.div(cnt + (ROW_CHUNK - 1), ROW_CHUNK)

            def issue(ch, c):
                for j in range(ROW_CHUNK):
                    r = ch * ROW_CHUNK + j
                    t = tok_ref[base + jnp.minimum(r, last)]
                    pltpu.make_async_copy(h1p_ref.at[pl.ds(t * SUBLANES, SUBLANES)],
                                          xbuf.at[slot, pl.ds((g * tm + r) * SUBLANES, SUBLANES)],
                                          gsem.at[slot]).start()
                return c

            lax.fori_loop(0, n_chunks, issue, 0)
            done = n_chunks * ROW_CHUNK
            pad = tm - done
            size = tm // 2
            while size >= ROW_CHUNK:
                @pl.when((pad & size) != 0)
                def _(done=done, size=size):
                    pltpu.make_async_copy(
                        h1p_ref.at[pl.ds(0, size * SUBLANES)],
                        xbuf.at[slot, pl.ds((g * tm + done) * SUBLANES, size * SUBLANES)],
                        gsem.at[slot]).start()
                done = done + jnp.where((pad & size) != 0, size, 0)
                size //= 2
            return carry

        lax.fori_loop(0, ROW_GROUP, one_block, 0)

    def process(i, out_ref):
        e = be_ref[i]
        wslot = slot_ref[i]
        grp = lax.div(i, ROW_GROUP)
        sub = lax.rem(i, ROW_GROUP)
        gslot = lax.rem(grp, 2)

        @pl.when(i == 0)
        def _():
            start_weights(e, wslot)
            nxt = nxt_ref[0]

            @pl.when(nxt >= 0)
            def _():
                start_weights(nxt, lax.rem(wslot + 1, WEIGHT_SLOTS))

            start_group(0, 0)

        @pl.when((i == 0) | (e != be_ref[jnp.maximum(i - 1, 0)]))
        def _():
            for cp in weight_copies(e, wslot):
                cp.wait()
            nxt2 = nxt2_ref[i]

            @pl.when(nxt2 >= 0)
            def _():
                start_weights(nxt2, lax.rem(wslot + 2, WEIGHT_SLOTS))

        @pl.when(sub == 0)
        def _():
            pltpu.make_async_copy(h1p_ref.at[pl.ds(0, ROW_GROUP * tm * SUBLANES)], xbuf.at[gslot],
                                  gsem.at[gslot]).wait()

            @pl.when((grp + 1) * ROW_GROUP < nb)
            def _():
                start_group(grp + 1, 1 - gslot)

        row0 = pl.multiple_of(sub * (tm * SUBLANES), tm * SUBLANES)
        x = _unpack_bf16_pair(_load_row_tiles(xbuf.at[gslot, pl.ds(row0, tm * SUBLANES)], tm))
        a = jnp.dot(x, wgb[wslot].astype(BF16), preferred_element_type=F32)
        u = jnp.dot(x, wub[wslot].astype(BF16), preferred_element_type=F32)
        hid = (jax.nn.silu(a) * u).astype(BF16)
        y = jnp.dot(hid, wdb[wslot].astype(BF16), preferred_element_type=F32)
        _store_row_tiles(out_ref, _pack_bf16_pair(y))

    for h in range(BLOCKS_PER_STEP):
        blk = step * BLOCKS_PER_STEP + h

        @pl.when(blk < nb)
        def _(blk=blk, h=h):
            process(blk, ys_ref.at[pl.ds(h * tm * SUBLANES, tm * SUBLANES)])


def grouped_experts(h1p_tiles, row_tok, block_expert, block_next, block_next2, block_slot, block_count,
                    n_blocks_used, w_gate, w_up, w_down, tm=MOE_BLOCK):
    n_rows = row_tok.shape[0]
    n_e, d, ff = w_gate.shape
    n_blocks = n_rows // tm
    assert n_blocks % BLOCKS_PER_STEP == 0
    bps = BLOCKS_PER_STEP
    rowmap = lambda i, be, nx, nx2, sl, cn, tok, nb: (jnp.minimum(i, jnp.maximum(nb[0] - 1, 0) // bps), 0)
    hbm = pl.BlockSpec(memory_space=pl.ANY)
    return pl.pallas_call(
        _experts_kernel,
        grid_spec=pltpu.PrefetchScalarGridSpec(
            num_scalar_prefetch=7,
            grid=(n_blocks // bps,),
            in_specs=[hbm, hbm, hbm, hbm],
            out_specs=pl.BlockSpec((bps * tm * SUBLANES, LANES), rowmap),
            scratch_shapes=[pltpu.VMEM((2, ROW_GROUP * tm * SUBLANES, LANES), U32),
                            pltpu.VMEM((WEIGHT_SLOTS, d, ff), F32), pltpu.VMEM((WEIGHT_SLOTS, d, ff), F32),
                            pltpu.VMEM((WEIGHT_SLOTS, ff, d), F32),
                            pltpu.SemaphoreType.DMA((2,)), pltpu.SemaphoreType.DMA((WEIGHT_SLOTS,))],
        ),
        out_shape=jax.ShapeDtypeStruct((n_rows * SUBLANES, LANES), U32),
        compiler_params=_cparams(("arbitrary",)),
        name="grouped_experts",
    )(block_expert, block_next, block_next2, block_slot, block_count, row_tok, n_blocks_used,
      h1p_tiles, w_gate, w_up, w_down)


def _combine_kernel(dest_ref, h1_ref, h1p_ref, gate_ref, wsg_ref, wsu_ref, wsd_ref, g2_ref, b2_ref, ys_ref,
                    o_ref, buf_s, sem):
    tm = h1_ref.shape[0]
    i = pl.program_id(0)
    n_steps = pl.num_programs(0)
    n_tok = n_steps * tm
    slot = lax.rem(i, 2)

    def start_rows(step, sl):
        def issue(r, c):
            for k in range(TOP_K):
                row = dest_ref[k * n_tok + step * tm + r]
                pltpu.make_async_copy(ys_ref.at[pl.ds(row * SUBLANES, SUBLANES)],
                                      buf_s.at[sl, k, pl.ds(r * SUBLANES, SUBLANES)], sem.at[sl]).start()
            return c

        lax.fori_loop(0, tm, issue, 0, unroll=2)

    @pl.when(i == 0)
    def _():
        start_rows(0, 0)

    @pl.when(i + 1 < n_steps)
    def _():
        start_rows(i + 1, 1 - slot)

    xb = _unpack_bf16_pair(_load_row_tiles(h1p_ref, tm))
    a = jnp.dot(xb, wsg_ref[...].astype(BF16), preferred_element_type=F32)
    u = jnp.dot(xb, wsu_ref[...].astype(BF16), preferred_element_type=F32)
    shared = jnp.dot((jax.nn.silu(a) * u).astype(BF16), wsd_ref[...].astype(BF16),
                     preferred_element_type=F32)

    for k in range(TOP_K):
        pltpu.make_async_copy(ys_ref.at[pl.ds(0, tm * SUBLANES)], buf_s.at[slot, k], sem.at[slot]).wait()

    gate = gate_ref[...]
    routed = jnp.zeros_like(shared)
    for k in range(TOP_K):
        routed = routed + _unpack_pair_f32(_load_row_tiles(buf_s.at[slot, k], tm)) * gate[:, k:k + 1]
    o_ref[...] = _layer_norm(DN_ALPHA * h1_ref[...] + routed + shared, g2_ref[...], b2_ref[...])


def combine_shared_ln(dest_flat, h1, h1p_tiles, gate_nk, ys_tiles, wsg, wsu, wsd, g2, b2, tm=128):
    n, d = h1.shape
    ff = wsg.shape[1]
    row = lambda i, dest: (i, 0)
    const = lambda i, dest: (0, 0)
    return pl.pallas_call(
        _combine_kernel,
        grid_spec=pltpu.PrefetchScalarGridSpec(
            num_scalar_prefetch=1,
            grid=(n // tm,),
            in_specs=[pl.BlockSpec((tm, d), row), pl.BlockSpec((tm * SUBLANES, LANES), row),
                      pl.BlockSpec((tm, TOP_K), row),
                      pl.BlockSpec((d, ff), const), pl.BlockSpec((d, ff), const), pl.BlockSpec((ff, d), const),
                      pl.BlockSpec((1, d), const), pl.BlockSpec((1, d), const),
                      pl.BlockSpec(memory_space=pl.ANY)],
            out_specs=pl.BlockSpec((tm, d), row),
            scratch_shapes=[pltpu.VMEM((2, TOP_K, tm * SUBLANES, LANES), U32), pltpu.SemaphoreType.DMA((2,))],
        ),
        out_shape=jax.ShapeDtypeStruct((n, d), F32),
        compiler_params=_cparams(("arbitrary",)),
        name="combine_shared_ln",
    )(dest_flat, h1, h1p_tiles, gate_nk, wsg, wsu, wsd, g2.reshape(1, d), b2.reshape(1, d), ys_tiles)


def kernel(x, ln_emb_g, ln_emb_b, w_in, lb_logits, hg_norm_w, conv_w, conv_b, rg_wa, rg_ba, rg_wx, rg_bx,
           rg_lam, w_out, ln1_g, ln1_b, w_router, router_bias, w_gate, w_up, w_down, ws_gate, ws_up, ws_down,
           ln2_g, ln2_b):
    batch, seq, d = x.shape
    assert d == 2 * SUBLANES * LANES, "packed rows must fill one 8x128 tile"
    n_tok = batch * seq
    layer = 0
    x2 = x.reshape(n_tok, d)
    lb = jnp.cumsum(jax.nn.softmax(lb_logits.astype(F32), axis=1), axis=1)

    proj = ln_inproj(x2, ln_emb_g, ln_emb_b, w_in[layer])
    y_hg = hgrn2_mixer(proj, lb[0, layer], lb[1, layer], hg_norm_w[layer], batch, seq)
    y_rg = rglru_mixer(proj, conv_w[layer], conv_b[layer], rg_wa[layer], rg_ba[layer], rg_wx[layer],
                       rg_bx[layer], rg_lam[layer], batch, seq)
    h1, h1p_tiles, logits_t = outproj_ln_router(
        y_hg, y_rg, x2, ln_emb_g, ln_emb_b, w_out[layer], ln1_g[layer], ln1_b[layer],
        w_router[layer].T.astype(BF16))

    idx_t, gate_t = route_topk(logits_t, router_bias[layer])
    rank_t, counts = expert_ranks(idx_t)
    counts = counts[:, 0]

    n_assign = n_tok * TOP_K
    n_blocks = -(-(n_assign + N_EXPERTS * (MOE_BLOCK - 1)) // MOE_BLOCK)
    n_rows = n_blocks * MOE_BLOCK
    padded = (counts + MOE_BLOCK - 1) // MOE_BLOCK * MOE_BLOCK
    padded_end = jnp.cumsum(padded)
    padded_start = padded_end - padded
    n_used_blocks = (padded_end[-1:] // MOE_BLOCK).astype(I32)
    blk_row = jnp.minimum(jnp.arange(n_blocks, dtype=I32), n_used_blocks[0] - 1) * MOE_BLOCK
    block_expert = jnp.minimum(
        jnp.sum((padded_end[None, :] <= blk_row[:, None]).astype(I32), axis=1), N_EXPERTS - 1)
    e_ids = jnp.arange(N_EXPERTS, dtype=I32)
    active = counts > 0
    later = lax.cummin(jnp.where(active, e_ids, N_EXPERTS)[::-1])[::-1]
    nxt = jnp.concatenate([later[1:], jnp.full((1,), N_EXPERTS, I32)])
    next1 = jnp.where(nxt < N_EXPERTS, nxt, -1).astype(I32)
    hot1 = (next1[:, None] == e_ids[None, :]).astype(I32)
    next2 = jnp.sum(hot1 * (next1 + 1)[None, :], axis=1) - 1
    ordinal = jnp.cumsum(active.astype(I32)) - 1
    onehot = (block_expert[:, None] == e_ids[None, :]).astype(I32)
    block_next = jnp.sum(onehot * next1[None, :], axis=1)
    block_next2 = jnp.sum(onehot * next2[None, :], axis=1)
    block_slot = jnp.sum(onehot * (ordinal % WEIGHT_SLOTS)[None, :], axis=1)
    block_count = jnp.clip(jnp.sum(onehot * (counts + padded_start)[None, :], axis=1) - blk_row, 1, MOE_BLOCK)

    dest_flat = assignment_rows(idx_t, rank_t, padded_start).reshape(-1)
    row_tok = row_tokens(dest_flat, n_rows)

    ys_tiles = grouped_experts(h1p_tiles, row_tok, block_expert, block_next, block_next2, block_slot,
                               block_count, n_used_blocks, w_gate[layer], w_up[layer], w_down[layer])
    out = combine_shared_ln(dest_flat, h1, h1p_tiles, gate_t.T, ys_tiles,
                            ws_gate[layer], ws_up[layer], ws_down[layer], ln2_g[layer], ln2_b[layer])
    return out.reshape(batch, seq, d)
```
